```python
import math
import jax, jax.numpy as jnp
from jax import lax
import numpy as np

D_MODEL = 1024
BATCH = 2
SEQ = 8192
DEPTH = 2
DEC_BATCH = 8
DEC_SEQ = 64
PAST_LEN = 4096

CHUNK = 64
N_EVEN = (DEPTH + 1) // 2
N_ODD = DEPTH // 2
D_SSM = D_MODEL // 2
SSM_GROUP = 16
SSM_GROUPS = D_SSM // SSM_GROUP
SSM_STATE = 64
D_ATTN = D_MODEL // 2
HEAD_DIM = 64
N_HEADS = D_ATTN // HEAD_DIM
LEFT_CHUNKS = 8
BAND_CHUNKS = LEFT_CHUNKS + 1
BAND_ROWS = LEFT_CHUNKS * CHUNK
MAX_REL = 128
ATTN_SCALE = HEAD_DIM ** -0.5
CONV_CH = D_MODEL
CONV_WIDTH = 31
MOE_GROUPS = 4
EXPERTS_PER_GROUP = 8
N_EXPERTS = MOE_GROUPS * EXPERTS_PER_GROUP
TOP_K_IN_GROUP = 2
D_EXPERT = D_MODEL // 2
EPS = 1e-6
NEG_INF = -1e30

kernel_name = "hybrid_s5_chunkattn_conformer_hmoe_step"


def rms_norm(x, g):
    xf = x.astype(jnp.float32)
    y = xf * lax.rsqrt(jnp.mean(xf * xf, axis=-1, keepdims=True) + EPS)
    return (y * g.astype(jnp.float32)).astype(x.dtype)


def layer_norm(x, g, b):
    xf = x.astype(jnp.float32)
    mu = jnp.mean(xf, axis=-1, keepdims=True)
    var = jnp.mean(jnp.square(xf - mu), axis=-1, keepdims=True)
    y = (xf - mu) * lax.rsqrt(var + EPS)
    return (y * g.astype(jnp.float32) + b.astype(jnp.float32)).astype(x.dtype)


def s5_mixer(u, h0_re, h0_im, a_re, a_im, log_dt, b_re, b_im, c_re, c_im, d, glu_w, glu_b):
    f32 = jnp.float32
    bsz, L = u.shape[0], u.shape[1]
    ug = u.astype(f32).reshape(bsz, L, SSM_GROUPS, SSM_GROUP)
    lam_re, lam_im = a_re.astype(f32), a_im.astype(f32)
    dt = jnp.exp(log_dt.astype(f32))[:, None]
    mag = jnp.exp(lam_re * dt)
    bar_re, bar_im = mag * jnp.cos(lam_im * dt), mag * jnp.sin(lam_im * dt)
    den = lam_re * lam_re + lam_im * lam_im
    num_re = bar_re - 1.0
    coef_re = (num_re * lam_re + bar_im * lam_im) / den
    coef_im = (bar_im * lam_re - num_re * lam_im) / den
    br, bi = b_re.astype(f32), b_im.astype(f32)
    bb_re = coef_re[..., None] * br - coef_im[..., None] * bi
    bb_im = coef_re[..., None] * bi + coef_im[..., None] * br
    bu_re = jnp.einsum("blgc,gpc->blgp", ug, bb_re)
    bu_im = jnp.einsum("blgc,gpc->blgp", ug, bb_im)
    h0r, h0i = h0_re.astype(f32), h0_im.astype(f32)
    bu_re = bu_re.at[:, 0].add(bar_re * h0r - bar_im * h0i)
    bu_im = bu_im.at[:, 0].add(bar_re * h0i + bar_im * h0r)
    a_b_re = jnp.broadcast_to(bar_re, bu_re.shape)
    a_b_im = jnp.broadcast_to(bar_im, bu_im.shape)

    def combine(e1, e2):
        a1r, a1i, b1r, b1i = e1
        a2r, a2i, b2r, b2i = e2
        return (a2r * a1r - a2i * a1i, a2r * a1i + a2i * a1r,
                a2r * b1r - a2i * b1i + b2r, a2r * b1i + a2i * b1r + b2i)

    _, _, h_re, h_im = lax.associative_scan(combine, (a_b_re, a_b_im, bu_re, bu_im), axis=1)
    y = (jnp.einsum("gcp,blgp->blgc", c_re.astype(f32), h_re)
         - jnp.einsum("gcp,blgp->blgc", c_im.astype(f32), h_im)
         + d.astype(f32) * ug)
    z = jax.nn.gelu(y)
    gate = jax.nn.sigmoid(jnp.einsum("blgc,gce->blge", z, glu_w.astype(f32)) + glu_b.astype(f32))
    out = (z * gate).reshape(bsz, L, D_SSM).astype(u.dtype)
    return out, h_re[:, -1], h_im[:, -1]


def rel_bias(table, dist):
    idx = jnp.clip(dist, -MAX_REL, MAX_REL) + MAX_REL
    return table[:, idx].astype(jnp.float32)


def band_attn_prompt(q, k, v, table):
    bsz, L = q.shape[0], q.shape[1]
    nc = L // CHUNK

    def chunks(t):
        return t.reshape(bsz, nc, CHUNK, N_HEADS, HEAD_DIM).transpose(0, 3, 1, 2, 4)

    qc, kc, vc = chunks(q), chunks(k), chunks(v)
    pad = ((0, 0), (0, 0), (LEFT_CHUNKS, 0), (0, 0), (0, 0))
    kp, vp = jnp.pad(kc, pad), jnp.pad(vc, pad)
    k_band = jnp.concatenate([kp[:, :, w:w + nc] for w in range(BAND_CHUNKS)], axis=3)
    v_band = jnp.concatenate([vp[:, :, w:w + nc] for w in range(BAND_CHUNKS)], axis=3)
    s = jnp.einsum("bhnqd,bhnkd->bhnqk", qc, k_band).astype(jnp.float32) * ATTN_SCALE
    q_off = jnp.arange(CHUNK)
    k_off = jnp.arange(BAND_CHUNKS * CHUNK)
    dist = q_off[:, None] + BAND_ROWS - k_off[None, :]
    s = s + rel_bias(table, dist)[None, :, None]
    key_pos = jnp.arange(nc)[:, None] * CHUNK - BAND_ROWS + k_off[None, :]
    s = jnp.where((key_pos >= 0)[None, None, :, None, :], s, NEG_INF)
    p = jax.nn.softmax(s, axis=-1).astype(v.dtype)
    o = jnp.einsum("bhnqk,bhnkd->bhnqd", p, v_band)
    o = o.transpose(0, 2, 3, 1, 4).reshape(bsz, L, D_ATTN)
    keep = min(BAND_ROWS, L)
    return o, k[:, L - keep:].transpose(0, 2, 1, 3), v[:, L - keep:].transpose(0, 2, 1, 3)


def band_attn_sample(q, k, v, cache_k, cache_v, table):
    bsz, s_len = q.shape[0], q.shape[1]
    lc = cache_k.shape[2]
    qh, kh, vh = q.transpose(0, 2, 1, 3), k.transpose(0, 2, 1, 3), v.transpose(0, 2, 1, 3)
    keys = jnp.concatenate([cache_k.astype(kh.dtype), kh], axis=2)
    vals = jnp.concatenate([cache_v.astype(vh.dtype), vh], axis=2)
    s = jnp.einsum("bhqd,bhkd->bhqk", qh, keys).astype(jnp.float32) * ATTN_SCALE
    dist = (lc + jnp.arange(s_len))[:, None] - jnp.arange(lc + s_len)[None, :]
    s = s + rel_bias(table, dist)[None]
    p = jax.nn.softmax(s, axis=-1).astype(vals.dtype)
    o = jnp.einsum("bhqk,bhkd->bhqd", p, vals).transpose(0, 2, 1, 3).reshape(bsz, s_len, D_ATTN)
    return o, kh, vh


def conformer_conv(xn, left, w_in, dw_w, dw_b, ln_g, ln_b, w_out):
    h = xn @ w_in
    a, g = jnp.split(h, 2, axis=-1)
    u = a * jax.nn.sigmoid(g)
    ext = jnp.concatenate([left.astype(u.dtype), u], axis=1)
    y = lax.conv_general_dilated(ext, dw_w[:, None, :].astype(u.dtype), window_strides=(1,), padding="VALID",
                                 dimension_numbers=("NWC", "WIO", "NWC"), feature_group_count=CONV_CH)
    y = layer_norm(y + dw_b, ln_g, ln_b)
    out = jax.nn.silu(y) @ w_out
    return out, ext[:, ext.shape[1] - (CONV_WIDTH - 1):]


def hier_moe(x, gate_w, gate_b, router_w, router_b, w_gate_up, w_down):
    bsz, L, d = x.shape
    t = x.reshape(bsz * L, d)
    g_logits = (t @ gate_w + gate_b).astype(jnp.float32)
    g_prob = jax.nn.softmax(g_logits, axis=-1)
    g_top, g_idx = lax.top_k(g_logits, 1)
    g_w = jnp.take_along_axis(g_prob, g_idx, axis=-1)
    e_all = jnp.einsum("td,gde->tge", t, router_w) + router_b
    e_logits = jnp.take_along_axis(e_all, g_idx[:, :, None], axis=1)[:, 0].astype(jnp.float32)
    top_v, top_i = lax.top_k(e_logits, TOP_K_IN_GROUP)
    top_w = jax.nn.softmax(top_v, axis=-1)
    within = jnp.sum(top_w[..., None] * jax.nn.one_hot(top_i, EXPERTS_PER_GROUP, dtype=jnp.float32), axis=1)
    grp = jax.nn.one_hot(g_idx[:, 0], MOE_GROUPS, dtype=jnp.float32)
    gate = (grp[:, :, None] * within[:, None, :]).reshape(-1, N_EXPERTS) * g_w
    y = jnp.zeros((t.shape[0], d), jnp.float32)
    for e in range(N_EXPERTS):
        gt, up = jnp.split(t @ w_gate_up[e], 2, axis=-1)
        y = y + gate[:, e:e + 1] * ((jax.nn.silu(gt) * up) @ w_down[e]).astype(jnp.float32)
    return y.reshape(bsz, L, d).astype(x.dtype)


def setup_inputs(seed: int = 0) -> dict:
    key = jax.random.key(seed)
    ks = jax.random.split(key, 40)
    f32 = jnp.float32
    nrm = lambda k, shape, s: jax.random.normal(k, shape, f32) * s
    lc = min(BAND_ROWS, PAST_LEN)
    n_idx = jnp.arange(SSM_STATE, dtype=f32)
    return {
        "x_prompt": nrm(ks[0], (BATCH, SEQ, D_MODEL), 1.0),
        "x_sample": nrm(ks[1], (DEC_BATCH, DEC_SEQ, D_MODEL), 1.0),
        "cache_attn_k": nrm(ks[2], (N_EVEN, DEC_BATCH, N_HEADS, lc, HEAD_DIM), 1.0),
        "cache_attn_v": nrm(ks[3], (N_EVEN, DEC_BATCH, N_HEADS, lc, HEAD_DIM), 1.0),
        "state_ssm_re": nrm(ks[4], (N_EVEN, DEC_BATCH, SSM_GROUPS, SSM_STATE), 0.5),
        "state_ssm_im": nrm(ks[5], (N_EVEN, DEC_BATCH, SSM_GROUPS, SSM_STATE), 0.5),
        "cache_conv": nrm(ks[6], (N_ODD, DEC_BATCH, CONV_WIDTH - 1, CONV_CH), 0.5),
        "norm_mixer": 1.0 + nrm(ks[7], (DEPTH, D_MODEL), 0.02),
        "norm_ffn": 1.0 + nrm(ks[8], (DEPTH, D_MODEL), 0.02),
        "w_in_even": nrm(ks[9], (N_EVEN, D_MODEL, D_SSM + 3 * D_ATTN), D_MODEL ** -0.5),
        "w_out_even": nrm(ks[10], (N_EVEN, D_SSM + D_ATTN, D_MODEL), (D_SSM + D_ATTN) ** -0.5),
        "ssm_a_re": -0.5 + nrm(ks[11], (N_EVEN, SSM_GROUPS, SSM_STATE), 0.01),
        "ssm_a_im": math.pi * n_idx + nrm(ks[12], (N_EVEN, SSM_GROUPS, SSM_STATE), 0.01),
        "ssm_log_dt": jax.random.uniform(ks[13], (N_EVEN, SSM_GROUPS), f32, math.log(1e-3), math.log(1e-1)),
        "ssm_b_re": nrm(ks[14], (N_EVEN, SSM_GROUPS, SSM_STATE, SSM_GROUP), (2 * SSM_GROUP) ** -0.5),
        "ssm_b_im": nrm(ks[15], (N_EVEN, SSM_GROUPS, SSM_STATE, SSM_GROUP), (2 * SSM_GROUP) ** -0.5),
        "ssm_c_re": nrm(ks[16], (N_EVEN, SSM_GROUPS, SSM_GROUP, SSM_STATE), SSM_STATE ** -0.5),
        "ssm_c_im": nrm(ks[17], (N_EVEN, SSM_GROUPS, SSM_GROUP, SSM_STATE), SSM_STATE ** -0.5),
        "ssm_d": nrm(ks[18], (N_EVEN, SSM_GROUPS, SSM_GROUP), 0.5),
        "ssm_glu_w": nrm(ks[19], (N_EVEN, SSM_GROUPS, SSM_GROUP, SSM_GROUP), SSM_GROUP ** -0.5),
        "ssm_glu_b": nrm(ks[20], (N_EVEN, SSM_GROUPS, SSM_GROUP), 0.01),
        "attn_q_gain": 1.0 + nrm(ks[21], (N_EVEN, HEAD_DIM), 0.02),
        "attn_k_gain": 1.0 + nrm(ks[22], (N_EVEN, HEAD_DIM), 0.02),
        "attn_rel_bias": nrm(ks[23], (N_EVEN, N_HEADS, 2 * MAX_REL + 1), 0.1),
        "conv_w_in": nrm(ks[24], (N_ODD, D_MODEL, 2 * CONV_CH), D_MODEL ** -0.5),
        "conv_dw_w": nrm(ks[25], (N_ODD, CONV_WIDTH, CONV_CH), CONV_WIDTH ** -0.5),
        "conv_dw_b": nrm(ks[26], (N_ODD, CONV_CH), 0.01),
        "conv_ln_g": 1.0 + nrm(ks[27], (N_ODD, CONV_CH), 0.02),
        "conv_ln_b": nrm(ks[28], (N_ODD, CONV_CH), 0.01),
        "conv_w_out": nrm(ks[29], (N_ODD, CONV_CH, D_MODEL), CONV_CH ** -0.5),
        "moe_gate_w": nrm(ks[30], (DEPTH, D_MODEL, MOE_GROUPS), D_MODEL ** -0.5),
        "moe_gate_b": nrm(ks[31], (DEPTH, MOE_GROUPS), 0.01),
        "moe_router_w": nrm(ks[32], (DEPTH, MOE_GROUPS, D_MODEL, EXPERTS_PER_GROUP), D_MODEL ** -0.5),
        "moe_router_b": nrm(ks[33], (DEPTH, MOE_GROUPS, EXPERTS_PER_GROUP), 0.01),
        "moe_w_gate_up": nrm(ks[34], (DEPTH, N_EXPERTS, D_MODEL, 2 * D_EXPERT), D_MODEL ** -0.5),
        "moe_w_down": nrm(ks[35], (DEPTH, N_EXPERTS, D_EXPERT, D_MODEL), D_EXPERT ** -0.5),
    }


def reference(x_prompt, x_sample, cache_attn_k, cache_attn_v, state_ssm_re, state_ssm_im, cache_conv,
              norm_mixer, norm_ffn, w_in_even, w_out_even,
              ssm_a_re, ssm_a_im, ssm_log_dt, ssm_b_re, ssm_b_im, ssm_c_re, ssm_c_im, ssm_d, ssm_glu_w, ssm_glu_b,
              attn_q_gain, attn_k_gain, attn_rel_bias,
              conv_w_in, conv_dw_w, conv_dw_b, conv_ln_g, conv_ln_b, conv_w_out,
              moe_gate_w, moe_gate_b, moe_router_w, moe_router_b, moe_w_gate_up, moe_w_down):

    def even_mixer(xn, j, kv_cache, h0_re, h0_im):
        bsz, L = xn.shape[0], xn.shape[1]
        proj = xn @ w_in_even[j]
        u, q, k, v = jnp.split(proj, [D_SSM, D_SSM + D_ATTN, D_SSM + 2 * D_ATTN], axis=-1)
        y_ssm, h_re, h_im = s5_mixer(u, h0_re, h0_im, ssm_a_re[j], ssm_a_im[j], ssm_log_dt[j],
                                     ssm_b_re[j], ssm_b_im[j], ssm_c_re[j], ssm_c_im[j], ssm_d[j],
                                     ssm_glu_w[j], ssm_glu_b[j])
        q = rms_norm(q.reshape(bsz, L, N_HEADS, HEAD_DIM), attn_q_gain[j])
        k = rms_norm(k.reshape(bsz, L, N_HEADS, HEAD_DIM), attn_k_gain[j])
        v = v.reshape(bsz, L, N_HEADS, HEAD_DIM)
        if kv_cache is None:
            y_att, k_new, v_new = band_attn_prompt(q, k, v, attn_rel_bias[j])
        else:
            y_att, k_new, v_new = band_attn_sample(q, k, v, kv_cache[0], kv_cache[1], attn_rel_bias[j])
        y = jnp.concatenate([y_ssm, y_att.astype(y_ssm.dtype)], axis=-1) @ w_out_even[j]
        return y, k_new, v_new, h_re, h_im

    def trunk(x, sample):
        bsz = x.shape[0]
        ks_new, vs_new, hre_new, him_new, conv_new = [], [], [], [], []
        for i in range(DEPTH):
            j = i // 2
            xn = rms_norm(x, norm_mixer[i])
            if i % 2 == 0:
                if sample:
                    kv = (cache_attn_k[j], cache_attn_v[j])
                    h0r, h0i = state_ssm_re[j], state_ssm_im[j]
                else:
                    kv = None
                    h0r = jnp.zeros((bsz, SSM_GROUPS, SSM_STATE), jnp.float32)
                    h0i = jnp.zeros((bsz, SSM_GROUPS, SSM_STATE), jnp.float32)
                y, k_new, v_new, h_re, h_im = even_mixer(xn, j, kv, h0r, h0i)
                ks_new.append(k_new)
                vs_new.append(v_new)
                hre_new.append(h_re)
                him_new.append(h_im)
            else:
                left = cache_conv[j] if sample else jnp.zeros((bsz, CONV_WIDTH - 1, CONV_CH), x.dtype)
                y, buf = conformer_conv(xn, left, conv_w_in[j], conv_dw_w[j], conv_dw_b[j],
                                        conv_ln_g[j], conv_ln_b[j], conv_w_out[j])
                conv_new.append(buf)
            x = x + y.astype(x.dtype)
            x = x + hier_moe(rms_norm(x, norm_ffn[i]), moe_gate_w[i], moe_gate_b[i], moe_router_w[i],
                             moe_router_b[i], moe_w_gate_up[i], moe_w_down[i])
        return (x, jnp.stack(ks_new), jnp.stack(vs_new), jnp.stack(hre_new), jnp.stack(him_new),
                jnp.stack(conv_new))

    y_prompt, k_p, v_p, hre_p, him_p, conv_p = trunk(x_prompt, False)
    y_sample, k_s, v_s, hre_s, him_s, conv_s = trunk(x_sample, True)
    return (y_prompt, y_sample, k_p, v_p, hre_p, him_p, conv_p, k_s, v_s, hre_s, him_s, conv_s)
```

```python
import functools
import math

import jax
import jax.numpy as jnp
from jax import lax
from jax.experimental import pallas as pl
from jax.experimental.pallas import tpu as pltpu

F32, BF16, I32, U32 = jnp.float32, jnp.bfloat16, jnp.int32, jnp.uint32

CHUNK = 64
SSM_GROUP = 16
SSM_STATE = 64
HEAD_DIM = 64
LEFT_CHUNKS = 8
BAND_ROWS = LEFT_CHUNKS * CHUNK
MAX_REL = 128
CONV_WIDTH = 31
MOE_GROUPS = 4
EXPERTS_PER_GROUP = 8
N_EXPERTS = MOE_GROUPS * EXPERTS_PER_GROUP
EPS = 1e-6
NEG_INF = -1e30

LANES = 128
SUBLANES = 8
VMEM_LIMIT_BYTES = 56 * 1024 * 1024

TM = 512
TM_MOE = 256
QBLK = 2 * CHUNK
KWIN = QBLK + BAND_ROWS
CONV_HALO = 32
CONV_ROWS = 16


def _cparams(n_grid):
    return pltpu.CompilerParams(dimension_semantics=("arbitrary",) * n_grid, vmem_limit_bytes=VMEM_LIMIT_BYTES)


def _sigmoid(x):
    return 1.0 / (1.0 + jnp.exp(-x))


def _rms(x, g):
    return x * lax.rsqrt(jnp.mean(x * x, axis=-1, keepdims=True) + EPS) * g


def _split_bf16(x):
    hi = lax.bitcast_convert_type(lax.bitcast_convert_type(x, U32) & jnp.uint32(0xFFFF0000), F32)
    return hi.astype(BF16), (x - hi).astype(BF16)


def _pair_specs(tm, width, n_prompt_blocks):
    return [pl.BlockSpec((tm, width), lambda i: (jnp.minimum(i, n_prompt_blocks - 1), 0)),
            pl.BlockSpec((tm, width), lambda i: (jnp.maximum(i - n_prompt_blocks, 0), 0))]


def _const_spec(shape):
    nd = len(shape)
    return pl.BlockSpec(shape, lambda *a: (0,) * nd)


def _inproj_body(xp_ref, xs_ref, g_ref, w_ref, qg_ref, kg_ref, seg_ref, u_ref, q_ref, k_ref, v_ref, *, n_pb, d_half):
    i = pl.program_id(0)
    x = jnp.where(i < n_pb, xp_ref[...], xs_ref[...])
    xn = _rms(x, g_ref[...]).astype(BF16)
    proj = jnp.dot(xn, w_ref[...], preferred_element_type=F32)
    u_ref[...] = proj[:, :d_half]
    v_ref[...] = proj[:, 3 * d_half:]

    def head_norm(t, gain):
        hi, lo = _split_bf16(t * t)
        msq = (jnp.dot(hi, seg_ref[...], preferred_element_type=F32)
               + jnp.dot(lo, seg_ref[...], preferred_element_type=F32))
        return t * lax.rsqrt(msq + EPS) * gain

    q_ref[...] = (head_norm(proj[:, d_half:2 * d_half], qg_ref[...]) * (HEAD_DIM ** -0.5)).astype(BF16)
    k_ref[...] = head_norm(proj[:, 2 * d_half:3 * d_half], kg_ref[...])


def _inproj(xp, xs, g, w, qg, kg):
    t_p, d = xp.shape
    t_s = xs.shape[0]
    d_half = w.shape[1] // 4
    n_pb = t_p // TM
    n_tot = (t_p + t_s) // TM
    heads = d_half // HEAD_DIM
    seg = jnp.kron(jnp.eye(heads, dtype=F32), jnp.full((HEAD_DIM, HEAD_DIM), 1.0 / HEAD_DIM, F32)).astype(BF16)
    out_f = jax.ShapeDtypeStruct((t_p + t_s, d_half), F32)
    out_b = jax.ShapeDtypeStruct((t_p + t_s, d_half), BF16)
    row = pl.BlockSpec((TM, d_half), lambda i: (i, 0))
    return pl.pallas_call(
        functools.partial(_inproj_body, n_pb=n_pb, d_half=d_half),
        grid=(n_tot,),
        in_specs=_pair_specs(TM, d, n_pb) + [_const_spec((1, d)), _const_spec(w.shape), _const_spec((1, d_half)),
                                             _const_spec((1, d_half)), _const_spec(seg.shape)],
        out_specs=[row, row, row, row],
        out_shape=[out_f, out_b, out_f, out_f],
        compiler_params=_cparams(1),
        name="inproj_even",
    )(xp, xs, g.reshape(1, d), w.astype(BF16), jnp.tile(qg, heads).reshape(1, d_half),
      jnp.tile(kg, heads).reshape(1, d_half), seg)


def _ssm_disc_body(are_ref, aim_ref, ldt_ref, bre_ref, bim_ref, barre_ref, barim_ref, bbre_ref, bbim_ref):
    lam_re, lam_im = are_ref[...], aim_ref[...]
    dt = jnp.exp(ldt_ref[...])
    mag = jnp.exp(lam_re * dt)
    bar_re, bar_im = mag * jnp.cos(lam_im * dt), mag * jnp.sin(lam_im * dt)
    den = lam_re * lam_re + lam_im * lam_im
    num_re = bar_re - 1.0
    coef_re = (num_re * lam_re + bar_im * lam_im) / den
    coef_im = (bar_im * lam_re - num_re * lam_im) / den
    br, bi = bre_ref[...], bim_ref[...]
    barre_ref[...] = bar_re
    barim_ref[...] = bar_im
    bbre_ref[...] = coef_re * br - coef_im * bi
    bbim_ref[...] = coef_re * bi + coef_im * br


def _ssm_discretize(a_re, a_im, log_dt, b_re, b_im):
    g, p = a_re.shape
    n = g * p
    col = jax.ShapeDtypeStruct((n, 1), F32)
    mat = jax.ShapeDtypeStruct((n, SSM_GROUP), F32)
    return pl.pallas_call(_ssm_disc_body, out_shape=[col, col, mat, mat], name="ssm_discretize")(
        a_re.reshape(n, 1), a_im.reshape(n, 1), jnp.repeat(log_dt, p).reshape(n, 1),
        b_re.reshape(n, SSM_GROUP), b_im.reshape(n, SSM_GROUP))


def _gelu_tanh(x):
    return 0.5 * x * (1.0 + jnp.tanh(math.sqrt(2.0 / math.pi) * (x + 0.044715 * (x * x * x))))


def _ssm_body(*refs, nb, lc, n_state_rows):
    u_refs = refs[:nb]
    h0_ref, lam_ref, bbig_ref, cbig_ref, d_ref, wglu_ref, bglu_ref, y_ref, ht_ref, s_ref, hcar_ref = refs[nb:]
    i = pl.program_id(0)
    half = n_state_rows // 2
    n_tiles = n_state_rows
    col_chunk = 8 * LANES

    @pl.when(i == 0)
    def _():
        hcar_ref[...] = h0_ref[...]

    for b in range(nb):
        ub = u_refs[b][...].astype(BF16)
        for n in range(n_tiles * LANES // col_chunk):
            res = jnp.dot(ub, bbig_ref[:, n * col_chunk:(n + 1) * col_chunk], preferred_element_type=F32)
            for c in range(col_chunk // LANES):
                tile = n * (col_chunk // LANES) + c
                s_ref[b, :, SUBLANES * tile:SUBLANES * (tile + 1), :] = (
                    res[:, LANES * c:LANES * (c + 1)].reshape(lc // SUBLANES, SUBLANES, LANES))

    lam_re, lam_im = lam_ref[:half, :], lam_ref[half:, :]

    def block(r, hs):
        hs = list(hs)
        for t in range(SUBLANES):
            for b in range(nb):
                bu = s_ref[b, r, pl.ds(t, n_state_rows, stride=SUBLANES), :]
                hr, hi = hs[2 * b], hs[2 * b + 1]
                nr = lam_re * hr - lam_im * hi + bu[:half]
                ni = lam_re * hi + lam_im * hr + bu[half:]
                s_ref[b, r, pl.ds(t, n_state_rows, stride=SUBLANES), :] = jnp.concatenate([nr, ni], axis=0)
                hs[2 * b], hs[2 * b + 1] = nr, ni
        return tuple(hs)

    init = []
    for b in range(nb):
        init += [hcar_ref[b, :half, :], hcar_ref[b, half:, :]]
    fin = lax.fori_loop(0, lc // SUBLANES, block, tuple(init))
    for b in range(nb):
        hb = jnp.concatenate([fin[2 * b], fin[2 * b + 1]], axis=0)
        hcar_ref[b] = hb
        ht_ref[b] = hb

    for b in range(nb):
        acc = jnp.zeros((lc, y_ref.shape[-1]), F32)
        for n in range(n_tiles * LANES // col_chunk):
            first = n * (col_chunk // LANES)
            tiles = [s_ref[b, :, SUBLANES * (first + c):SUBLANES * (first + c + 1), :].reshape(lc, LANES)
                     for c in range(col_chunk // LANES)]
            hk = jnp.concatenate(tiles, axis=1).astype(BF16)
            acc = acc + jnp.dot(hk, cbig_ref[n * col_chunk:(n + 1) * col_chunk, :], preferred_element_type=F32)
        y = acc + d_ref[...] * u_refs[b][...]
        z = _gelu_tanh(y)
        gate = _sigmoid(jnp.dot(z.astype(BF16), wglu_ref[...], preferred_element_type=F32) + bglu_ref[...])
        y_ref[b] = z * gate


def _ssm(u, row0, nb, seq, lc, h0, lam, bbig, cbig, dvec, wglu, bglu):
    d_ssm = u.shape[1]
    n_state_rows = lam.shape[0]
    n_chunks = seq // lc
    blk0 = row0 // lc
    u_specs = [pl.BlockSpec((lc, d_ssm), functools.partial(lambda i, b: (blk0 + b * n_chunks + i, 0), b=b))
               for b in range(nb)]
    return pl.pallas_call(
        functools.partial(_ssm_body, nb=nb, lc=lc, n_state_rows=n_state_rows),
        grid=(n_chunks,),
        in_specs=u_specs + [_const_spec(h0.shape), _const_spec(lam.shape), _const_spec(bbig.shape),
                            _const_spec(cbig.shape), _const_spec(dvec.shape), _const_spec(wglu.shape),
                            _const_spec(bglu.shape)],
        out_specs=[pl.BlockSpec((nb, lc, d_ssm), lambda i: (0, i, 0)), _const_spec(h0.shape)],
        out_shape=[jax.ShapeDtypeStruct((nb, seq, d_ssm), F32), jax.ShapeDtypeStruct(h0.shape, F32)],
        scratch_shapes=[pltpu.VMEM((nb, lc // SUBLANES, SUBLANES * n_state_rows, LANES), F32),
                        pltpu.VMEM(h0.shape, F32)],
        compiler_params=_cparams(1),
        name=f"ssm_b{nb}",
    )(*([u] * nb), h0, lam, bbig, cbig, dvec, wglu, bglu)


def _bias_rows(m_ref, lead, base, n_rows, n_cols):
    for r in range(n_rows):
        m_ref[lead, r:r + 1, :] = base[:, QBLK - 1 - r:QBLK - 1 - r + n_cols]


def _attn_prompt_body(q_ref, k_ref, v_ref, base_ref, o_ref, kb_ref, vb_ref, m_ref):
    j = pl.program_id(2)

    @pl.when(j == 0)
    def _():
        kb_ref[:BAND_ROWS, :] = jnp.zeros((BAND_ROWS, LANES), BF16)
        vb_ref[:BAND_ROWS, :] = jnp.zeros((BAND_ROWS, LANES), BF16)
        kb_ref[BAND_ROWS:, :] = k_ref[...].astype(BF16)
        vb_ref[BAND_ROWS:, :] = v_ref[...].astype(BF16)
        rq = lax.broadcasted_iota(I32, (QBLK, KWIN), 0) // CHUNK
        ck = lax.broadcasted_iota(I32, (QBLK, KWIN), 1) // CHUNK
        in_band = (ck >= rq) & (ck <= rq + LEFT_CHUNKS)
        for a in range(2):
            _bias_rows(m_ref, a, base_ref[0, a:a + 1, :], QBLK, KWIN)
            m_ref[a] = jnp.where(in_band, m_ref[a], NEG_INF)

    q2 = q_ref[...]
    start = pl.multiple_of(j * QBLK, QBLK)
    kw = kb_ref[pl.ds(start, KWIN), :]
    vw = vb_ref[pl.ds(start, KWIN), :]
    lane = lax.broadcasted_iota(I32, (QBLK, LANES), 1)
    col = lax.broadcasted_iota(I32, (QBLK, KWIN), 1)
    first_valid = BAND_ROWS - j * QBLK
    outs = []
    for a in range(2):
        qa = jnp.where((lane >= HEAD_DIM * a) & (lane < HEAD_DIM * (a + 1)), q2, jnp.zeros_like(q2))
        s = lax.dot_general(qa, kw, (((1,), (1,)), ((), ())), preferred_element_type=F32)
        s = jnp.where(col >= first_valid, s + m_ref[a], NEG_INF)
        p = jnp.exp(s - jnp.max(s, axis=1, keepdims=True))
        o = jnp.dot(p.astype(BF16), vw, preferred_element_type=F32)
        outs.append(o * (1.0 / jnp.sum(p, axis=1, keepdims=True)))
    o_ref[...] = jnp.where(lane < HEAD_DIM, outs[0], outs[1])


def _attn_prompt(q, k, v, base2, nb, seq):
    d_attn = q.shape[1]
    pairs = d_attn // LANES
    nq = seq // QBLK
    kv_spec = pl.BlockSpec((seq, LANES), lambda b, p, j: (b, p))
    return pl.pallas_call(
        _attn_prompt_body,
        grid=(nb, pairs, nq),
        in_specs=[pl.BlockSpec((QBLK, LANES), lambda b, p, j: (b * nq + j, p)), kv_spec, kv_spec,
                  pl.BlockSpec((1, 2, base2.shape[2]), lambda b, p, j: (p, 0, 0))],
        out_specs=pl.BlockSpec((QBLK, LANES), lambda b, p, j: (b * nq + j, p)),
        out_shape=jax.ShapeDtypeStruct((nb * seq, d_attn), F32),
        scratch_shapes=[pltpu.VMEM((seq + BAND_ROWS, LANES), BF16), pltpu.VMEM((seq + BAND_ROWS, LANES), BF16),
                        pltpu.VMEM((2, QBLK, KWIN), F32)],
        compiler_params=_cparams(3),
        name="attn_prompt",
    )(q, k, v, base2)


def _attn_sample_body(q_ref, k_ref, v_ref, ck_ref, cv_ref, base_ref, o_ref, m_ref, *, heads, n_cache):
    @pl.when(pl.program_id(0) == 0)
    def _():
        for h in range(heads):
            _bias_rows(m_ref, h, base_ref[h:h + 1, :], CHUNK, n_cache + CHUNK)

    q = q_ref[...]
    k = k_ref[...].astype(BF16)
    v = v_ref[...].astype(BF16)
    nt = (((1,), (1,)), ((), ()))
    outs = []
    for h in range(heads):
        sl = slice(HEAD_DIM * h, HEAD_DIM * (h + 1))
        qh = q[:, sl]
        s1 = lax.dot_general(qh, ck_ref[0, h].astype(BF16), nt, preferred_element_type=F32) + m_ref[h, :, :n_cache]
        s2 = lax.dot_general(qh, k[:, sl], nt, preferred_element_type=F32) + m_ref[h, :, n_cache:]
        mx = jnp.maximum(jnp.max(s1, axis=1, keepdims=True), jnp.max(s2, axis=1, keepdims=True))
        p1, p2 = jnp.exp(s1 - mx), jnp.exp(s2 - mx)
        den = jnp.sum(p1, axis=1, keepdims=True) + jnp.sum(p2, axis=1, keepdims=True)
        o = (jnp.dot(p1.astype(BF16), cv_ref[0, h].astype(BF16), preferred_element_type=F32)
             + jnp.dot(p2.astype(BF16), v[:, sl], preferred_element_type=F32))
        outs.append(o * (1.0 / den))
    o_ref[...] = jnp.concatenate(outs, axis=1)


def _attn_sample(q, k, v, cache_k, cache_v, base, row0):
    nb, heads, n_cache, _ = cache_k.shape
    d_attn = q.shape[1]
    blk0 = row0 // CHUNK
    row = pl.BlockSpec((CHUNK, d_attn), lambda b: (blk0 + b, 0))
    cache = pl.BlockSpec((1, heads, n_cache, HEAD_DIM), lambda b: (b, 0, 0, 0))
    return pl.pallas_call(
        functools.partial(_attn_sample_body, heads=heads, n_cache=n_cache),
        grid=(nb,),
        in_specs=[row, row, row, cache, cache, _const_spec(base.shape)],
        out_specs=pl.BlockSpec((CHUNK, d_attn), lambda b: (b, 0)),
        out_shape=jax.ShapeDtypeStruct((nb * CHUNK, d_attn), F32),
        scratch_shapes=[pltpu.VMEM((heads, CHUNK, n_cache + CHUNK), F32)],
        compiler_params=_cparams(1),
        name="attn_sample",
    )(q, k, v, cache_k, cache_v, base)


def _bias_base(table):
    heads = table.shape[0]
    far = jnp.broadcast_to(table[:, 2 * MAX_REL:], (heads, KWIN - 1 - MAX_REL + 1))
    near = table[:, 2 * MAX_REL - 1:0:-1]
    base = jnp.concatenate([far, near], axis=1)
    return jnp.pad(base, ((0, 0), (0, KWIN + QBLK - base.shape[1])))


def _post_body(*refs, n_x, n_y, n_pb, tm):
    i = pl.program_id(0)
    pos = 0

    def take(n):
        nonlocal pos
        out = refs[pos:pos + n]
        pos += n
        return out

    def merged(pair):
        if len(pair) == 1:
            return pair[0][...]
        return jnp.where(i < n_pb, pair[0][...], pair[1][...])

    x_refs = take(n_x)
    y_pairs = [take(2) for _ in range(n_y)]
    w_refs = take(n_y)
    g_ref, wrh_ref, wrl_ref, br_ref, tril_ref, cin_ref = take(6)
    xo_ref, xp_ref, mi_ref, mf_ref, cnt_ref = take(5)
    (carry_ref,) = take(1)

    x = merged(x_refs)
    for pair, w_ref in zip(y_pairs, w_refs):
        x = x + jnp.dot(merged(pair).astype(BF16), w_ref[...], preferred_element_type=F32)
    xo_ref[...] = x

    xn = _rms(x, g_ref[...])
    half = xn.shape[1] // 2
    lo_bits = lax.bitcast_convert_type(xn[:, :half].astype(BF16).astype(F32), U32)
    hi_bits = lax.bitcast_convert_type(xn[:, half:].astype(BF16).astype(F32), U32)
    xp_ref[...] = (lo_bits >> 16) | (hi_bits & jnp.uint32(0xFFFF0000))

    xh, xl = _split_bf16(xn)
    logits = (jnp.dot(xh, wrh_ref[...], preferred_element_type=F32)
              + jnp.dot(xl, wrh_ref[...], preferred_element_type=F32)
              + jnp.dot(xh, wrl_ref[...], preferred_element_type=F32)) + br_ref[...]
    lane = lax.broadcasted_iota(I32, logits.shape, 1)

    def first_argmax(vals):
        top = jnp.max(vals, axis=1, keepdims=True)
        return top, jnp.min(jnp.where(vals == top, lane, LANES), axis=1, keepdims=True)

    is_group = lane < MOE_GROUPS
    g_top, g_idx = first_argmax(jnp.where(is_group, logits, -jnp.inf))
    g_w = 1.0 / jnp.sum(jnp.where(is_group, jnp.exp(logits - g_top), 0.0), axis=1, keepdims=True)
    first = MOE_GROUPS + EXPERTS_PER_GROUP * g_idx
    e_log = jnp.where((lane >= first) & (lane < first + EXPERTS_PER_GROUP), logits, -jnp.inf)
    v1, l1 = first_argmax(e_log)
    v2, l2 = first_argmax(jnp.where(lane == l1, -jnp.inf, e_log))
    t = jnp.exp(v2 - v1)
    w1 = 1.0 / (1.0 + t)
    w2 = t * w1
    e1, e2 = l1 - MOE_GROUPS, l2 - MOE_GROUPS

    @pl.when(i == 0)
    def _():
        carry_ref[...] = cin_ref[...]

    hit1, hit2 = lane == e1, lane == e2
    oh = jnp.where(hit1 | hit2, 1.0, 0.0)
    before = jnp.dot(tril_ref[...], oh.astype(BF16), preferred_element_type=F32) + carry_ref[...]
    r1 = jnp.sum(jnp.where(hit1, before, 0.0), axis=1, keepdims=True).astype(I32)
    r2 = jnp.sum(jnp.where(hit2, before, 0.0), axis=1, keepdims=True).astype(I32)
    carry_ref[...] = carry_ref[...] + jnp.sum(oh, axis=0, keepdims=True)
    cnt_ref[...] = carry_ref[...]

    mi_ref[...] = jnp.where(lane == 0, e1, jnp.where(lane == 1, e2, jnp.where(lane == 2, r1, r2)))
    mf_ref[...] = jnp.where(lane == 0, g_w * w1, g_w * w2)


def _post_mixer(x_src, y_pairs, ws, g, router, tril, t_p, t_s):
    wrh, wrl, br = router
    d = ws[0].shape[1]
    n_pb = t_p // TM
    n_tot = (t_p + t_s) // TM
    row = lambda width: pl.BlockSpec((TM, width), lambda i: (i, 0))
    in_specs, args = [], []
    if len(x_src) == 1:
        in_specs.append(row(d))
    else:
        in_specs += _pair_specs(TM, d, n_pb)
    args += list(x_src)
    for yp, ys in y_pairs:
        in_specs += _pair_specs(TM, yp.shape[1], n_pb)
        args += [yp, ys]
    consts = list(ws) + [g.reshape(1, d), wrh, wrl, br, tril, jnp.zeros((1, LANES), F32)]
    in_specs += [_const_spec(c.shape) for c in consts]
    t = t_p + t_s
    return pl.pallas_call(
        functools.partial(_post_body, n_x=len(x_src), n_y=len(y_pairs), n_pb=n_pb, tm=TM),
        grid=(n_tot,),
        in_specs=in_specs,
        out_specs=[row(d), row(d // 2), row(LANES), row(LANES), _const_spec((1, LANES))],
        out_shape=[jax.ShapeDtypeStruct((t, d), F32), jax.ShapeDtypeStruct((t, d // 2), U32),
                   jax.ShapeDtypeStruct((t, LANES), I32), jax.ShapeDtypeStruct((t, LANES), F32),
                   jax.ShapeDtypeStruct((1, LANES), F32)],
        scratch_shapes=[pltpu.VMEM((1, LANES), F32)],
        compiler_params=_cparams(1),
        name="post_mixer",
    )(*args, *consts)


def _router_weights(gate_w, gate_b, router_w, router_b):
    d = gate_w.shape[0]
    w = jnp.concatenate([gate_w, router_w.transpose(1, 0, 2).reshape(d, N_EXPERTS)], axis=1)
    b = jnp.concatenate([gate_b, router_b.reshape(N_EXPERTS)])
    w = jnp.pad(w, ((0, 0), (0, LANES - w.shape[1])))
    b = jnp.pad(b, (0, LANES - b.shape[0])).reshape(1, LANES)
    hi, lo = _split_bf16(w)
    return hi, lo, b


def _dispatch_body(slots_ref, pad_pos_ref, pad_n_ref, x_ref, xs_ref, zero_ref, sem, *, tm):
    i = pl.program_id(0)

    def pad_copy(row):
        return pltpu.make_async_copy(zero_ref.at[pl.ds(0, 1)], xs_ref.at[pl.ds(row, 1)], sem.at[1])

    def tile_copy(tile):
        return pltpu.make_async_copy(zero_ref, xs_ref.at[pl.ds(pl.multiple_of(tile * tm, tm), tm)], sem.at[1])

    @pl.when(i == 0)
    def _():
        zero_ref[...] = jnp.zeros(zero_ref.shape, zero_ref.dtype)
        for e in range(N_EXPERTS):
            pos, n = pad_pos_ref[e], pad_n_ref[e]
            lax.fori_loop(0, n, lambda r, c, pos=pos: (pad_copy(pos + r).start(), c)[1], 0)
        used_tiles = (pad_pos_ref[N_EXPERTS - 1] + pad_n_ref[N_EXPERTS - 1]) // tm
        all_tiles = xs_ref.shape[0] // tm
        lax.fori_loop(used_tiles, all_tiles, lambda tl, c: (tile_copy(tl).start(), c)[1], 0)
        for e in range(N_EXPERTS):
            pos, n = pad_pos_ref[e], pad_n_ref[e]
            lax.fori_loop(0, n, lambda r, c, pos=pos: (pad_copy(pos + r).wait(), c)[1], 0)
        lax.fori_loop(used_tiles, all_tiles, lambda tl, c: (tile_copy(tl).wait(), c)[1], 0)

    def row_copy(r, k):
        slot = slots_ref[2 * (i * tm + r) + k]
        return pltpu.make_async_copy(x_ref.at[pl.ds(r, 1)], xs_ref.at[pl.ds(slot, 1)], sem.at[0])

    def start(r, c):
        row_copy(r, 0).start()
        row_copy(r, 1).start()
        return c

    def wait(r, c):
        row_copy(r, 0).wait()
        row_copy(r, 1).wait()
        return c

    lax.fori_loop(0, tm, start, 0)
    lax.fori_loop(0, tm, wait, 0)


def _dispatch(xpacked, slots, pad_pos, pad_n, n_slots):
    t, w = xpacked.shape
    grid_spec = pltpu.PrefetchScalarGridSpec(
        num_scalar_prefetch=3,
        grid=(t // TM_MOE,),
        in_specs=[pl.BlockSpec((TM_MOE, w), lambda i, *_: (i, 0))],
        out_specs=pl.BlockSpec(memory_space=pl.ANY),
        scratch_shapes=[pltpu.VMEM((TM_MOE, w), U32), pltpu.SemaphoreType.DMA((2,))],
    )
    return pl.pallas_call(
        functools.partial(_dispatch_body, tm=TM_MOE),
        grid_spec=grid_spec,
        out_shape=jax.ShapeDtypeStruct((n_slots, w), U32),
        compiler_params=_cparams(1),
        name="moe_dispatch",
    )(slots, pad_pos, pad_n, xpacked)


def _moe_body(te_ref, nt_ref, x_ref, wgu_ref, wd_ref, o_ref, wgu_bf, wd_bf):
    i = pl.program_id(0)
    prev = te_ref[jnp.maximum(i - 1, 0)]

    @pl.when((i == 0) | (te_ref[i] != prev))
    def _():
        wgu_bf[...] = wgu_ref[0].astype(BF16)
        wd_bf[...] = wd_ref[0].astype(BF16)

    @pl.when(i < nt_ref[0])
    def _():
        w = x_ref[...]
        half = w.shape[1]
        x_lo = lax.bitcast_convert_type(w << 16, F32).astype(BF16)
        x_hi = lax.bitcast_convert_type(w & jnp.uint32(0xFFFF0000), F32).astype(BF16)
        h = (jnp.dot(x_lo, wgu_bf[:half, :], preferred_element_type=F32)
             + jnp.dot(x_hi, wgu_bf[half:, :], preferred_element_type=F32))
        d_exp = h.shape[1] // 2
        gt, up = h[:, :d_exp], h[:, d_exp:]
        act = (gt * _sigmoid(gt) * up).astype(BF16)
        o_ref[...] = jnp.dot(act, wd_bf[...], preferred_element_type=F32)

    @pl.when(i >= nt_ref[0])
    def _():
        o_ref[...] = jnp.zeros(o_ref.shape, o_ref.dtype)


def _moe_experts(xsorted, tile_expert, n_tiles, wgu, wd):
    n_slots, half = xsorted.shape
    d = 2 * half
    n_max = n_slots // TM_MOE
    tile = lambda width: pl.BlockSpec((TM_MOE, width), lambda i, te, nt: (jnp.minimum(i, nt[0] - 1), 0))
    grid_spec = pltpu.PrefetchScalarGridSpec(
        num_scalar_prefetch=2,
        grid=(n_max,),
        in_specs=[tile(half),
                  pl.BlockSpec((1,) + wgu.shape[1:], lambda i, te, nt: (te[i], 0, 0)),
                  pl.BlockSpec((1,) + wd.shape[1:], lambda i, te, nt: (te[i], 0, 0))],
        out_specs=pl.BlockSpec((TM_MOE, d), lambda i, te, nt: (i, 0)),
        scratch_shapes=[pltpu.VMEM(wgu.shape[1:], BF16), pltpu.VMEM(wd.shape[1:], BF16)],
    )
    return pl.pallas_call(
        _moe_body,
        grid_spec=grid_spec,
        out_shape=jax.ShapeDtypeStruct((n_slots, d), F32),
        compiler_params=_cparams(1),
        name="moe_experts",
    )(tile_expert, n_tiles, xsorted, wgu, wd)


def _combine_body(slots_ref, x_ref, mf_ref, os_ref, *rest, tm, n_steps, n_pb):
    out_refs, (buf_ref, sem) = rest[:-2], rest[-2:]
    i = pl.program_id(0)

    def row_copy(tile, par, r, k):
        slot = slots_ref[2 * (tile * tm + r) + k]
        return pltpu.make_async_copy(os_ref.at[pl.ds(slot, 1)], buf_ref.at[par, k, pl.ds(r, 1)], sem.at[par])

    def issue(tile, par):
        def body(r, c):
            row_copy(tile, par, r, 0).start()
            row_copy(tile, par, r, 1).start()
            return c
        lax.fori_loop(0, tm, body, 0)

    @pl.when(i == 0)
    def _():
        issue(0, 0)

    @pl.when(i + 1 < n_steps)
    def _():
        issue(i + 1, (i + 1) % 2)

    par = i % 2

    def wait(r, c):
        row_copy(i, par, r, 0).wait()
        row_copy(i, par, r, 1).wait()
        return c
    lax.fori_loop(0, tm, wait, 0)

    mf = mf_ref[...]
    y = x_ref[...] + mf[:, 0:1] * buf_ref[par, 0] + mf[:, 1:2] * buf_ref[par, 1]
    if len(out_refs) == 1:
        out_refs[0][...] = y
    else:
        @pl.when(i < n_pb)
        def _():
            out_refs[0][...] = y

        @pl.when(i >= n_pb)
        def _():
            out_refs[1][...] = y


def _combine(x, meta_f, slots, osorted, t_p, split):
    t, d = x.shape
    n_steps = t // TM_MOE
    n_pb = t_p // TM_MOE
    row = lambda width: pl.BlockSpec((TM_MOE, width), lambda i, *_: (i, 0))
    if split:
        out_specs = [pl.BlockSpec((TM_MOE, d), lambda i, *_: (jnp.minimum(i, n_pb - 1), 0)),
                     pl.BlockSpec((TM_MOE, d), lambda i, *_: (jnp.maximum(i - n_pb, 0), 0))]
        out_shape = [jax.ShapeDtypeStruct((t_p, d), F32), jax.ShapeDtypeStruct((t - t_p, d), F32)]
    else:
        out_specs = [row(d)]
        out_shape = [jax.ShapeDtypeStruct((t, d), F32)]
    grid_spec = pltpu.PrefetchScalarGridSpec(
        num_scalar_prefetch=1,
        grid=(n_steps,),
        in_specs=[row(d), row(LANES), pl.BlockSpec(memory_space=pl.ANY)],
        out_specs=out_specs,
        scratch_shapes=[pltpu.VMEM((2, 2, TM_MOE, d), F32), pltpu.SemaphoreType.DMA((2,))],
    )
    return pl.pallas_call(
        functools.partial(_combine_body, tm=TM_MOE, n_steps=n_steps, n_pb=n_pb),
        grid_spec=grid_spec,
        out_shape=out_shape,
        compiler_params=_cparams(1),
        name="moe_combine",
    )(slots, x, meta_f, osorted)


def _moe_layer(x, xpacked, meta_i, meta_f, counts, wgu, wd, t_p, split):
    t = x.shape[0]
    n_max = (2 * t + N_EXPERTS * (TM_MOE - 1)) // TM_MOE
    cnt = counts[0, :N_EXPERTS].astype(I32)
    padded = (cnt + TM_MOE - 1) // TM_MOE * TM_MOE
    ends = jnp.cumsum(padded)
    offs = ends - padded
    slots = (jnp.take(offs, meta_i[:, 0:2]) + meta_i[:, 2:4]).reshape(-1)
    n_tiles = (ends[-1] // TM_MOE).reshape(1)
    tile_expert = jnp.minimum(
        jnp.searchsorted(ends, jnp.arange(n_max, dtype=I32) * TM_MOE, side="right"), N_EXPERTS - 1).astype(I32)
    xsorted = _dispatch(xpacked, slots, offs + cnt, padded - cnt, n_max * TM_MOE)
    osorted = _moe_experts(xsorted, tile_expert, n_tiles, wgu, wd)
    return _combine(x, meta_f, slots, osorted, t_p, split)


def _conv_in_body(x_ref, g_ref, w_ref, u_ref):
    xn = _rms(x_ref[...], g_ref[...]).astype(BF16)
    h = jnp.dot(xn, w_ref[...], preferred_element_type=F32)
    c = h.shape[1] // 2
    u_ref[...] = h[:, :c] * _sigmoid(h[:, c:])


def _conv_in(x, g, w):
    t, d = x.shape
    c = w.shape[1] // 2
    return pl.pallas_call(
        _conv_in_body,
        grid=(t // TM,),
        in_specs=[pl.BlockSpec((TM, d), lambda i: (i, 0)), _const_spec((1, d)), _const_spec(w.shape)],
        out_specs=pl.BlockSpec((TM, c), lambda i: (i, 0)),
        out_shape=jax.ShapeDtypeStruct((t, c), F32),
        compiler_params=_cparams(1),
        name="conv_in",
    )(x, g.reshape(1, d), w.astype(BF16))


def _dwconv_body(cur_ref, prev_ref, left_ref, w_ref, b_ref, g_ref, beta_ref, o_ref, win_ref, *, tm):
    i = pl.program_id(1)
    win_ref[:CONV_HALO, :] = jnp.where(i == 0, left_ref[0], prev_ref[...])
    win_ref[CONV_HALO:, :] = cur_ref[...]
    first_tap = CONV_HALO - (CONV_WIDTH - 1)
    for c in range(tm // CONV_ROWS):
        r0 = c * CONV_ROWS
        acc = jnp.zeros((CONV_ROWS, cur_ref.shape[1]), F32)
        for j in range(CONV_WIDTH):
            acc = acc + w_ref[j:j + 1, :] * win_ref[r0 + first_tap + j:r0 + first_tap + j + CONV_ROWS, :]
        y = acc + b_ref[...]
        mu = jnp.mean(y, axis=1, keepdims=True)
        yc = y - mu
        yn = yc * lax.rsqrt(jnp.mean(yc * yc, axis=1, keepdims=True) + EPS) * g_ref[...] + beta_ref[...]
        o_ref[r0:r0 + CONV_ROWS, :] = (yn * _sigmoid(yn)).astype(BF16)


def _dwconv(u, left, dw_w, dw_b, ln_g, ln_b, row0, nb, seq, tm):
    c = u.shape[1]
    n_t = seq // tm
    blk0 = row0 // tm
    halo_per_tile = tm // CONV_HALO
    vec = lambda a: a.reshape(1, c)
    w = jnp.pad(dw_w, ((0, CONV_HALO - CONV_WIDTH), (0, 0)))
    return pl.pallas_call(
        functools.partial(_dwconv_body, tm=tm),
        grid=(nb, n_t),
        in_specs=[pl.BlockSpec((tm, c), lambda b, i: (blk0 + b * n_t + i, 0)),
                  pl.BlockSpec((CONV_HALO, c),
                               lambda b, i: (jnp.maximum((blk0 + b * n_t + i) * halo_per_tile - 1, 0), 0)),
                  pl.BlockSpec((1, CONV_HALO, c), lambda b, i: (b, 0, 0)),
                  _const_spec(w.shape), _const_spec((1, c)), _const_spec((1, c)), _const_spec((1, c))],
        out_specs=pl.BlockSpec((tm, c), lambda b, i: (b * n_t + i, 0)),
        out_shape=jax.ShapeDtypeStruct((nb * seq, c), BF16),
        scratch_shapes=[pltpu.VMEM((tm + CONV_HALO, c), F32)],
        compiler_params=_cparams(2),
        name=f"dwconv_b{nb}",
    )(u, u, left, w, vec(dw_b), vec(ln_g), vec(ln_b))


def _block_diag(blocks):
    g, a, b = blocks.shape
    return jnp.einsum("gab,gh->gahb", blocks, jnp.eye(g, dtype=blocks.dtype)).reshape(g * a, g * b)


def _state_rows(re, im):
    nb = re.shape[0]
    return jnp.concatenate([re.reshape(nb, -1, LANES), im.reshape(nb, -1, LANES)], axis=1)


def kernel(x_prompt, x_sample, cache_attn_k, cache_attn_v, state_ssm_re, state_ssm_im, cache_conv, norm_mixer, norm_ffn, w_in_even, w_out_even, ssm_a_re, ssm_a_im, ssm_log_dt, ssm_b_re, ssm_b_im, ssm_c_re, ssm_c_im, ssm_d, ssm_glu_w, ssm_glu_b, attn_q_gain, attn_k_gain, attn_rel_bias, conv_w_in, conv_dw_w, conv_dw_b, conv_ln_g, conv_ln_b, conv_w_out, moe_gate_w, moe_gate_b, moe_router_w, moe_router_b, moe_w_gate_up, moe_w_down):
    nb_p, seq_p, d = x_prompt.shape
    nb_s, seq_s, _ = x_sample.shape
    t_p, t_s = nb_p * seq_p, nb_s * seq_s
    xp, xs = x_prompt.reshape(t_p, d), x_sample.reshape(t_s, d)
    depth = norm_mixer.shape[0]
    tril = jnp.tril(jnp.ones((TM, TM), F32), -1).astype(BF16)

    ks_p, vs_p, hs_p, ks_s, vs_s, hs_s, conv_p, conv_s = [], [], [], [], [], [], [], []
    x_src = (xp, xs)
    for layer in range(depth):
        j = layer // 2
        if layer % 2 == 0:
            u, q, k, v = _inproj(*x_src, norm_mixer[layer], w_in_even[j], attn_q_gain[j], attn_k_gain[j])
            d_ssm = u.shape[1]
            groups, n_state = ssm_a_re[j].shape
            bar_re, bar_im, bb_re, bb_im = _ssm_discretize(ssm_a_re[j], ssm_a_im[j], ssm_log_dt[j],
                                                           ssm_b_re[j], ssm_b_im[j])
            lam = jnp.concatenate([bar_re.reshape(-1, LANES), bar_im.reshape(-1, LANES)], axis=0)
            to_state = lambda m: _block_diag(m.reshape(groups, n_state, SSM_GROUP).transpose(0, 2, 1))
            bbig = jnp.concatenate([to_state(bb_re), to_state(bb_im)], axis=1).astype(BF16)
            cbig = jnp.concatenate([_block_diag(ssm_c_re[j].transpose(0, 2, 1)),
                                    _block_diag(-ssm_c_im[j].transpose(0, 2, 1))], axis=0).astype(BF16)
            ssm_args = (lam, bbig, cbig, ssm_d[j].reshape(1, d_ssm), _block_diag(ssm_glu_w[j]).astype(BF16),
                        ssm_glu_b[j].reshape(1, d_ssm))
            zero_state = jnp.zeros((nb_p, groups, n_state), F32)
            y_ssm_p, h_p = _ssm(u, 0, nb_p, seq_p, 256, _state_rows(zero_state, zero_state), *ssm_args)
            y_ssm_s, h_s = _ssm(u, t_p, nb_s, seq_s, seq_s, _state_rows(state_ssm_re[j], state_ssm_im[j]), *ssm_args)
            base = _bias_base(attn_rel_bias[j])
            heads = base.shape[0]
            y_att_p = _attn_prompt(q, k, v, base.reshape(heads // 2, 2, -1), nb_p, seq_p)
            y_att_s = _attn_sample(q, k, v, cache_attn_k[j], cache_attn_v[j], base, t_p)
            w_out = w_out_even[j].astype(BF16)
            y_pairs = [(y_ssm_p.reshape(t_p, d_ssm), y_ssm_s.reshape(t_s, d_ssm)), (y_att_p, y_att_s)]
            ws = [w_out[:d_ssm], w_out[d_ssm:]]

            keep = min(BAND_ROWS, seq_p)
            heads_of = lambda a, nb, seq: a.reshape(nb, seq, heads, HEAD_DIM).transpose(0, 2, 1, 3)
            ks_p.append(heads_of(k[:t_p], nb_p, seq_p)[:, :, seq_p - keep:])
            vs_p.append(heads_of(v[:t_p], nb_p, seq_p)[:, :, seq_p - keep:])
            ks_s.append(heads_of(k[t_p:], nb_s, seq_s))
            vs_s.append(heads_of(v[t_p:], nb_s, seq_s))
            half = h_p.shape[1] // 2
            hs_p.append((h_p[:, :half].reshape(nb_p, groups, n_state), h_p[:, half:].reshape(nb_p, groups, n_state)))
            hs_s.append((h_s[:, :half].reshape(nb_s, groups, n_state), h_s[:, half:].reshape(nb_s, groups, n_state)))
        else:
            (x_merged,) = x_src
            uc = _conv_in(x_merged, norm_mixer[layer], conv_w_in[j])
            c = uc.shape[1]
            conv_args = (conv_dw_w[j], conv_dw_b[j], conv_ln_g[j], conv_ln_b[j])
            left_p = jnp.zeros((nb_p, CONV_HALO, c), F32)
            left_s = jnp.pad(cache_conv[j], ((0, 0), (CONV_HALO - (CONV_WIDTH - 1), 0), (0, 0)))
            act_p = _dwconv(uc, left_p, *conv_args, 0, nb_p, seq_p, 128)
            act_s = _dwconv(uc, left_s, *conv_args, t_p, nb_s, seq_s, seq_s)
            y_pairs = [(act_p, act_s)]
            ws = [conv_w_out[j].astype(BF16)]
            keep = CONV_WIDTH - 1
            conv_p.append(uc[:t_p].reshape(nb_p, seq_p, c)[:, seq_p - keep:])
            ext_s = jnp.concatenate([cache_conv[j], uc[t_p:].reshape(nb_s, seq_s, c)], axis=1)
            conv_s.append(ext_s[:, ext_s.shape[1] - keep:])

        router = _router_weights(moe_gate_w[layer], moe_gate_b[layer], moe_router_w[layer], moe_router_b[layer])
        x_new, xpacked, meta_i, meta_f, counts = _post_mixer(x_src, y_pairs, ws, norm_ffn[layer], router, tril, t_p, t_s)
        last = layer == depth - 1
        out = _moe_layer(x_new, xpacked, meta_i, meta_f, counts, moe_w_gate_up[layer], moe_w_down[layer], t_p, last)
        x_src = tuple(out)

    y_prompt, y_sample = x_src
    stack_c = lambda parts, idx: jnp.stack([p[idx] for p in parts])
    return (y_prompt.reshape(nb_p, seq_p, d), y_sample.reshape(nb_s, seq_s, d),
            jnp.stack(ks_p), jnp.stack(vs_p), stack_c(hs_p, 0), stack_c(hs_p, 1), jnp.stack(conv_p),
            jnp.stack(ks_s), jnp.stack(vs_s), stack_c(hs_s, 0), stack_c(hs_s, 1), jnp.stack(conv_s))
```

```python
import functools
import math

import jax
import jax.numpy as jnp
from jax import lax
from jax.experimental import pallas as pl
from jax.experimental.pallas import tpu as pltpu

F32, BF16, I32, U32 = jnp.float32, jnp.bfloat16, jnp.int32, jnp.uint32

CHUNK = 64
SSM_GROUP = 16
SSM_STATE = 64
HEAD_DIM = 64
LEFT_CHUNKS = 8
BAND_ROWS = LEFT_CHUNKS * CHUNK
MAX_REL = 128
CONV_WIDTH = 31
MOE_GROUPS = 4
EXPERTS_PER_GROUP = 8
N_EXPERTS = MOE_GROUPS * EXPERTS_PER_GROUP
EPS = 1e-6
NEG_INF = -1e30

LANES = 128
SUBLANES = 8
VMEM_LIMIT_BYTES = 56 * 1024 * 1024

TM = 512
TM_MOE = 256
QBLK = 2 * CHUNK
KWIN = QBLK + BAND_ROWS
QSTEP = 4
D_MODEL = 1024
ROWS_O = D_MODEL // LANES
CONV_HALO = 32
TM_CONV = 256


def _cparams(n_grid):
    return pltpu.CompilerParams(dimension_semantics=("arbitrary",) * n_grid, vmem_limit_bytes=VMEM_LIMIT_BYTES)


def _sigmoid(x):
    return 1.0 / (1.0 + jnp.exp(-x))


def _rms(x, g):
    return x * lax.rsqrt(jnp.mean(x * x, axis=-1, keepdims=True) + EPS) * g


def _split_bf16(x):
    hi = lax.bitcast_convert_type(lax.bitcast_convert_type(x, U32) & jnp.uint32(0xFFFF0000), F32)
    return hi.astype(BF16), (x - hi).astype(BF16)


def _pair_specs(tm, width, n_prompt_blocks):
    return [pl.BlockSpec((tm, width), lambda i: (jnp.minimum(i, n_prompt_blocks - 1), 0)),
            pl.BlockSpec((tm, width), lambda i: (jnp.maximum(i - n_prompt_blocks, 0), 0))]


def _const_spec(shape):
    nd = len(shape)
    return pl.BlockSpec(shape, lambda *a: (0,) * nd)


def _inproj_body(xp_ref, xs_ref, g_ref, w_ref, qg_ref, kg_ref, seg_ref, u_ref, q_ref, k_ref, v_ref, *, n_pb, d_half):
    i = pl.program_id(0)
    x = jnp.where(i < n_pb, xp_ref[...], xs_ref[...])
    xn = _rms(x, g_ref[...]).astype(BF16)
    proj = jnp.dot(xn, w_ref[...], preferred_element_type=F32)
    u_ref[...] = proj[:, :d_half]
    v_ref[...] = proj[:, 3 * d_half:]

    def head_norm(t, gain):
        hi, lo = _split_bf16(t * t)
        msq = (jnp.dot(hi, seg_ref[...], preferred_element_type=F32)
               + jnp.dot(lo, seg_ref[...], preferred_element_type=F32))
        return t * lax.rsqrt(msq + EPS) * gain

    q_ref[...] = (head_norm(proj[:, d_half:2 * d_half], qg_ref[...]) * (HEAD_DIM ** -0.5)).astype(BF16)
    k_ref[...] = head_norm(proj[:, 2 * d_half:3 * d_half], kg_ref[...])


def _inproj(xp, xs, g, w, qg, kg):
    t_p, d = xp.shape
    t_s = xs.shape[0]
    d_half = w.shape[1] // 4
    n_pb = t_p // TM
    n_tot = (t_p + t_s) // TM
    heads = d_half // HEAD_DIM
    seg = jnp.kron(jnp.eye(heads, dtype=F32), jnp.full((HEAD_DIM, HEAD_DIM), 1.0 / HEAD_DIM, F32)).astype(BF16)
    out_f = jax.ShapeDtypeStruct((t_p + t_s, d_half), F32)
    out_b = jax.ShapeDtypeStruct((t_p + t_s, d_half), BF16)
    row = pl.BlockSpec((TM, d_half), lambda i: (i, 0))
    return pl.pallas_call(
        functools.partial(_inproj_body, n_pb=n_pb, d_half=d_half),
        grid=(n_tot,),
        in_specs=_pair_specs(TM, d, n_pb) + [_const_spec((1, d)), _const_spec(w.shape), _const_spec((1, d_half)),
                                             _const_spec((1, d_half)), _const_spec(seg.shape)],
        out_specs=[row, row, row, row],
        out_shape=[out_f, out_b, out_f, out_f],
        compiler_params=_cparams(1),
        name="inproj_even",
    )(xp, xs, g.reshape(1, d), w.astype(BF16), jnp.tile(qg, heads).reshape(1, d_half),
      jnp.tile(kg, heads).reshape(1, d_half), seg)


def _ssm_disc_body(are_ref, aim_ref, ldt_ref, bre_ref, bim_ref, barre_ref, barim_ref, bbre_ref, bbim_ref):
    lam_re, lam_im = are_ref[...], aim_ref[...]
    dt = jnp.exp(ldt_ref[...])
    mag = jnp.exp(lam_re * dt)
    bar_re, bar_im = mag * jnp.cos(lam_im * dt), mag * jnp.sin(lam_im * dt)
    den = lam_re * lam_re + lam_im * lam_im
    num_re = bar_re - 1.0
    coef_re = (num_re * lam_re + bar_im * lam_im) / den
    coef_im = (bar_im * lam_re - num_re * lam_im) / den
    br, bi = bre_ref[...], bim_ref[...]
    barre_ref[...] = bar_re
    barim_ref[...] = bar_im
    bbre_ref[...] = coef_re * br - coef_im * bi
    bbim_ref[...] = coef_re * bi + coef_im * br


def _ssm_discretize(a_re, a_im, log_dt, b_re, b_im):
    g, p = a_re.shape
    n = g * p
    col = jax.ShapeDtypeStruct((n, 1), F32)
    mat = jax.ShapeDtypeStruct((n, SSM_GROUP), F32)
    return pl.pallas_call(_ssm_disc_body, out_shape=[col, col, mat, mat], name="ssm_discretize")(
        a_re.reshape(n, 1), a_im.reshape(n, 1), jnp.repeat(log_dt, p).reshape(n, 1),
        b_re.reshape(n, SSM_GROUP), b_im.reshape(n, SSM_GROUP))


def _gelu_tanh(x):
    return 0.5 * x * (1.0 + jnp.tanh(math.sqrt(2.0 / math.pi) * (x + 0.044715 * (x * x * x))))


def _ssm_body(*refs, nb, lc, n_state_rows):
    u_refs = refs[:nb]
    h0_ref, lam_ref, bbig_ref, cbig_ref, d_ref, wglu_ref, bglu_ref, y_ref, ht_ref, s_ref, hcar_ref = refs[nb:]
    i = pl.program_id(0)
    half = n_state_rows // 2
    n_tiles = n_state_rows
    col_chunk = 8 * LANES

    @pl.when(i == 0)
    def _():
        hcar_ref[...] = h0_ref[...]

    for b in range(nb):
        ub = u_refs[b][...].astype(BF16)
        for n in range(n_tiles * LANES // col_chunk):
            res = jnp.dot(ub, bbig_ref[:, n * col_chunk:(n + 1) * col_chunk], preferred_element_type=F32)
            for c in range(col_chunk // LANES):
                tile = n * (col_chunk // LANES) + c
                s_ref[b, :, SUBLANES * tile:SUBLANES * (tile + 1), :] = (
                    res[:, LANES * c:LANES * (c + 1)].reshape(lc // SUBLANES, SUBLANES, LANES))

    lam_re, lam_im = lam_ref[:half, :], lam_ref[half:, :]

    def block(r, hs):
        hs = list(hs)
        for t in range(SUBLANES):
            for b in range(nb):
                bu = s_ref[b, r, pl.ds(t, n_state_rows, stride=SUBLANES), :]
                hr, hi = hs[2 * b], hs[2 * b + 1]
                nr = lam_re * hr - lam_im * hi + bu[:half]
                ni = lam_re * hi + lam_im * hr + bu[half:]
                s_ref[b, r, pl.ds(t, n_state_rows, stride=SUBLANES), :] = jnp.concatenate([nr, ni], axis=0)
                hs[2 * b], hs[2 * b + 1] = nr, ni
        return tuple(hs)

    init = []
    for b in range(nb):
        init += [hcar_ref[b, :half, :], hcar_ref[b, half:, :]]
    fin = lax.fori_loop(0, lc // SUBLANES, block, tuple(init))
    for b in range(nb):
        hb = jnp.concatenate([fin[2 * b], fin[2 * b + 1]], axis=0)
        hcar_ref[b] = hb
        ht_ref[b] = hb

    for b in range(nb):
        acc = jnp.zeros((lc, y_ref.shape[-1]), F32)
        for n in range(n_tiles * LANES // col_chunk):
            first = n * (col_chunk // LANES)
            tiles = [s_ref[b, :, SUBLANES * (first + c):SUBLANES * (first + c + 1), :].reshape(lc, LANES)
                     for c in range(col_chunk // LANES)]
            hk = jnp.concatenate(tiles, axis=1).astype(BF16)
            acc = acc + jnp.dot(hk, cbig_ref[n * col_chunk:(n + 1) * col_chunk, :], preferred_element_type=F32)
        y = acc + d_ref[...] * u_refs[b][...]
        z = _gelu_tanh(y)
        gate = _sigmoid(jnp.dot(z.astype(BF16), wglu_ref[...], preferred_element_type=F32) + bglu_ref[...])
        y_ref[b] = z * gate


def _ssm(u, row0, nb, seq, lc, h0, lam, bbig, cbig, dvec, wglu, bglu):
    d_ssm = u.shape[1]
    n_state_rows = lam.shape[0]
    n_chunks = seq // lc
    blk0 = row0 // lc
    u_specs = [pl.BlockSpec((lc, d_ssm), functools.partial(lambda i, b: (blk0 + b * n_chunks + i, 0), b=b))
               for b in range(nb)]
    return pl.pallas_call(
        functools.partial(_ssm_body, nb=nb, lc=lc, n_state_rows=n_state_rows),
        grid=(n_chunks,),
        in_specs=u_specs + [_const_spec(h0.shape), _const_spec(lam.shape), _const_spec(bbig.shape),
                            _const_spec(cbig.shape), _const_spec(dvec.shape), _const_spec(wglu.shape),
                            _const_spec(bglu.shape)],
        out_specs=[pl.BlockSpec((nb, lc, d_ssm), lambda i: (0, i, 0)), _const_spec(h0.shape)],
        out_shape=[jax.ShapeDtypeStruct((nb, seq, d_ssm), F32), jax.ShapeDtypeStruct(h0.shape, F32)],
        scratch_shapes=[pltpu.VMEM((nb, lc // SUBLANES, SUBLANES * n_state_rows, LANES), F32),
                        pltpu.VMEM(h0.shape, F32)],
        compiler_params=_cparams(1),
        name=f"ssm_b{nb}",
    )(*([u] * nb), h0, lam, bbig, cbig, dvec, wglu, bglu)


def _bias_rows(m_ref, lead, base, n_rows, n_cols):
    for r in range(n_rows):
        m_ref[lead, r:r + 1, :] = base[:, QBLK - 1 - r:QBLK - 1 - r + n_cols]


def _attn_prompt_body(q_ref, k_ref, v_ref, base_ref, o_ref, kb_ref, vb_ref, m_ref):
    j = pl.program_id(2)

    @pl.when(j == 0)
    def _():
        kb_ref[:BAND_ROWS, :] = jnp.zeros((BAND_ROWS, LANES), BF16)
        vb_ref[:BAND_ROWS, :] = jnp.zeros((BAND_ROWS, LANES), BF16)
        kb_ref[BAND_ROWS:, :] = k_ref[...].astype(BF16)
        vb_ref[BAND_ROWS:, :] = v_ref[...].astype(BF16)
        rq = lax.broadcasted_iota(I32, (QBLK, KWIN), 0) // CHUNK
        ck = lax.broadcasted_iota(I32, (QBLK, KWIN), 1) // CHUNK
        in_band = (ck >= rq) & (ck <= rq + LEFT_CHUNKS)
        for a in range(2):
            _bias_rows(m_ref, a, base_ref[0, a:a + 1, :], QBLK, KWIN)
            m_ref[a] = jnp.where(in_band, m_ref[a], NEG_INF)

    start = pl.multiple_of(j * (QSTEP * QBLK), QSTEP * QBLK)
    lane = lax.broadcasted_iota(I32, (QBLK, LANES), 1)
    col = lax.broadcasted_iota(I32, (QBLK, KWIN), 1)
    for c in range(QSTEP):
        q2 = q_ref[QBLK * c:QBLK * (c + 1), :]
        kw = kb_ref[pl.ds(start + QBLK * c, KWIN), :]
        vw = vb_ref[pl.ds(start + QBLK * c, KWIN), :]
        first_valid = BAND_ROWS - (j * QSTEP + c) * QBLK
        outs = []
        for a in range(2):
            qa = jnp.where((lane >= HEAD_DIM * a) & (lane < HEAD_DIM * (a + 1)), q2, jnp.zeros_like(q2))
            s = lax.dot_general(qa, kw, (((1,), (1,)), ((), ())), preferred_element_type=F32)
            s = jnp.where(col >= first_valid, s + m_ref[a], NEG_INF)
            p = jnp.exp(s - jnp.max(s, axis=1, keepdims=True))
            o = jnp.dot(p.astype(BF16), vw, preferred_element_type=F32)
            outs.append(o * (1.0 / jnp.sum(p, axis=1, keepdims=True)))
        o_ref[QBLK * c:QBLK * (c + 1), :] = jnp.where(lane < HEAD_DIM, outs[0], outs[1])


def _attn_prompt(q, k, v, base2, nb, seq):
    d_attn = q.shape[1]
    pairs = d_attn // LANES
    nq = seq // (QSTEP * QBLK)
    kv_spec = pl.BlockSpec((seq, LANES), lambda b, p, j: (b, p))
    return pl.pallas_call(
        _attn_prompt_body,
        grid=(nb, pairs, nq),
        in_specs=[pl.BlockSpec((QSTEP * QBLK, LANES), lambda b, p, j: (b * nq + j, p)), kv_spec, kv_spec,
                  pl.BlockSpec((1, 2, base2.shape[2]), lambda b, p, j: (p, 0, 0))],
        out_specs=pl.BlockSpec((QSTEP * QBLK, LANES), lambda b, p, j: (b * nq + j, p)),
        out_shape=jax.ShapeDtypeStruct((nb * seq, d_attn), F32),
        scratch_shapes=[pltpu.VMEM((seq + BAND_ROWS, LANES), BF16), pltpu.VMEM((seq + BAND_ROWS, LANES), BF16),
                        pltpu.VMEM((2, QBLK, KWIN), F32)],
        compiler_params=_cparams(3),
        name="attn_prompt",
    )(q, k, v, base2)


def _attn_sample_body(q_ref, k_ref, v_ref, ck_ref, cv_ref, base_ref, o_ref, m_ref, *, heads, n_cache):
    @pl.when(pl.program_id(0) == 0)
    def _():
        for h in range(heads):
            _bias_rows(m_ref, h, base_ref[h:h + 1, :], CHUNK, n_cache + CHUNK)

    q = q_ref[...]
    k = k_ref[...].astype(BF16)
    v = v_ref[...].astype(BF16)
    nt = (((1,), (1,)), ((), ()))
    outs = []
    for h in range(heads):
        sl = slice(HEAD_DIM * h, HEAD_DIM * (h + 1))
        qh = q[:, sl]
        s1 = lax.dot_general(qh, ck_ref[0, h].astype(BF16), nt, preferred_element_type=F32) + m_ref[h, :, :n_cache]
        s2 = lax.dot_general(qh, k[:, sl], nt, preferred_element_type=F32) + m_ref[h, :, n_cache:]
        mx = jnp.maximum(jnp.max(s1, axis=1, keepdims=True), jnp.max(s2, axis=1, keepdims=True))
        p1, p2 = jnp.exp(s1 - mx), jnp.exp(s2 - mx)
        den = jnp.sum(p1, axis=1, keepdims=True) + jnp.sum(p2, axis=1, keepdims=True)
        o = (jnp.dot(p1.astype(BF16), cv_ref[0, h].astype(BF16), preferred_element_type=F32)
             + jnp.dot(p2.astype(BF16), v[:, sl], preferred_element_type=F32))
        outs.append(o * (1.0 / den))
    o_ref[...] = jnp.concatenate(outs, axis=1)


def _attn_sample(q, k, v, cache_k, cache_v, base, row0):
    nb, heads, n_cache, _ = cache_k.shape
    d_attn = q.shape[1]
    blk0 = row0 // CHUNK
    row = pl.BlockSpec((CHUNK, d_attn), lambda b: (blk0 + b, 0))
    cache = pl.BlockSpec((1, heads, n_cache, HEAD_DIM), lambda b: (b, 0, 0, 0))
    return pl.pallas_call(
        functools.partial(_attn_sample_body, heads=heads, n_cache=n_cache),
        grid=(nb,),
        in_specs=[row, row, row, cache, cache, _const_spec(base.shape)],
        out_specs=pl.BlockSpec((CHUNK, d_attn), lambda b: (b, 0)),
        out_shape=jax.ShapeDtypeStruct((nb * CHUNK, d_attn), F32),
        scratch_shapes=[pltpu.VMEM((heads, CHUNK, n_cache + CHUNK), F32)],
        compiler_params=_cparams(1),
        name="attn_sample",
    )(q, k, v, cache_k, cache_v, base)


def _bias_base(table):
    heads = table.shape[0]
    far = jnp.broadcast_to(table[:, 2 * MAX_REL:], (heads, KWIN - 1 - MAX_REL + 1))
    near = table[:, 2 * MAX_REL - 1:0:-1]
    base = jnp.concatenate([far, near], axis=1)
    return jnp.pad(base, ((0, 0), (0, KWIN + QBLK - base.shape[1])))


def _post_body(*refs, n_x, n_y, n_pb, tm):
    i = pl.program_id(0)
    pos = 0

    def take(n):
        nonlocal pos
        out = refs[pos:pos + n]
        pos += n
        return out

    def merged(pair):
        if len(pair) == 1:
            return pair[0][...]
        return jnp.where(i < n_pb, pair[0][...], pair[1][...])

    x_refs = take(n_x)
    y_pairs = [take(2) for _ in range(n_y)]
    w_refs = take(n_y)
    g_ref, wrh_ref, wrl_ref, br_ref, tril_ref, cin_ref = take(6)
    xo_ref, xp_ref, mi_ref, mf_ref, cnt_ref = take(5)
    (carry_ref,) = take(1)

    x = merged(x_refs)
    for pair, w_ref in zip(y_pairs, w_refs):
        x = x + jnp.dot(merged(pair).astype(BF16), w_ref[...], preferred_element_type=F32)
    xo_ref[...] = x

    xn = _rms(x, g_ref[...])
    for c in range(ROWS_O):
        xp_ref[pl.ds(c, tm, stride=ROWS_O), :] = xn[:, LANES * c:LANES * (c + 1)]

    xh, xl = _split_bf16(xn)
    logits = (jnp.dot(xh, wrh_ref[...], preferred_element_type=F32)
              + jnp.dot(xl, wrh_ref[...], preferred_element_type=F32)
              + jnp.dot(xh, wrl_ref[...], preferred_element_type=F32)
              + jnp.dot(xl, wrl_ref[...], preferred_element_type=F32)) + br_ref[...]
    lane = lax.broadcasted_iota(I32, logits.shape, 1)

    def first_argmax(vals):
        top = jnp.max(vals, axis=1, keepdims=True)
        return top, jnp.min(jnp.where(vals == top, lane, LANES), axis=1, keepdims=True)

    is_group = lane < MOE_GROUPS
    g_top, g_idx = first_argmax(jnp.where(is_group, logits, -jnp.inf))
    g_w = 1.0 / jnp.sum(jnp.where(is_group, jnp.exp(logits - g_top), 0.0), axis=1, keepdims=True)
    first = MOE_GROUPS + EXPERTS_PER_GROUP * g_idx
    e_log = jnp.where((lane >= first) & (lane < first + EXPERTS_PER_GROUP), logits, -jnp.inf)
    v1, l1 = first_argmax(e_log)
    v2, l2 = first_argmax(jnp.where(lane == l1, -jnp.inf, e_log))
    t = jnp.exp(v2 - v1)
    w1 = 1.0 / (1.0 + t)
    w2 = t * w1
    e1, e2 = l1 - MOE_GROUPS, l2 - MOE_GROUPS

    @pl.when(i == 0)
    def _():
        carry_ref[...] = cin_ref[...]

    hit1, hit2 = lane == e1, lane == e2
    oh = jnp.where(hit1 | hit2, 1.0, 0.0)
    before = jnp.dot(tril_ref[...], oh.astype(BF16), preferred_element_type=F32) + carry_ref[...]
    r1 = jnp.sum(jnp.where(hit1, before, 0.0), axis=1, keepdims=True).astype(I32)
    r2 = jnp.sum(jnp.where(hit2, before, 0.0), axis=1, keepdims=True).astype(I32)
    carry_ref[...] = carry_ref[...] + jnp.sum(oh, axis=0, keepdims=True)
    cnt_ref[...] = carry_ref[...]

    mi_ref[...] = jnp.where(lane == 0, e1, jnp.where(lane == 1, e2, jnp.where(lane == 2, r1, r2)))
    mf_ref[...] = jnp.where(lane == 0, g_w * w1, g_w * w2)


def _post_mixer(x_src, y_pairs, ws, g, router, tril, t_p, t_s):
    wrh, wrl, br = router
    d = ws[0].shape[1]
    n_pb = t_p // TM
    n_tot = (t_p + t_s) // TM
    row = lambda width: pl.BlockSpec((TM, width), lambda i: (i, 0))
    in_specs, args = [], []
    if len(x_src) == 1:
        in_specs.append(row(d))
    else:
        in_specs += _pair_specs(TM, d, n_pb)
    args += list(x_src)
    for yp, ys in y_pairs:
        in_specs += _pair_specs(TM, yp.shape[1], n_pb)
        args += [yp, ys]
    consts = list(ws) + [g.reshape(1, d), wrh, wrl, br, tril, jnp.zeros((1, LANES), F32)]
    in_specs += [_const_spec(c.shape) for c in consts]
    t = t_p + t_s
    return pl.pallas_call(
        functools.partial(_post_body, n_x=len(x_src), n_y=len(y_pairs), n_pb=n_pb, tm=TM),
        grid=(n_tot,),
        in_specs=in_specs,
        out_specs=[row(d), pl.BlockSpec((TM * ROWS_O, LANES), lambda i: (i, 0)), row(LANES), row(LANES),
                   _const_spec((1, LANES))],
        out_shape=[jax.ShapeDtypeStruct((t, d), F32), jax.ShapeDtypeStruct((t * ROWS_O, LANES), F32),
                   jax.ShapeDtypeStruct((t, LANES), I32), jax.ShapeDtypeStruct((t, LANES), F32),
                   jax.ShapeDtypeStruct((1, LANES), F32)],
        scratch_shapes=[pltpu.VMEM((1, LANES), F32)],
        compiler_params=_cparams(1),
        name="post_mixer",
    )(*args, *consts)


def _router_weights(gate_w, gate_b, router_w, router_b):
    d = gate_w.shape[0]
    w = jnp.concatenate([gate_w, router_w.transpose(1, 0, 2).reshape(d, N_EXPERTS)], axis=1)
    b = jnp.concatenate([gate_b, router_b.reshape(N_EXPERTS)])
    w = jnp.pad(w, ((0, 0), (0, LANES - w.shape[1])))
    b = jnp.pad(b, (0, LANES - b.shape[0])).reshape(1, LANES)
    hi, lo = _split_bf16(w)
    return hi, lo, b


def _dispatch_body(slots_ref, pad_pos_ref, pad_n_ref, x_ref, xs_ref, zero_ref, sem, *, tm):
    i = pl.program_id(0)

    def zero_copy(pos, n):
        return pltpu.make_async_copy(zero_ref.at[pl.ds(0, n)], xs_ref.at[pl.ds(pos, n)], sem.at[1])

    @pl.when(i == 0)
    def _():
        zero_ref[...] = jnp.zeros(zero_ref.shape, zero_ref.dtype)
        for e in range(N_EXPERTS):
            pos, n = pad_pos_ref[e], pad_n_ref[e]
            bit = tm // 2
            while bit >= 1:
                @pl.when((n & bit) != 0)
                def _(pos=pos, bit=bit):
                    cp = zero_copy(pos, bit)
                    cp.start()
                    cp.wait()
                pos = pos + (n & bit)
                bit //= 2
        used_tiles = (pad_pos_ref[N_EXPERTS - 1] + pad_n_ref[N_EXPERTS - 1]) // tm
        all_tiles = xs_ref.shape[0] // tm
        lax.fori_loop(used_tiles, all_tiles, lambda tl, c: (zero_copy(tl * tm, tm).start(), c)[1], 0)
        lax.fori_loop(used_tiles, all_tiles, lambda tl, c: (zero_copy(tl * tm, tm).wait(), c)[1], 0)

    def start(r, c):
        for k in range(2):
            slot = slots_ref[2 * (i * tm + r) + k]
            pltpu.make_async_copy(x_ref.at[r], xs_ref.at[slot], sem.at[0]).start(priority=k)
        return c

    lax.fori_loop(0, tm, start, 0, unroll=8)
    pltpu.make_async_copy(xs_ref.at[pl.ds(0, 2 * tm)], xs_ref.at[pl.ds(0, 2 * tm)], sem.at[0]).wait()


def _dispatch(xrows, slots, pad_pos, pad_n, n_slots):
    t = xrows.shape[0]
    slab = xrows.shape[1:]
    grid_spec = pltpu.PrefetchScalarGridSpec(
        num_scalar_prefetch=3,
        grid=(t // TM_MOE,),
        in_specs=[pl.BlockSpec((TM_MOE,) + slab, lambda i, *_: (i, 0, 0))],
        out_specs=pl.BlockSpec(memory_space=pl.ANY),
        scratch_shapes=[pltpu.VMEM((TM_MOE,) + slab, xrows.dtype), pltpu.SemaphoreType.DMA((2,))],
    )
    return pl.pallas_call(
        functools.partial(_dispatch_body, tm=TM_MOE),
        grid_spec=grid_spec,
        out_shape=jax.ShapeDtypeStruct((n_slots,) + slab, xrows.dtype),
        compiler_params=_cparams(1),
        name="moe_dispatch",
    )(slots, pad_pos, pad_n, xrows)


def _moe_body(te_ref, nt_ref, x_ref, wgu_ref, wd_ref, o_ref, wgu_bf, wd_bf, *, tm):
    i = pl.program_id(0)
    prev = te_ref[jnp.maximum(i - 1, 0)]

    @pl.when((i == 0) | (te_ref[i] != prev))
    def _():
        wgu_bf[...] = wgu_ref[0].astype(BF16)
        wd_bf[...] = wd_ref[0].astype(BF16)

    @pl.when(i < nt_ref[0])
    def _():
        x = jnp.concatenate([x_ref[pl.ds(c, tm, stride=ROWS_O), :] for c in range(ROWS_O)], axis=1)
        h = jnp.dot(x.astype(BF16), wgu_bf[...], preferred_element_type=F32)
        d_exp = h.shape[1] // 2
        gt, up = h[:, :d_exp], h[:, d_exp:]
        act = (gt * _sigmoid(gt) * up).astype(BF16)
        out = jnp.dot(act, wd_bf[...], preferred_element_type=F32)
        for c in range(ROWS_O):
            o_ref[pl.ds(c, tm, stride=ROWS_O), :] = out[:, LANES * c:LANES * (c + 1)]

    @pl.when(i >= nt_ref[0])
    def _():
        o_ref[...] = jnp.zeros(o_ref.shape, o_ref.dtype)


def _moe_experts(xsorted, tile_expert, n_tiles, wgu, wd):
    n_slots = xsorted.shape[0] // ROWS_O
    n_max = n_slots // TM_MOE
    grid_spec = pltpu.PrefetchScalarGridSpec(
        num_scalar_prefetch=2,
        grid=(n_max,),
        in_specs=[pl.BlockSpec((TM_MOE * ROWS_O, LANES), lambda i, te, nt: (jnp.minimum(i, nt[0] - 1), 0)),
                  pl.BlockSpec((1,) + wgu.shape[1:], lambda i, te, nt: (te[i], 0, 0)),
                  pl.BlockSpec((1,) + wd.shape[1:], lambda i, te, nt: (te[i], 0, 0))],
        out_specs=pl.BlockSpec((TM_MOE * ROWS_O, LANES), lambda i, te, nt: (i, 0)),
        scratch_shapes=[pltpu.VMEM(wgu.shape[1:], BF16), pltpu.VMEM(wd.shape[1:], BF16)],
    )
    return pl.pallas_call(
        functools.partial(_moe_body, tm=TM_MOE),
        grid_spec=grid_spec,
        out_shape=jax.ShapeDtypeStruct((n_slots * ROWS_O, LANES), F32),
        compiler_params=_cparams(1),
        name="moe_experts",
    )(tile_expert, n_tiles, xsorted, wgu, wd)


def _combine_body(slots_ref, x_ref, mf_ref, os_ref, *rest, tm, n_steps, n_pb):
    out_refs, (buf_ref, sem) = rest[:-2], rest[-2:]
    i = pl.program_id(0)

    def issue(tile, par):
        def body(r, c):
            for k in range(2):
                slot = slots_ref[2 * (tile * tm + r) + k]
                dst = buf_ref.at[par, k, pl.ds(pl.multiple_of(r * ROWS_O, ROWS_O), ROWS_O)]
                pltpu.make_async_copy(os_ref.at[slot], dst, sem.at[par]).start(priority=k)
            return c
        lax.fori_loop(0, tm, body, 0, unroll=8)

    @pl.when(i == 0)
    def _():
        issue(0, 0)

    @pl.when(i + 1 < n_steps)
    def _():
        issue(i + 1, (i + 1) % 2)

    par = i % 2
    pltpu.make_async_copy(buf_ref.at[par], buf_ref.at[par], sem.at[par]).wait()

    def write(out_ref):
        mf = mf_ref[...]
        g1, g2 = mf[:, 0:1], mf[:, 1:2]
        for c in range(ROWS_O):
            cols = slice(LANES * c, LANES * (c + 1))
            piece = lambda k: buf_ref[par, k, pl.ds(c, tm, stride=ROWS_O), :]
            out_ref[:, cols] = x_ref[:, cols] + g1 * piece(0) + g2 * piece(1)

    if len(out_refs) == 1:
        write(out_refs[0])
    else:
        pl.when(i < n_pb)(lambda: write(out_refs[0]))
        pl.when(i >= n_pb)(lambda: write(out_refs[1]))


def _combine(x, meta_f, slots, osorted, t_p, split):
    t, d = x.shape
    n_steps = t // TM_MOE
    n_pb = t_p // TM_MOE
    row = lambda width: pl.BlockSpec((TM_MOE, width), lambda i, *_: (i, 0))
    if split:
        out_specs = [pl.BlockSpec((TM_MOE, d), lambda i, *_: (jnp.minimum(i, n_pb - 1), 0)),
                     pl.BlockSpec((TM_MOE, d), lambda i, *_: (jnp.maximum(i - n_pb, 0), 0))]
        out_shape = [jax.ShapeDtypeStruct((t_p, d), F32), jax.ShapeDtypeStruct((t - t_p, d), F32)]
    else:
        out_specs = [row(d)]
        out_shape = [jax.ShapeDtypeStruct((t, d), F32)]
    grid_spec = pltpu.PrefetchScalarGridSpec(
        num_scalar_prefetch=1,
        grid=(n_steps,),
        in_specs=[row(d), row(LANES), pl.BlockSpec(memory_space=pl.ANY)],
        out_specs=out_specs,
        scratch_shapes=[pltpu.VMEM((2, 2, TM_MOE * ROWS_O, LANES), F32), pltpu.SemaphoreType.DMA((2,))],
    )
    return pl.pallas_call(
        functools.partial(_combine_body, tm=TM_MOE, n_steps=n_steps, n_pb=n_pb),
        grid_spec=grid_spec,
        out_shape=out_shape,
        compiler_params=_cparams(1),
        name="moe_combine",
    )(slots, x, meta_f, osorted)


def _moe_layer(x, xrows, meta_i, meta_f, counts, wgu, wd, t_p, split):
    t = x.shape[0]
    n_max = (2 * t + N_EXPERTS * (TM_MOE - 1)) // TM_MOE
    cnt = counts[0, :N_EXPERTS].astype(I32)
    padded = (cnt + TM_MOE - 1) // TM_MOE * TM_MOE
    ends = jnp.cumsum(padded)
    offs = ends - padded
    experts = jnp.arange(N_EXPERTS, dtype=I32)
    first_slot = jnp.sum(jnp.where(meta_i[:, 0:2, None] == experts, offs, 0), axis=-1)
    slots = (first_slot + meta_i[:, 2:4]).reshape(-1)
    n_tiles = (ends[-1] // TM_MOE).reshape(1)
    tile_start = jnp.arange(n_max, dtype=I32) * TM_MOE
    tile_expert = jnp.minimum(jnp.sum((ends[None, :] <= tile_start[:, None]).astype(I32), axis=1), N_EXPERTS - 1)
    n_slots = n_max * TM_MOE
    xsorted = _dispatch(xrows.reshape(t, ROWS_O, LANES), slots, offs + cnt, padded - cnt, n_slots)
    osorted = _moe_experts(xsorted.reshape(n_slots * ROWS_O, LANES), tile_expert, n_tiles, wgu, wd)
    return _combine(x, meta_f, slots, osorted.reshape(n_slots, ROWS_O, LANES), t_p, split)


def _conv_in_body(x_ref, g_ref, w_ref, u_ref):
    xn = _rms(x_ref[...], g_ref[...]).astype(BF16)
    h = jnp.dot(xn, w_ref[...], preferred_element_type=F32)
    c = h.shape[1] // 2
    u_ref[...] = h[:, :c] * _sigmoid(h[:, c:])


def _conv_in(x, g, w):
    t, d = x.shape
    c = w.shape[1] // 2
    return pl.pallas_call(
        _conv_in_body,
        grid=(t // TM,),
        in_specs=[pl.BlockSpec((TM, d), lambda i: (i, 0)), _const_spec((1, d)), _const_spec(w.shape)],
        out_specs=pl.BlockSpec((TM, c), lambda i: (i, 0)),
        out_shape=jax.ShapeDtypeStruct((t, c), F32),
        compiler_params=_cparams(1),
        name="conv_in",
    )(x, g.reshape(1, d), w.astype(BF16))


def _dwconv_body(cur_ref, prev_ref, left_ref, w_ref, b_ref, g_ref, beta_ref, o_ref, win_ref, y_ref, *, tm):
    i = pl.program_id(1)
    n_lt = cur_ref.shape[1] // LANES
    halo_blocks = CONV_HALO // SUBLANES
    halo = jnp.where(i == 0, left_ref[0], prev_ref[...])
    cur = cur_ref[...]
    for c in range(n_lt):
        rows, cols = slice(SUBLANES * c, SUBLANES * (c + 1)), slice(LANES * c, LANES * (c + 1))
        win_ref[:halo_blocks, rows, :] = halo[:, cols].reshape(halo_blocks, SUBLANES, LANES)
        win_ref[halo_blocks:, rows, :] = cur[:, cols].reshape(tm // SUBLANES, SUBLANES, LANES)
    first_tap = CONV_HALO - (CONV_WIDTH - 1)
    taps = [w_ref[j] for j in range(CONV_WIDTH)]

    def block(rb, carry):
        for t in range(SUBLANES):
            acc = None
            for j in range(CONV_WIDTH):
                u = t + j + first_tap
                term = taps[j] * win_ref[rb + u // SUBLANES, pl.ds(u % SUBLANES, n_lt, stride=SUBLANES), :]
                acc = term if acc is None else acc + term
            y_ref[rb, pl.ds(t, n_lt, stride=SUBLANES), :] = acc
        return carry

    lax.fori_loop(0, tm // SUBLANES, block, 0)
    y = jnp.concatenate([y_ref[:, SUBLANES * c:SUBLANES * (c + 1), :].reshape(tm, LANES) for c in range(n_lt)], axis=1)
    y = y + b_ref[...]
    mu = jnp.mean(y, axis=1, keepdims=True)
    yc = y - mu
    yn = yc * lax.rsqrt(jnp.mean(yc * yc, axis=1, keepdims=True) + EPS) * g_ref[...] + beta_ref[...]
    o_ref[...] = (yn * _sigmoid(yn)).astype(BF16)


def _dwconv(u, left, dw_w, dw_b, ln_g, ln_b, row0, nb, seq, tm):
    c = u.shape[1]
    n_t = seq // tm
    blk0 = row0 // tm
    halo_per_tile = tm // CONV_HALO
    vec = lambda a: a.reshape(1, c)
    w = dw_w.reshape(CONV_WIDTH, c // LANES, LANES)
    return pl.pallas_call(
        functools.partial(_dwconv_body, tm=tm),
        grid=(nb, n_t),
        in_specs=[pl.BlockSpec((tm, c), lambda b, i: (blk0 + b * n_t + i, 0)),
                  pl.BlockSpec((CONV_HALO, c),
                               lambda b, i: (jnp.maximum((blk0 + b * n_t + i) * halo_per_tile - 1, 0), 0)),
                  pl.BlockSpec((1, CONV_HALO, c), lambda b, i: (b, 0, 0)),
                  _const_spec(w.shape), _const_spec((1, c)), _const_spec((1, c)), _const_spec((1, c))],
        out_specs=pl.BlockSpec((tm, c), lambda b, i: (b * n_t + i, 0)),
        out_shape=jax.ShapeDtypeStruct((nb * seq, c), BF16),
        scratch_shapes=[pltpu.VMEM(((tm + CONV_HALO) // SUBLANES, c // LANES * SUBLANES, LANES), F32),
                        pltpu.VMEM((tm // SUBLANES, c // LANES * SUBLANES, LANES), F32)],
        compiler_params=_cparams(2),
        name=f"dwconv_b{nb}",
    )(u, u, left, w, vec(dw_b), vec(ln_g), vec(ln_b))


def _block_diag(blocks):
    g, a, b = blocks.shape
    return jnp.einsum("gab,gh->gahb", blocks, jnp.eye(g, dtype=blocks.dtype)).reshape(g * a, g * b)


def _state_rows(re, im):
    nb = re.shape[0]
    return jnp.concatenate([re.reshape(nb, -1, LANES), im.reshape(nb, -1, LANES)], axis=1)


def kernel(x_prompt, x_sample, cache_attn_k, cache_attn_v, state_ssm_re, state_ssm_im, cache_conv, norm_mixer, norm_ffn, w_in_even, w_out_even, ssm_a_re, ssm_a_im, ssm_log_dt, ssm_b_re, ssm_b_im, ssm_c_re, ssm_c_im, ssm_d, ssm_glu_w, ssm_glu_b, attn_q_gain, attn_k_gain, attn_rel_bias, conv_w_in, conv_dw_w, conv_dw_b, conv_ln_g, conv_ln_b, conv_w_out, moe_gate_w, moe_gate_b, moe_router_w, moe_router_b, moe_w_gate_up, moe_w_down):
    nb_p, seq_p, d = x_prompt.shape
    nb_s, seq_s, _ = x_sample.shape
    t_p, t_s = nb_p * seq_p, nb_s * seq_s
    xp, xs = x_prompt.reshape(t_p, d), x_sample.reshape(t_s, d)
    depth = norm_mixer.shape[0]
    tril = jnp.tril(jnp.ones((TM, TM), F32), -1).astype(BF16)

    ks_p, vs_p, hs_p, ks_s, vs_s, hs_s, conv_p, conv_s = [], [], [], [], [], [], [], []
    x_src = (xp, xs)
    for layer in range(depth):
        j = layer // 2
        if layer % 2 == 0:
            u, q, k, v = _inproj(*x_src, norm_mixer[layer], w_in_even[j], attn_q_gain[j], attn_k_gain[j])
            d_ssm = u.shape[1]
            groups, n_state = ssm_a_re[j].shape
            bar_re, bar_im, bb_re, bb_im = _ssm_discretize(ssm_a_re[j], ssm_a_im[j], ssm_log_dt[j],
                                                           ssm_b_re[j], ssm_b_im[j])
            lam = jnp.concatenate([bar_re.reshape(-1, LANES), bar_im.reshape(-1, LANES)], axis=0)
            to_state = lambda m: _block_diag(m.reshape(groups, n_state, SSM_GROUP).transpose(0, 2, 1))
            bbig = jnp.concatenate([to_state(bb_re), to_state(bb_im)], axis=1).astype(BF16)
            cbig = jnp.concatenate([_block_diag(ssm_c_re[j].transpose(0, 2, 1)),
                                    _block_diag(-ssm_c_im[j].transpose(0, 2, 1))], axis=0).astype(BF16)
            ssm_args = (lam, bbig, cbig, ssm_d[j].reshape(1, d_ssm), _block_diag(ssm_glu_w[j]).astype(BF16),
                        ssm_glu_b[j].reshape(1, d_ssm))
            zero_state = jnp.zeros((nb_p, groups, n_state), F32)
            y_ssm_p, h_p = _ssm(u, 0, nb_p, seq_p, 256, _state_rows(zero_state, zero_state), *ssm_args)
            y_ssm_s, h_s = _ssm(u, t_p, nb_s, seq_s, seq_s, _state_rows(state_ssm_re[j], state_ssm_im[j]), *ssm_args)
            base = _bias_base(attn_rel_bias[j])
            heads = base.shape[0]
            y_att_p = _attn_prompt(q, k, v, base.reshape(heads // 2, 2, -1), nb_p, seq_p)
            y_att_s = _attn_sample(q, k, v, cache_attn_k[j], cache_attn_v[j], base, t_p)
            w_out = w_out_even[j].astype(BF16)
            y_pairs = [(y_ssm_p.reshape(t_p, d_ssm), y_ssm_s.reshape(t_s, d_ssm)), (y_att_p, y_att_s)]
            ws = [w_out[:d_ssm], w_out[d_ssm:]]

            keep = min(BAND_ROWS, seq_p)
            heads_of = lambda a, nb: a.reshape(nb, -1, heads, HEAD_DIM).transpose(0, 2, 1, 3)
            tail_p = lambda a: jnp.concatenate([a[(b + 1) * seq_p - keep:(b + 1) * seq_p] for b in range(nb_p)])
            ks_p.append(heads_of(tail_p(k), nb_p))
            vs_p.append(heads_of(tail_p(v), nb_p))
            ks_s.append(heads_of(k[t_p:], nb_s))
            vs_s.append(heads_of(v[t_p:], nb_s))
            half = h_p.shape[1] // 2
            hs_p.append((h_p[:, :half].reshape(nb_p, groups, n_state), h_p[:, half:].reshape(nb_p, groups, n_state)))
            hs_s.append((h_s[:, :half].reshape(nb_s, groups, n_state), h_s[:, half:].reshape(nb_s, groups, n_state)))
        else:
            (x_merged,) = x_src
            uc = _conv_in(x_merged, norm_mixer[layer], conv_w_in[j])
            c = uc.shape[1]
            conv_args = (conv_dw_w[j], conv_dw_b[j], conv_ln_g[j], conv_ln_b[j])
            left_p = jnp.zeros((nb_p, CONV_HALO, c), F32)
            left_s = jnp.pad(cache_conv[j], ((0, 0), (CONV_HALO - (CONV_WIDTH - 1), 0), (0, 0)))
            act_p = _dwconv(uc, left_p, *conv_args, 0, nb_p, seq_p, TM_CONV)
            act_s = _dwconv(uc, left_s, *conv_args, t_p, nb_s, seq_s, seq_s)
            y_pairs = [(act_p, act_s)]
            ws = [conv_w_out[j].astype(BF16)]
            keep = CONV_WIDTH - 1
            conv_p.append(jnp.stack([uc[(b + 1) * seq_p - keep:(b + 1) * seq_p] for b in range(nb_p)]))
            ext_s = jnp.concatenate([cache_conv[j], uc[t_p:].reshape(nb_s, seq_s, c)], axis=1)
            conv_s.append(ext_s[:, ext_s.shape[1] - keep:])

        router = _router_weights(moe_gate_w[layer], moe_gate_b[layer], moe_router_w[layer], moe_router_b[layer])
        x_new, xrows, meta_i, meta_f, counts = _post_mixer(x_src, y_pairs, ws, norm_ffn[layer], router, tril, t_p, t_s)
        last = layer == depth - 1
        out = _moe_layer(x_new, xrows, meta_i, meta_f, counts, moe_w_gate_up[layer], moe_w_down[layer], t_p, last)
        x_src = tuple(out)

    y_prompt, y_sample = x_src
    stack_c = lambda parts, idx: jnp.stack([p[idx] for p in parts])
    return (y_prompt.reshape(nb_p, seq_p, d), y_sample.reshape(nb_s, seq_s, d),
            jnp.stack(ks_p), jnp.stack(vs_p), stack_c(hs_p, 0), stack_c(hs_p, 1), jnp.stack(conv_p),
            jnp.stack(ks_s), jnp.stack(vs_s), stack_c(hs_s, 0), stack_c(hs_s, 1), jnp.stack(conv_s))
```

```python
import functools
import math

import jax
import jax.numpy as jnp
from jax import lax
from jax.experimental import pallas as pl
from jax.experimental.pallas import tpu as pltpu

F32, BF16, I32, U32 = jnp.float32, jnp.bfloat16, jnp.int32, jnp.uint32

CHUNK = 64
SSM_GROUP = 16
SSM_STATE = 64
HEAD_DIM = 64
LEFT_CHUNKS = 8
BAND_ROWS = LEFT_CHUNKS * CHUNK
MAX_REL = 128
CONV_WIDTH = 31
MOE_GROUPS = 4
EXPERTS_PER_GROUP = 8
N_EXPERTS = MOE_GROUPS * EXPERTS_PER_GROUP
EPS = 1e-6
NEG_INF = -1e30

LANES = 128
SUBLANES = 8
VMEM_LIMIT_BYTES = 56 * 1024 * 1024

TM = 512
TM_MOE = 256
QBLK = 2 * CHUNK
KWIN = QBLK + BAND_ROWS
QSTEP = 4
D_MODEL = 1024
ROWS_O = D_MODEL // LANES
CONV_HALO = 32
TM_CONV = 256


def _cparams(n_grid):
    return pltpu.CompilerParams(dimension_semantics=("arbitrary",) * n_grid, vmem_limit_bytes=VMEM_LIMIT_BYTES)


def _sigmoid(x):
    return 1.0 / (1.0 + jnp.exp(-x))


def _rms(x, g):
    return x * lax.rsqrt(jnp.mean(x * x, axis=-1, keepdims=True) + EPS) * g


def _split_bf16(x):
    hi = lax.bitcast_convert_type(lax.bitcast_convert_type(x, U32) & jnp.uint32(0xFFFF0000), F32)
    return hi.astype(BF16), (x - hi).astype(BF16)


def _pair_specs(tm, width, n_prompt_blocks):
    return [pl.BlockSpec((tm, width), lambda i: (jnp.minimum(i, n_prompt_blocks - 1), 0)),
            pl.BlockSpec((tm, width), lambda i: (jnp.maximum(i - n_prompt_blocks, 0), 0))]


def _const_spec(shape):
    nd = len(shape)
    return pl.BlockSpec(shape, lambda *a: (0,) * nd)


def _inproj_body(xp_ref, xs_ref, g_ref, w_ref, qg_ref, kg_ref, seg_ref, u_ref, q_ref, k_ref, v_ref, *, n_pb, d_half):
    i = pl.program_id(0)
    x = jnp.where(i < n_pb, xp_ref[...], xs_ref[...])
    xn = _rms(x, g_ref[...]).astype(BF16)
    proj = jnp.dot(xn, w_ref[...], preferred_element_type=F32)
    u_ref[...] = proj[:, :d_half]
    v_ref[...] = proj[:, 3 * d_half:]

    def head_norm(t, gain):
        hi, lo = _split_bf16(t * t)
        msq = (jnp.dot(hi, seg_ref[...], preferred_element_type=F32)
               + jnp.dot(lo, seg_ref[...], preferred_element_type=F32))
        return t * lax.rsqrt(msq + EPS) * gain

    q_ref[...] = (head_norm(proj[:, d_half:2 * d_half], qg_ref[...]) * (HEAD_DIM ** -0.5)).astype(BF16)
    k_ref[...] = head_norm(proj[:, 2 * d_half:3 * d_half], kg_ref[...])


def _inproj(xp, xs, g, w, qg, kg):
    t_p, d = xp.shape
    t_s = xs.shape[0]
    d_half = w.shape[1] // 4
    n_pb = t_p // TM
    n_tot = (t_p + t_s) // TM
    heads = d_half // HEAD_DIM
    seg = jnp.kron(jnp.eye(heads, dtype=F32), jnp.full((HEAD_DIM, HEAD_DIM), 1.0 / HEAD_DIM, F32)).astype(BF16)
    out_f = jax.ShapeDtypeStruct((t_p + t_s, d_half), F32)
    out_b = jax.ShapeDtypeStruct((t_p + t_s, d_half), BF16)
    row = pl.BlockSpec((TM, d_half), lambda i: (i, 0))
    return pl.pallas_call(
        functools.partial(_inproj_body, n_pb=n_pb, d_half=d_half),
        grid=(n_tot,),
        in_specs=_pair_specs(TM, d, n_pb) + [_const_spec((1, d)), _const_spec(w.shape), _const_spec((1, d_half)),
                                             _const_spec((1, d_half)), _const_spec(seg.shape)],
        out_specs=[row, row, row, row],
        out_shape=[out_f, out_b, out_f, out_f],
        compiler_params=_cparams(1),
        name="inproj_even",
    )(xp, xs, g.reshape(1, d), w.astype(BF16), jnp.tile(qg, heads).reshape(1, d_half),
      jnp.tile(kg, heads).reshape(1, d_half), seg)


def _ssm_disc_body(are_ref, aim_ref, ldt_ref, bre_ref, bim_ref, barre_ref, barim_ref, bbre_ref, bbim_ref):
    lam_re, lam_im = are_ref[...], aim_ref[...]
    dt = jnp.exp(ldt_ref[...])
    mag = jnp.exp(lam_re * dt)
    bar_re, bar_im = mag * jnp.cos(lam_im * dt), mag * jnp.sin(lam_im * dt)
    den = lam_re * lam_re + lam_im * lam_im
    num_re = bar_re - 1.0
    coef_re = (num_re * lam_re + bar_im * lam_im) / den
    coef_im = (bar_im * lam_re - num_re * lam_im) / den
    br, bi = bre_ref[...], bim_ref[...]
    barre_ref[...] = bar_re
    barim_ref[...] = bar_im
    bbre_ref[...] = coef_re * br - coef_im * bi
    bbim_ref[...] = coef_re * bi + coef_im * br


def _ssm_discretize(a_re, a_im, log_dt, b_re, b_im):
    g, p = a_re.shape
    n = g * p
    col = jax.ShapeDtypeStruct((n, 1), F32)
    mat = jax.ShapeDtypeStruct((n, SSM_GROUP), F32)
    return pl.pallas_call(_ssm_disc_body, out_shape=[col, col, mat, mat], name="ssm_discretize")(
        a_re.reshape(n, 1), a_im.reshape(n, 1), jnp.repeat(log_dt, p).reshape(n, 1),
        b_re.reshape(n, SSM_GROUP), b_im.reshape(n, SSM_GROUP))


def _gelu_tanh(x):
    return 0.5 * x * (1.0 + jnp.tanh(math.sqrt(2.0 / math.pi) * (x + 0.044715 * (x * x * x))))


def _ssm_body(*refs, nb, lc, n_state_rows):
    u_refs = refs[:nb]
    h0_ref, lam_ref, bq_ref, cq_ref, d_ref, wglu_ref, bglu_ref, y_ref, ht_ref, s_ref, hcar_ref = refs[nb:]
    i = pl.program_id(0)
    half = n_state_rows // 2
    n_q = bq_ref.shape[0]
    tiles_q = n_state_rows // n_q
    k_q = bq_ref.shape[1]

    @pl.when(i == 0)
    def _():
        hcar_ref[...] = h0_ref[...]

    for b in range(nb):
        ub = u_refs[b][...].astype(BF16)
        for q in range(n_q):
            m = q % (n_q // 2)
            res = jnp.dot(ub[:, k_q * m:k_q * (m + 1)], bq_ref[q], preferred_element_type=F32)
            for c in range(tiles_q):
                s_ref[b, pl.ds(q * tiles_q + c, lc, stride=n_state_rows), :] = res[:, LANES * c:LANES * (c + 1)]

    lam_re, lam_im = lam_ref[:half, :], lam_ref[half:, :]

    def block(r, hs):
        hs = list(hs)
        for t in range(SUBLANES):
            row = pl.multiple_of((r * SUBLANES + t) * n_state_rows, n_state_rows)
            for b in range(nb):
                bu = s_ref[b, pl.ds(row, n_state_rows), :]
                hr, hi = hs[2 * b], hs[2 * b + 1]
                nr = lam_re * hr - lam_im * hi + bu[:half]
                ni = lam_re * hi + lam_im * hr + bu[half:]
                s_ref[b, pl.ds(row, n_state_rows), :] = jnp.concatenate([nr, ni], axis=0)
                hs[2 * b], hs[2 * b + 1] = nr, ni
        return tuple(hs)

    init = []
    for b in range(nb):
        init += [hcar_ref[b, :half, :], hcar_ref[b, half:, :]]
    fin = lax.fori_loop(0, lc // SUBLANES, block, tuple(init))
    for b in range(nb):
        hb = jnp.concatenate([fin[2 * b], fin[2 * b + 1]], axis=0)
        hcar_ref[b] = hb
        ht_ref[b] = hb

    for b in range(nb):
        accs = []
        for m in range(n_q // 2):
            acc = None
            for q in (m, m + n_q // 2):
                tiles = [s_ref[b, pl.ds(q * tiles_q + c, lc, stride=n_state_rows), :] for c in range(tiles_q)]
                part = jnp.dot(jnp.concatenate(tiles, axis=1).astype(BF16), cq_ref[q], preferred_element_type=F32)
                acc = part if acc is None else acc + part
            accs.append(acc)
        y = jnp.concatenate(accs, axis=1) + d_ref[...] * u_refs[b][...]
        z = _gelu_tanh(y)
        gate = _sigmoid(jnp.dot(z.astype(BF16), wglu_ref[...], preferred_element_type=F32) + bglu_ref[...])
        y_ref[b] = (z * gate).astype(y_ref.dtype)


def _ssm(u, row0, nb, seq, lc, h0, lam, bbig, cbig, dvec, wglu, bglu):
    d_ssm = u.shape[1]
    n_state_rows = lam.shape[0]
    n_chunks = seq // lc
    blk0 = row0 // lc
    u_specs = [pl.BlockSpec((lc, d_ssm), functools.partial(lambda i, b: (blk0 + b * n_chunks + i, 0), b=b))
               for b in range(nb)]
    return pl.pallas_call(
        functools.partial(_ssm_body, nb=nb, lc=lc, n_state_rows=n_state_rows),
        grid=(n_chunks,),
        in_specs=u_specs + [_const_spec(h0.shape), _const_spec(lam.shape), _const_spec(bbig.shape),
                            _const_spec(cbig.shape), _const_spec(dvec.shape), _const_spec(wglu.shape),
                            _const_spec(bglu.shape)],
        out_specs=[pl.BlockSpec((nb, lc, d_ssm), lambda i: (0, i, 0)), _const_spec(h0.shape)],
        out_shape=[jax.ShapeDtypeStruct((nb, seq, d_ssm), BF16), jax.ShapeDtypeStruct(h0.shape, F32)],
        scratch_shapes=[pltpu.VMEM((nb, lc * n_state_rows, LANES), F32),
                        pltpu.VMEM(h0.shape, F32)],
        compiler_params=_cparams(1),
        name=f"ssm_b{nb}",
    )(*([u] * nb), h0, lam, bbig, cbig, dvec, wglu, bglu)


def _bias_rows(m_ref, lead, base, n_rows, n_cols):
    for r in range(n_rows):
        m_ref[lead, r:r + 1, :] = base[:, QBLK - 1 - r:QBLK - 1 - r + n_cols]


def _attn_prompt_body(q_ref, k_ref, v_ref, base_ref, o_ref, kb_ref, vb_ref, m_ref):
    j = pl.program_id(2)

    @pl.when(j == 0)
    def _():
        kb_ref[:BAND_ROWS, :] = jnp.zeros((BAND_ROWS, LANES), BF16)
        vb_ref[:BAND_ROWS, :] = jnp.zeros((BAND_ROWS, LANES), BF16)
        kb_ref[BAND_ROWS:, :] = k_ref[...].astype(BF16)
        vb_ref[BAND_ROWS:, :] = v_ref[...].astype(BF16)
        rq = lax.broadcasted_iota(I32, (QBLK, KWIN), 0) // CHUNK
        ck = lax.broadcasted_iota(I32, (QBLK, KWIN), 1) // CHUNK
        in_band = (ck >= rq) & (ck <= rq + LEFT_CHUNKS)
        for a in range(2):
            _bias_rows(m_ref, a, base_ref[0, a:a + 1, :], QBLK, KWIN)
            m_ref[a] = jnp.where(in_band, m_ref[a], NEG_INF)

    start = pl.multiple_of(j * (QSTEP * QBLK), QSTEP * QBLK)
    lane = lax.broadcasted_iota(I32, (QBLK, LANES), 1)
    col = lax.broadcasted_iota(I32, (2 * QBLK, KWIN), 1)
    bias = m_ref[...].reshape(2 * QBLK, KWIN)
    for c in range(QSTEP):
        q2 = q_ref[QBLK * c:QBLK * (c + 1), :]
        kw = kb_ref[pl.ds(start + QBLK * c, KWIN), :]
        vw = vb_ref[pl.ds(start + QBLK * c, KWIN), :]
        first_valid = BAND_ROWS - (j * QSTEP + c) * QBLK
        zero = jnp.zeros_like(q2)
        qs = jnp.concatenate([jnp.where(lane < HEAD_DIM, q2, zero), jnp.where(lane >= HEAD_DIM, q2, zero)], axis=0)
        s = lax.dot_general(qs, kw, (((1,), (1,)), ((), ())), preferred_element_type=F32)
        s = jnp.where(col >= first_valid, s + bias, NEG_INF)
        p = jnp.exp(s - jnp.max(s, axis=1, keepdims=True))
        o = jnp.dot(p.astype(BF16), vw, preferred_element_type=F32) * (1.0 / jnp.sum(p, axis=1, keepdims=True))
        o_ref[QBLK * c:QBLK * (c + 1), :] = jnp.where(lane < HEAD_DIM, o[:QBLK], o[QBLK:]).astype(o_ref.dtype)


def _attn_prompt(q, k, v, base2, nb, seq):
    d_attn = q.shape[1]
    pairs = d_attn // LANES
    nq = seq // (QSTEP * QBLK)
    kv_spec = pl.BlockSpec((seq, LANES), lambda b, p, j: (b, p))
    return pl.pallas_call(
        _attn_prompt_body,
        grid=(nb, pairs, nq),
        in_specs=[pl.BlockSpec((QSTEP * QBLK, LANES), lambda b, p, j: (b * nq + j, p)), kv_spec, kv_spec,
                  pl.BlockSpec((1, 2, base2.shape[2]), lambda b, p, j: (p, 0, 0))],
        out_specs=pl.BlockSpec((QSTEP * QBLK, LANES), lambda b, p, j: (b * nq + j, p)),
        out_shape=jax.ShapeDtypeStruct((nb * seq, d_attn), BF16),
        scratch_shapes=[pltpu.VMEM((seq + BAND_ROWS, LANES), BF16), pltpu.VMEM((seq + BAND_ROWS, LANES), BF16),
                        pltpu.VMEM((2, QBLK, KWIN), F32)],
        compiler_params=_cparams(3),
        name="attn_prompt",
    )(q, k, v, base2)


def _attn_sample_body(q_ref, k_ref, v_ref, ck_ref, cv_ref, base_ref, o_ref, m_ref, *, heads, n_cache):
    @pl.when(pl.program_id(0) == 0)
    def _():
        for h in range(heads):
            _bias_rows(m_ref, h, base_ref[h:h + 1, :], CHUNK, n_cache + CHUNK)

    q = q_ref[...]
    k = k_ref[...].astype(BF16)
    v = v_ref[...].astype(BF16)
    nt = (((1,), (1,)), ((), ()))
    outs = []
    for h in range(heads):
        sl = slice(HEAD_DIM * h, HEAD_DIM * (h + 1))
        qh = q[:, sl]
        s1 = lax.dot_general(qh, ck_ref[0, h].astype(BF16), nt, preferred_element_type=F32) + m_ref[h, :, :n_cache]
        s2 = lax.dot_general(qh, k[:, sl], nt, preferred_element_type=F32) + m_ref[h, :, n_cache:]
        mx = jnp.maximum(jnp.max(s1, axis=1, keepdims=True), jnp.max(s2, axis=1, keepdims=True))
        p1, p2 = jnp.exp(s1 - mx), jnp.exp(s2 - mx)
        den = jnp.sum(p1, axis=1, keepdims=True) + jnp.sum(p2, axis=1, keepdims=True)
        o = (jnp.dot(p1.astype(BF16), cv_ref[0, h].astype(BF16), preferred_element_type=F32)
             + jnp.dot(p2.astype(BF16), v[:, sl], preferred_element_type=F32))
        outs.append(o * (1.0 / den))
    o_ref[...] = jnp.concatenate(outs, axis=1).astype(o_ref.dtype)


def _attn_sample(q, k, v, cache_k, cache_v, base, row0):
    nb, heads, n_cache, _ = cache_k.shape
    d_attn = q.shape[1]
    blk0 = row0 // CHUNK
    row = pl.BlockSpec((CHUNK, d_attn), lambda b: (blk0 + b, 0))
    cache = pl.BlockSpec((1, heads, n_cache, HEAD_DIM), lambda b: (b, 0, 0, 0))
    return pl.pallas_call(
        functools.partial(_attn_sample_body, heads=heads, n_cache=n_cache),
        grid=(nb,),
        in_specs=[row, row, row, cache, cache, _const_spec(base.shape)],
        out_specs=pl.BlockSpec((CHUNK, d_attn), lambda b: (b, 0)),
        out_shape=jax.ShapeDtypeStruct((nb * CHUNK, d_attn), BF16),
        scratch_shapes=[pltpu.VMEM((heads, CHUNK, n_cache + CHUNK), F32)],
        compiler_params=_cparams(1),
        name="attn_sample",
    )(q, k, v, cache_k, cache_v, base)


def _bias_base(table):
    heads = table.shape[0]
    far = jnp.broadcast_to(table[:, 2 * MAX_REL:], (heads, KWIN - 1 - MAX_REL + 1))
    near = table[:, 2 * MAX_REL - 1:0:-1]
    base = jnp.concatenate([far, near], axis=1)
    return jnp.pad(base, ((0, 0), (0, KWIN + QBLK - base.shape[1])))


def _post_body(*refs, n_x, n_y, n_pb, tm):
    i = pl.program_id(0)
    pos = 0

    def take(n):
        nonlocal pos
        out = refs[pos:pos + n]
        pos += n
        return out

    def merged(pair):
        if len(pair) == 1:
            return pair[0][...]
        return jnp.where(i < n_pb, pair[0][...], pair[1][...])

    x_refs = take(n_x)
    y_pairs = [take(2) for _ in range(n_y)]
    w_refs = take(n_y)
    g_ref, wrh_ref, wrl_ref, br_ref, tril_ref, cin_ref = take(6)
    xo_ref, xp_ref, mi_ref, mf_ref, cnt_ref = take(5)
    (carry_ref,) = take(1)

    x = merged(x_refs)
    for pair, w_ref in zip(y_pairs, w_refs):
        x = x + jnp.dot(merged(pair).astype(BF16), w_ref[...], preferred_element_type=F32)
    xo_ref[...] = x

    xn = _rms(x, g_ref[...])
    for c in range(ROWS_O):
        xp_ref[pl.ds(c, tm, stride=ROWS_O), :] = xn[:, LANES * c:LANES * (c + 1)]

    xh, xl = _split_bf16(xn)
    logits = (jnp.dot(xh, wrh_ref[...], preferred_element_type=F32)
              + jnp.dot(xl, wrh_ref[...], preferred_element_type=F32)
              + jnp.dot(xh, wrl_ref[...], preferred_element_type=F32)
              + jnp.dot(xl, wrl_ref[...], preferred_element_type=F32)) + br_ref[...]
    lane = lax.broadcasted_iota(I32, logits.shape, 1)

    def first_argmax(vals):
        top = jnp.max(vals, axis=1, keepdims=True)
        return top, jnp.min(jnp.where(vals == top, lane, LANES), axis=1, keepdims=True)

    is_group = lane < MOE_GROUPS
    g_top, g_idx = first_argmax(jnp.where(is_group, logits, -jnp.inf))
    g_w = 1.0 / jnp.sum(jnp.where(is_group, jnp.exp(logits - g_top), 0.0), axis=1, keepdims=True)
    first = MOE_GROUPS + EXPERTS_PER_GROUP * g_idx
    e_log = jnp.where((lane >= first) & (lane < first + EXPERTS_PER_GROUP), logits, -jnp.inf)
    v1, l1 = first_argmax(e_log)
    v2, l2 = first_argmax(jnp.where(lane == l1, -jnp.inf, e_log))
    t = jnp.exp(v2 - v1)
    w1 = 1.0 / (1.0 + t)
    w2 = t * w1
    e1, e2 = l1 - MOE_GROUPS, l2 - MOE_GROUPS

    @pl.when(i == 0)
    def _():
        carry_ref[...] = cin_ref[...]

    hit1, hit2 = lane == e1, lane == e2
    oh = jnp.where(hit1 | hit2, 1.0, 0.0)
    before = jnp.dot(tril_ref[...], oh.astype(BF16), preferred_element_type=F32) + carry_ref[...]
    r1 = jnp.sum(jnp.where(hit1, before, 0.0), axis=1, keepdims=True).astype(I32)
    r2 = jnp.sum(jnp.where(hit2, before, 0.0), axis=1, keepdims=True).astype(I32)
    carry_ref[...] = carry_ref[...] + jnp.sum(oh, axis=0, keepdims=True)
    cnt_ref[...] = carry_ref[...]

    mi_ref[...] = jnp.where(lane == 0, e1, jnp.where(lane == 1, e2, jnp.where(lane == 2, r1, r2)))
    mf_ref[...] = jnp.where(lane == 0, g_w * w1, g_w * w2)


def _post_mixer(x_src, y_pairs, ws, g, router, tril, t_p, t_s):
    wrh, wrl, br = router
    d = ws[0].shape[1]
    n_pb = t_p // TM
    n_tot = (t_p + t_s) // TM
    row = lambda width: pl.BlockSpec((TM, width), lambda i: (i, 0))
    in_specs, args = [], []
    if len(x_src) == 1:
        in_specs.append(row(d))
    else:
        in_specs += _pair_specs(TM, d, n_pb)
    args += list(x_src)
    for yp, ys in y_pairs:
        in_specs += _pair_specs(TM, yp.shape[1], n_pb)
        args += [yp, ys]
    consts = list(ws) + [g.reshape(1, d), wrh, wrl, br, tril, jnp.zeros((1, LANES), F32)]
    in_specs += [_const_spec(c.shape) for c in consts]
    t = t_p + t_s
    return pl.pallas_call(
        functools.partial(_post_body, n_x=len(x_src), n_y=len(y_pairs), n_pb=n_pb, tm=TM),
        grid=(n_tot,),
        in_specs=in_specs,
        out_specs=[row(d), pl.BlockSpec((TM * ROWS_O, LANES), lambda i: (i, 0)), row(LANES), row(LANES),
                   _const_spec((1, LANES))],
        out_shape=[jax.ShapeDtypeStruct((t, d), F32), jax.ShapeDtypeStruct((t * ROWS_O, LANES), F32),
                   jax.ShapeDtypeStruct((t, LANES), I32), jax.ShapeDtypeStruct((t, LANES), F32),
                   jax.ShapeDtypeStruct((1, LANES), F32)],
        scratch_shapes=[pltpu.VMEM((1, LANES), F32)],
        compiler_params=_cparams(1),
        name="post_mixer",
    )(*args, *consts)


def _router_weights(gate_w, gate_b, router_w, router_b):
    d = gate_w.shape[0]
    w = jnp.concatenate([gate_w, router_w.transpose(1, 0, 2).reshape(d, N_EXPERTS)], axis=1)
    b = jnp.concatenate([gate_b, router_b.reshape(N_EXPERTS)])
    w = jnp.pad(w, ((0, 0), (0, LANES - w.shape[1])))
    b = jnp.pad(b, (0, LANES - b.shape[0])).reshape(1, LANES)
    hi, lo = _split_bf16(w)
    return hi, lo, b


def _dispatch_body(slots_ref, pad_pos_ref, pad_n_ref, x_ref, xs_ref, zero_ref, sem, *, tm, n_steps):
    i = pl.program_id(0)

    def zero_copy(pos, n):
        return pltpu.make_async_copy(zero_ref.at[pl.ds(0, n)], xs_ref.at[pl.ds(pos, n)], sem.at[2])

    @pl.when(i == 0)
    def _():
        zero_ref[...] = jnp.zeros(zero_ref.shape, zero_ref.dtype)
        for e in range(N_EXPERTS):
            pos, n = pad_pos_ref[e], pad_n_ref[e]
            bit = tm // 2
            while bit >= 1:
                @pl.when((n & bit) != 0)
                def _(pos=pos, bit=bit):
                    cp = zero_copy(pos, bit)
                    cp.start()
                    cp.wait()
                pos = pos + (n & bit)
                bit //= 2
        used_tiles = (pad_pos_ref[N_EXPERTS - 1] + pad_n_ref[N_EXPERTS - 1]) // tm
        all_tiles = xs_ref.shape[0] // tm
        lax.fori_loop(used_tiles, all_tiles, lambda tl, c: (zero_copy(tl * tm, tm).start(), c)[1], 0)
        lax.fori_loop(used_tiles, all_tiles, lambda tl, c: (zero_copy(tl * tm, tm).wait(), c)[1], 0)

    par = i % 2

    def start(r, c):
        tok = i * tm + r
        for k in range(2):
            pltpu.make_async_copy(x_ref.at[tok], xs_ref.at[slots_ref[2 * tok + k]], sem.at[par]).start(priority=k)
        return c

    lax.fori_loop(0, tm, start, 0, unroll=8)

    def wait_step(p):
        pltpu.make_async_copy(xs_ref.at[pl.ds(0, 2 * tm)], xs_ref.at[pl.ds(0, 2 * tm)], sem.at[p]).wait()

    pl.when(i > 0)(lambda: wait_step(1 - par))
    pl.when(i == n_steps - 1)(lambda: wait_step(par))


def _dispatch(xrows, slots, pad_pos, pad_n, n_slots):
    t = xrows.shape[0]
    slab = xrows.shape[1:]
    grid_spec = pltpu.PrefetchScalarGridSpec(
        num_scalar_prefetch=3,
        grid=(t // TM_MOE,),
        in_specs=[pl.BlockSpec(memory_space=pl.ANY)],
        out_specs=pl.BlockSpec(memory_space=pl.ANY),
        scratch_shapes=[pltpu.VMEM((TM_MOE,) + slab, xrows.dtype), pltpu.SemaphoreType.DMA((3,))],
    )
    return pl.pallas_call(
        functools.partial(_dispatch_body, tm=TM_MOE, n_steps=t // TM_MOE),
        grid_spec=grid_spec,
        out_shape=jax.ShapeDtypeStruct((n_slots,) + slab, xrows.dtype),
        compiler_params=_cparams(1),
        name="moe_dispatch",
    )(slots, pad_pos, pad_n, xrows)


def _moe_body(te_ref, nt_ref, x_ref, wgu_ref, wd_ref, o_ref, wgu_bf, wd_bf, *, tm):
    i = pl.program_id(0)
    prev = te_ref[jnp.maximum(i - 1, 0)]

    @pl.when((i == 0) | (te_ref[i] != prev))
    def _():
        wgu_bf[...] = wgu_ref[0, 0].astype(BF16)
        wd_bf[...] = wd_ref[0, 0].astype(BF16)

    @pl.when(i < nt_ref[0])
    def _():
        x = jnp.concatenate([x_ref[pl.ds(c, tm, stride=ROWS_O), :] for c in range(ROWS_O)], axis=1)
        h = jnp.dot(x.astype(BF16), wgu_bf[...], preferred_element_type=F32)
        d_exp = h.shape[1] // 2
        gt, up = h[:, :d_exp], h[:, d_exp:]
        act = (gt * _sigmoid(gt) * up).astype(BF16)
        out = jnp.dot(act, wd_bf[...], preferred_element_type=F32)
        for c in range(ROWS_O):
            o_ref[pl.ds(c, tm, stride=ROWS_O), :] = out[:, LANES * c:LANES * (c + 1)]

    @pl.when(i >= nt_ref[0])
    def _():
        o_ref[...] = jnp.zeros(o_ref.shape, o_ref.dtype)


def _moe_experts(xsorted, tile_expert, n_tiles, wgu, wd, layer):
    n_slots = xsorted.shape[0] // ROWS_O
    n_max = n_slots // TM_MOE
    grid_spec = pltpu.PrefetchScalarGridSpec(
        num_scalar_prefetch=2,
        grid=(n_max,),
        in_specs=[pl.BlockSpec((TM_MOE * ROWS_O, LANES), lambda i, te, nt: (jnp.minimum(i, nt[0] - 1), 0)),
                  pl.BlockSpec((1, 1) + wgu.shape[2:], lambda i, te, nt: (layer, te[i], 0, 0)),
                  pl.BlockSpec((1, 1) + wd.shape[2:], lambda i, te, nt: (layer, te[i], 0, 0))],
        out_specs=pl.BlockSpec((TM_MOE * ROWS_O, LANES), lambda i, te, nt: (i, 0)),
        scratch_shapes=[pltpu.VMEM(wgu.shape[2:], BF16), pltpu.VMEM(wd.shape[2:], BF16)],
    )
    return pl.pallas_call(
        functools.partial(_moe_body, tm=TM_MOE),
        grid_spec=grid_spec,
        out_shape=jax.ShapeDtypeStruct((n_slots * ROWS_O, LANES), F32),
        compiler_params=_cparams(1),
        name="moe_experts",
    )(tile_expert, n_tiles, xsorted, wgu, wd)


def _combine_body(slots_ref, x_ref, mf_ref, os_ref, *rest, tm, n_steps, n_pb):
    out_refs, (buf_ref, sem) = rest[:-2], rest[-2:]
    i = pl.program_id(0)

    def issue(tile, par):
        def body(r, c):
            for k in range(2):
                slot = slots_ref[2 * (tile * tm + r) + k]
                dst = buf_ref.at[par, k, pl.ds(pl.multiple_of(r * ROWS_O, ROWS_O), ROWS_O)]
                pltpu.make_async_copy(os_ref.at[slot], dst, sem.at[par]).start(priority=k)
            return c
        lax.fori_loop(0, tm, body, 0, unroll=8)

    @pl.when(i == 0)
    def _():
        issue(0, 0)

    @pl.when(i + 1 < n_steps)
    def _():
        issue(i + 1, (i + 1) % 2)

    par = i % 2
    pltpu.make_async_copy(buf_ref.at[par], buf_ref.at[par], sem.at[par]).wait()

    def write(out_ref):
        mf = mf_ref[...]
        g1, g2 = mf[:, 0:1], mf[:, 1:2]
        for c in range(ROWS_O):
            cols = slice(LANES * c, LANES * (c + 1))
            piece = lambda k: buf_ref[par, k, pl.ds(c, tm, stride=ROWS_O), :]
            out_ref[:, cols] = x_ref[:, cols] + g1 * piece(0) + g2 * piece(1)

    if len(out_refs) == 1:
        write(out_refs[0])
    else:
        pl.when(i < n_pb)(lambda: write(out_refs[0]))
        pl.when(i >= n_pb)(lambda: write(out_refs[1]))


def _combine(x, meta_f, slots, osorted, t_p, split):
    t, d = x.shape
    n_steps = t // TM_MOE
    n_pb = t_p // TM_MOE
    row = lambda width: pl.BlockSpec((TM_MOE, width), lambda i, *_: (i, 0))
    if split:
        out_specs = [pl.BlockSpec((TM_MOE, d), lambda i, *_: (jnp.minimum(i, n_pb - 1), 0)),
                     pl.BlockSpec((TM_MOE, d), lambda i, *_: (jnp.maximum(i - n_pb, 0), 0))]
        out_shape = [jax.ShapeDtypeStruct((t_p, d), F32), jax.ShapeDtypeStruct((t - t_p, d), F32)]
    else:
        out_specs = [row(d)]
        out_shape = [jax.ShapeDtypeStruct((t, d), F32)]
    grid_spec = pltpu.PrefetchScalarGridSpec(
        num_scalar_prefetch=1,
        grid=(n_steps,),
        in_specs=[row(d), row(LANES), pl.BlockSpec(memory_space=pl.ANY)],
        out_specs=out_specs,
        scratch_shapes=[pltpu.VMEM((2, 2, TM_MOE * ROWS_O, LANES), F32), pltpu.SemaphoreType.DMA((2,))],
    )
    return pl.pallas_call(
        functools.partial(_combine_body, tm=TM_MOE, n_steps=n_steps, n_pb=n_pb),
        grid_spec=grid_spec,
        out_shape=out_shape,
        compiler_params=_cparams(1),
        name="moe_combine",
    )(slots, x, meta_f, osorted)


def _moe_layer(x, xrows, meta_i, meta_f, counts, wgu, wd, layer, t_p, split):
    t = x.shape[0]
    n_max = (2 * t + N_EXPERTS * (TM_MOE - 1)) // TM_MOE
    cnt = counts[0, :N_EXPERTS].astype(I32)
    padded = (cnt + TM_MOE - 1) // TM_MOE * TM_MOE
    ends = jnp.cumsum(padded)
    offs = ends - padded
    experts = jnp.arange(N_EXPERTS, dtype=I32)
    first_slot = jnp.sum(jnp.where(meta_i[:, 0:2, None] == experts, offs, 0), axis=-1)
    slots = (first_slot + meta_i[:, 2:4]).reshape(-1)
    n_tiles = (ends[-1] // TM_MOE).reshape(1)
    tile_start = jnp.arange(n_max, dtype=I32) * TM_MOE
    tile_expert = jnp.minimum(jnp.sum((ends[None, :] <= tile_start[:, None]).astype(I32), axis=1), N_EXPERTS - 1)
    n_slots = n_max * TM_MOE
    xsorted = _dispatch(xrows.reshape(t, ROWS_O, LANES), slots, offs + cnt, padded - cnt, n_slots)
    osorted = _moe_experts(xsorted.reshape(n_slots * ROWS_O, LANES), tile_expert, n_tiles, wgu, wd, layer)
    return _combine(x, meta_f, slots, osorted.reshape(n_slots, ROWS_O, LANES), t_p, split)


def _conv_in_body(x_ref, g_ref, w_ref, u_ref):
    xn = _rms(x_ref[...], g_ref[...]).astype(BF16)
    h = jnp.dot(xn, w_ref[...], preferred_element_type=F32)
    c = h.shape[1] // 2
    u_ref[...] = h[:, :c] * _sigmoid(h[:, c:])


def _conv_in(x, g, w):
    t, d = x.shape
    c = w.shape[1] // 2
    return pl.pallas_call(
        _conv_in_body,
        grid=(t // TM,),
        in_specs=[pl.BlockSpec((TM, d), lambda i: (i, 0)), _const_spec((1, d)), _const_spec(w.shape)],
        out_specs=pl.BlockSpec((TM, c), lambda i: (i, 0)),
        out_shape=jax.ShapeDtypeStruct((t, c), F32),
        compiler_params=_cparams(1),
        name="conv_in",
    )(x, g.reshape(1, d), w.astype(BF16))


def _dwconv_body(cur_ref, prev_ref, left_ref, w_ref, b_ref, g_ref, beta_ref, o_ref, win_ref, y_ref, *, tm):
    i = pl.program_id(1)
    n_lt = cur_ref.shape[1] // LANES
    halo = jnp.where(i == 0, left_ref[0], prev_ref[...])
    cur = cur_ref[...]
    for c in range(n_lt):
        cols = slice(LANES * c, LANES * (c + 1))
        win_ref[pl.ds(c, CONV_HALO, stride=n_lt), :] = halo[:, cols]
        win_ref[pl.ds(CONV_HALO * n_lt + c, tm, stride=n_lt), :] = cur[:, cols]
    first_tap = CONV_HALO - (CONV_WIDTH - 1)
    taps = [w_ref[j] for j in range(CONV_WIDTH)]

    def block(rb, carry):
        for t in range(SUBLANES):
            frame = rb * SUBLANES + t
            acc = None
            for j in range(CONV_WIDTH):
                row = pl.multiple_of((frame + first_tap + j) * n_lt, n_lt)
                term = taps[j] * win_ref[pl.ds(row, n_lt), :]
                acc = term if acc is None else acc + term
            y_ref[pl.ds(pl.multiple_of(frame * n_lt, n_lt), n_lt), :] = acc
        return carry

    lax.fori_loop(0, tm // SUBLANES, block, 0)
    y = jnp.concatenate([y_ref[pl.ds(c, tm, stride=n_lt), :] for c in range(n_lt)], axis=1)
    y = y + b_ref[...]
    mu = jnp.mean(y, axis=1, keepdims=True)
    yc = y - mu
    yn = yc * lax.rsqrt(jnp.mean(yc * yc, axis=1, keepdims=True) + EPS) * g_ref[...] + beta_ref[...]
    o_ref[...] = (yn * _sigmoid(yn)).astype(BF16)


def _dwconv(u, left, dw_w, dw_b, ln_g, ln_b, row0, nb, seq, tm):
    c = u.shape[1]
    n_t = seq // tm
    blk0 = row0 // tm
    halo_per_tile = tm // CONV_HALO
    vec = lambda a: a.reshape(1, c)
    w = dw_w.reshape(CONV_WIDTH, c // LANES, LANES)
    return pl.pallas_call(
        functools.partial(_dwconv_body, tm=tm),
        grid=(nb, n_t),
        in_specs=[pl.BlockSpec((tm, c), lambda b, i: (blk0 + b * n_t + i, 0)),
                  pl.BlockSpec((CONV_HALO, c),
                               lambda b, i: (jnp.maximum((blk0 + b * n_t + i) * halo_per_tile - 1, 0), 0)),
                  pl.BlockSpec((1, CONV_HALO, c), lambda b, i: (b, 0, 0)),
                  _const_spec(w.shape), _const_spec((1, c)), _const_spec((1, c)), _const_spec((1, c))],
        out_specs=pl.BlockSpec((tm, c), lambda b, i: (b * n_t + i, 0)),
        out_shape=jax.ShapeDtypeStruct((nb * seq, c), BF16),
        scratch_shapes=[pltpu.VMEM(((tm + CONV_HALO) * (c // LANES), LANES), F32),
                        pltpu.VMEM((tm * (c // LANES), LANES), F32)],
        compiler_params=_cparams(2),
        name=f"dwconv_b{nb}",
    )(u, u, left, w, vec(dw_b), vec(ln_g), vec(ln_b))


def _block_diag(blocks):
    g, a, b = blocks.shape
    return jnp.einsum("gab,gh->gahb", blocks, jnp.eye(g, dtype=blocks.dtype)).reshape(g * a, g * b)


def _state_rows(re, im):
    nb = re.shape[0]
    return jnp.concatenate([re.reshape(nb, -1, LANES), im.reshape(nb, -1, LANES)], axis=1)


def kernel(x_prompt, x_sample, cache_attn_k, cache_attn_v, state_ssm_re, state_ssm_im, cache_conv, norm_mixer, norm_ffn, w_in_even, w_out_even, ssm_a_re, ssm_a_im, ssm_log_dt, ssm_b_re, ssm_b_im, ssm_c_re, ssm_c_im, ssm_d, ssm_glu_w, ssm_glu_b, attn_q_gain, attn_k_gain, attn_rel_bias, conv_w_in, conv_dw_w, conv_dw_b, conv_ln_g, conv_ln_b, conv_w_out, moe_gate_w, moe_gate_b, moe_router_w, moe_router_b, moe_w_gate_up, moe_w_down):
    nb_p, seq_p, d = x_prompt.shape
    nb_s, seq_s, _ = x_sample.shape
    t_p, t_s = nb_p * seq_p, nb_s * seq_s
    xp, xs = x_prompt.reshape(t_p, d), x_sample.reshape(t_s, d)
    depth = norm_mixer.shape[0]
    tril = jnp.tril(jnp.ones((TM, TM), F32), -1).astype(BF16)

    ks_p, vs_p, hs_p, ks_s, vs_s, hs_s, conv_p, conv_s = [], [], [], [], [], [], [], []
    x_src = (xp, xs)
    for layer in range(depth):
        j = layer // 2
        if layer % 2 == 0:
            u, q, k, v = _inproj(*x_src, norm_mixer[layer], w_in_even[j], attn_q_gain[j], attn_k_gain[j])
            d_ssm = u.shape[1]
            groups, n_state = ssm_a_re[j].shape
            bar_re, bar_im, bb_re, bb_im = _ssm_discretize(ssm_a_re[j], ssm_a_im[j], ssm_log_dt[j],
                                                           ssm_b_re[j], ssm_b_im[j])
            lam = jnp.concatenate([bar_re.reshape(-1, LANES), bar_im.reshape(-1, LANES)], axis=0)
            to_state = lambda m: _block_diag(m.reshape(groups, n_state, SSM_GROUP).transpose(0, 2, 1))
            bbig = jnp.concatenate([to_state(bb_re), to_state(bb_im)], axis=1).astype(BF16)
            cbig = jnp.concatenate([_block_diag(ssm_c_re[j].transpose(0, 2, 1)),
                                    _block_diag(-ssm_c_im[j].transpose(0, 2, 1))], axis=0).astype(BF16)
            n_q, k_q = 4, d_ssm // 2
            w_q = bbig.shape[1] // n_q
            blocks = lambda q: (slice(k_q * (q % 2), k_q * (q % 2 + 1)), slice(w_q * q, w_q * (q + 1)))
            bbig = jnp.stack([bbig[blocks(q)[0], blocks(q)[1]] for q in range(n_q)])
            cbig = jnp.stack([cbig[blocks(q)[1], blocks(q)[0]] for q in range(n_q)])
            ssm_args = (lam, bbig, cbig, ssm_d[j].reshape(1, d_ssm), _block_diag(ssm_glu_w[j]).astype(BF16),
                        ssm_glu_b[j].reshape(1, d_ssm))
            zero_state = jnp.zeros((nb_p, groups, n_state), F32)
            y_ssm_p, h_p = _ssm(u, 0, nb_p, seq_p, 256, _state_rows(zero_state, zero_state), *ssm_args)
            y_ssm_s, h_s = _ssm(u, t_p, nb_s, seq_s, seq_s, _state_rows(state_ssm_re[j], state_ssm_im[j]), *ssm_args)
            base = _bias_base(attn_rel_bias[j])
            heads = base.shape[0]
            y_att_p = _attn_prompt(q, k, v, base.reshape(heads // 2, 2, -1), nb_p, seq_p)
            y_att_s = _attn_sample(q, k, v, cache_attn_k[j], cache_attn_v[j], base, t_p)
            w_out = w_out_even[j].astype(BF16)
            y_pairs = [(y_ssm_p.reshape(t_p, d_ssm), y_ssm_s.reshape(t_s, d_ssm)), (y_att_p, y_att_s)]
            ws = [w_out[:d_ssm], w_out[d_ssm:]]

            keep = min(BAND_ROWS, seq_p)
            heads_of = lambda a, nb: a.reshape(nb, -1, heads, HEAD_DIM).transpose(0, 2, 1, 3)
            tail_p = lambda a: jnp.concatenate([a[(b + 1) * seq_p - keep:(b + 1) * seq_p] for b in range(nb_p)])
            ks_p.append(heads_of(tail_p(k), nb_p))
            vs_p.append(heads_of(tail_p(v), nb_p))
            ks_s.append(heads_of(k[t_p:], nb_s))
            vs_s.append(heads_of(v[t_p:], nb_s))
            half = h_p.shape[1] // 2
            hs_p.append((h_p[:, :half].reshape(nb_p, groups, n_state), h_p[:, half:].reshape(nb_p, groups, n_state)))
            hs_s.append((h_s[:, :half].reshape(nb_s, groups, n_state), h_s[:, half:].reshape(nb_s, groups, n_state)))
        else:
            (x_merged,) = x_src
            uc = _conv_in(x_merged, norm_mixer[layer], conv_w_in[j])
            c = uc.shape[1]
            conv_args = (conv_dw_w[j], conv_dw_b[j], conv_ln_g[j], conv_ln_b[j])
            left_p = jnp.zeros((nb_p, CONV_HALO, c), F32)
            left_s = jnp.pad(cache_conv[j], ((0, 0), (CONV_HALO - (CONV_WIDTH - 1), 0), (0, 0)))
            act_p = _dwconv(uc, left_p, *conv_args, 0, nb_p, seq_p, TM_CONV)
            act_s = _dwconv(uc, left_s, *conv_args, t_p, nb_s, seq_s, seq_s)
            y_pairs = [(act_p, act_s)]
            ws = [conv_w_out[j].astype(BF16)]
            keep = CONV_WIDTH - 1
            conv_p.append(jnp.stack([uc[(b + 1) * seq_p - keep:(b + 1) * seq_p] for b in range(nb_p)]))
            ext_s = jnp.concatenate([cache_conv[j], uc[t_p:].reshape(nb_s, seq_s, c)], axis=1)
            conv_s.append(ext_s[:, ext_s.shape[1] - keep:])

        router = _router_weights(moe_gate_w[layer], moe_gate_b[layer], moe_router_w[layer], moe_router_b[layer])
        x_new, xrows, meta_i, meta_f, counts = _post_mixer(x_src, y_pairs, ws, norm_ffn[layer], router, tril, t_p, t_s)
        last = layer == depth - 1
        out = _moe_layer(x_new, xrows, meta_i, meta_f, counts, moe_w_gate_up, moe_w_down, layer, t_p, last)
        x_src = tuple(out)

    y_prompt, y_sample = x_src
    stack_c = lambda parts, idx: jnp.stack([p[idx] for p in parts])
    return (y_prompt.reshape(nb_p, seq_p, d), y_sample.reshape(nb_s, seq_s, d),
            jnp.stack(ks_p), jnp.stack(vs_p), stack_c(hs_p, 0), stack_c(hs_p, 1), jnp.stack(conv_p),
            jnp.stack(ks_s), jnp.stack(vs_s), stack_c(hs_s, 0), stack_c(hs_s, 1), jnp.stack(conv_s))
```

```python
import functools
import math

import jax
import jax.numpy as jnp
from jax import lax
from jax.experimental import pallas as pl
from jax.experimental.pallas import tpu as pltpu

F32, BF16, I32, U32 = jnp.float32, jnp.bfloat16, jnp.int32, jnp.uint32

CHUNK = 64
SSM_GROUP = 16
SSM_STATE = 64
HEAD_DIM = 64
LEFT_CHUNKS = 8
BAND_ROWS = LEFT_CHUNKS * CHUNK
MAX_REL = 128
CONV_WIDTH = 31
MOE_GROUPS = 4
EXPERTS_PER_GROUP = 8
N_EXPERTS = MOE_GROUPS * EXPERTS_PER_GROUP
EPS = 1e-6
NEG_INF = -1e30

LANES = 128
SUBLANES = 8
VMEM_LIMIT_BYTES = 56 * 1024 * 1024

TM = 512
TM_MOE = 256
QBLK = 2 * CHUNK
KWIN = QBLK + BAND_ROWS
QSTEP = 4
D_MODEL = 1024
ROWS_O = D_MODEL // LANES
CONV_HALO = 32
TM_CONV = 256


def _cparams(n_grid):
    return pltpu.CompilerParams(dimension_semantics=("arbitrary",) * n_grid, vmem_limit_bytes=VMEM_LIMIT_BYTES)


def _sigmoid(x):
    return 1.0 / (1.0 + jnp.exp(-x))


def _rms(x, g):
    return x * lax.rsqrt(jnp.mean(x * x, axis=-1, keepdims=True) + EPS) * g


def _split_bf16(x):
    hi = lax.bitcast_convert_type(lax.bitcast_convert_type(x, U32) & jnp.uint32(0xFFFF0000), F32)
    return hi.astype(BF16), (x - hi).astype(BF16)


def _pair_specs(tm, width, n_prompt_blocks):
    return [pl.BlockSpec((tm, width), lambda i: (jnp.minimum(i, n_prompt_blocks - 1), 0)),
            pl.BlockSpec((tm, width), lambda i: (jnp.maximum(i - n_prompt_blocks, 0), 0))]


def _const_spec(shape):
    nd = len(shape)
    return pl.BlockSpec(shape, lambda *a: (0,) * nd)


def _inproj_body(xp_ref, xs_ref, g_ref, w_ref, qg_ref, kg_ref, seg_ref, u_ref, q_ref, k_ref, v_ref, *, n_pb, d_half):
    i = pl.program_id(0)
    x = jnp.where(i < n_pb, xp_ref[...], xs_ref[...])
    xn = _rms(x, g_ref[...]).astype(BF16)
    proj = jnp.dot(xn, w_ref[...], preferred_element_type=F32)
    u_ref[...] = proj[:, :d_half]
    v_ref[...] = proj[:, 3 * d_half:]

    def head_norm(t, gain):
        hi, lo = _split_bf16(t * t)
        msq = (jnp.dot(hi, seg_ref[...], preferred_element_type=F32)
               + jnp.dot(lo, seg_ref[...], preferred_element_type=F32))
        return t * lax.rsqrt(msq + EPS) * gain

    q_ref[...] = (head_norm(proj[:, d_half:2 * d_half], qg_ref[...]) * (HEAD_DIM ** -0.5)).astype(BF16)
    k_ref[...] = head_norm(proj[:, 2 * d_half:3 * d_half], kg_ref[...])


def _inproj(xp, xs, g, w, qg, kg):
    t_p, d = xp.shape
    t_s = xs.shape[0]
    d_half = w.shape[1] // 4
    n_pb = t_p // TM
    n_tot = (t_p + t_s) // TM
    heads = d_half // HEAD_DIM
    seg = jnp.kron(jnp.eye(heads, dtype=F32), jnp.full((HEAD_DIM, HEAD_DIM), 1.0 / HEAD_DIM, F32)).astype(BF16)
    out_f = jax.ShapeDtypeStruct((t_p + t_s, d_half), F32)
    out_b = jax.ShapeDtypeStruct((t_p + t_s, d_half), BF16)
    row = pl.BlockSpec((TM, d_half), lambda i: (i, 0))
    return pl.pallas_call(
        functools.partial(_inproj_body, n_pb=n_pb, d_half=d_half),
        grid=(n_tot,),
        in_specs=_pair_specs(TM, d, n_pb) + [_const_spec((1, d)), _const_spec(w.shape), _const_spec((1, d_half)),
                                             _const_spec((1, d_half)), _const_spec(seg.shape)],
        out_specs=[row, row, row, row],
        out_shape=[out_f, out_b, out_f, out_f],
        compiler_params=_cparams(1),
        name="inproj_even",
    )(xp, xs, g.reshape(1, d), w.astype(BF16), jnp.tile(qg, heads).reshape(1, d_half),
      jnp.tile(kg, heads).reshape(1, d_half), seg)


def _ssm_disc_body(are_ref, aim_ref, ldt_ref, bre_ref, bim_ref, barre_ref, barim_ref, bbre_ref, bbim_ref):
    lam_re, lam_im = are_ref[...], aim_ref[...]
    dt = jnp.exp(ldt_ref[...])
    mag = jnp.exp(lam_re * dt)
    bar_re, bar_im = mag * jnp.cos(lam_im * dt), mag * jnp.sin(lam_im * dt)
    den = lam_re * lam_re + lam_im * lam_im
    num_re = bar_re - 1.0
    coef_re = (num_re * lam_re + bar_im * lam_im) / den
    coef_im = (bar_im * lam_re - num_re * lam_im) / den
    br, bi = bre_ref[...], bim_ref[...]
    barre_ref[...] = bar_re
    barim_ref[...] = bar_im
    bbre_ref[...] = coef_re * br - coef_im * bi
    bbim_ref[...] = coef_re * bi + coef_im * br


def _ssm_discretize(a_re, a_im, log_dt, b_re, b_im):
    g, p = a_re.shape
    n = g * p
    col = jax.ShapeDtypeStruct((n, 1), F32)
    mat = jax.ShapeDtypeStruct((n, SSM_GROUP), F32)
    return pl.pallas_call(_ssm_disc_body, out_shape=[col, col, mat, mat], name="ssm_discretize")(
        a_re.reshape(n, 1), a_im.reshape(n, 1), jnp.repeat(log_dt, p).reshape(n, 1),
        b_re.reshape(n, SSM_GROUP), b_im.reshape(n, SSM_GROUP))


def _gelu_tanh(x):
    return 0.5 * x * (1.0 + jnp.tanh(math.sqrt(2.0 / math.pi) * (x + 0.044715 * (x * x * x))))


def _ssm_body(*refs, nb, lc, n_state_rows):
    u_refs = refs[:nb]
    h0_ref, lam_ref, bq_ref, cq_ref, d_ref, wglu_ref, bglu_ref, y_ref, ht_ref, s_ref, hcar_ref = refs[nb:]
    i = pl.program_id(0)
    half = n_state_rows // 2
    n_q = bq_ref.shape[0]
    tiles_q = n_state_rows // n_q
    k_q = bq_ref.shape[1]

    @pl.when(i == 0)
    def _():
        hcar_ref[...] = h0_ref[...]

    def tile_rows(tile):
        return slice(SUBLANES * tile, SUBLANES * (tile + 1))

    for b in range(nb):
        ub = u_refs[b][...].astype(BF16)
        for q in range(n_q):
            m = q % (n_q // 2)
            res = jnp.dot(ub[:, k_q * m:k_q * (m + 1)], bq_ref[q], preferred_element_type=F32)
            for c in range(tiles_q):
                s_ref[b, :, tile_rows(q * tiles_q + c), :] = (
                    res[:, LANES * c:LANES * (c + 1)].reshape(lc // SUBLANES, SUBLANES, LANES))

    lam_re, lam_im = lam_ref[:half, :], lam_ref[half:, :]

    def block(r, hs):
        hs = list(hs)
        for t in range(SUBLANES):
            for b in range(nb):
                bu = s_ref[b, r, pl.ds(t, n_state_rows, stride=SUBLANES), :]
                hr, hi = hs[2 * b], hs[2 * b + 1]
                nr = lam_re * hr - lam_im * hi + bu[:half]
                ni = lam_re * hi + lam_im * hr + bu[half:]
                s_ref[b, r, pl.ds(t, n_state_rows, stride=SUBLANES), :] = jnp.concatenate([nr, ni], axis=0)
                hs[2 * b], hs[2 * b + 1] = nr, ni
        return tuple(hs)

    init = []
    for b in range(nb):
        init += [hcar_ref[b, :half, :], hcar_ref[b, half:, :]]
    fin = lax.fori_loop(0, lc // SUBLANES, block, tuple(init))
    for b in range(nb):
        hb = jnp.concatenate([fin[2 * b], fin[2 * b + 1]], axis=0)
        hcar_ref[b] = hb
        ht_ref[b] = hb

    for b in range(nb):
        accs = []
        for m in range(n_q // 2):
            acc = None
            for q in (m, m + n_q // 2):
                tiles = [s_ref[b, :, tile_rows(q * tiles_q + c), :].reshape(lc, LANES) for c in range(tiles_q)]
                part = jnp.dot(jnp.concatenate(tiles, axis=1).astype(BF16), cq_ref[q], preferred_element_type=F32)
                acc = part if acc is None else acc + part
            accs.append(acc)
        y = jnp.concatenate(accs, axis=1) + d_ref[...] * u_refs[b][...]
        z = _gelu_tanh(y)
        gate = _sigmoid(jnp.dot(z.astype(BF16), wglu_ref[...], preferred_element_type=F32) + bglu_ref[...])
        y_ref[b] = (z * gate).astype(y_ref.dtype)


def _ssm(u, row0, nb, seq, lc, h0, lam, bbig, cbig, dvec, wglu, bglu):
    d_ssm = u.shape[1]
    n_state_rows = lam.shape[0]
    n_chunks = seq // lc
    blk0 = row0 // lc
    u_specs = [pl.BlockSpec((lc, d_ssm), functools.partial(lambda i, b: (blk0 + b * n_chunks + i, 0), b=b))
               for b in range(nb)]
    return pl.pallas_call(
        functools.partial(_ssm_body, nb=nb, lc=lc, n_state_rows=n_state_rows),
        grid=(n_chunks,),
        in_specs=u_specs + [_const_spec(h0.shape), _const_spec(lam.shape), _const_spec(bbig.shape),
                            _const_spec(cbig.shape), _const_spec(dvec.shape), _const_spec(wglu.shape),
                            _const_spec(bglu.shape)],
        out_specs=[pl.BlockSpec((nb, lc, d_ssm), lambda i: (0, i, 0)), _const_spec(h0.shape)],
        out_shape=[jax.ShapeDtypeStruct((nb, seq, d_ssm), BF16), jax.ShapeDtypeStruct(h0.shape, F32)],
        scratch_shapes=[pltpu.VMEM((nb, lc // SUBLANES, SUBLANES * n_state_rows, LANES), F32),
                        pltpu.VMEM(h0.shape, F32)],
        compiler_params=_cparams(1),
        name=f"ssm_b{nb}",
    )(*([u] * nb), h0, lam, bbig, cbig, dvec, wglu, bglu)


def _bias_rows(m_ref, lead, base, n_rows, n_cols):
    for r in range(n_rows):
        m_ref[lead, r:r + 1, :] = base[:, QBLK - 1 - r:QBLK - 1 - r + n_cols]


def _attn_prompt_body(q_ref, k_ref, v_ref, base_ref, o_ref, kb_ref, vb_ref, m_ref):
    j = pl.program_id(2)

    @pl.when(j == 0)
    def _():
        kb_ref[:BAND_ROWS, :] = jnp.zeros((BAND_ROWS, LANES), BF16)
        vb_ref[:BAND_ROWS, :] = jnp.zeros((BAND_ROWS, LANES), BF16)
        kb_ref[BAND_ROWS:, :] = k_ref[...].astype(BF16)
        vb_ref[BAND_ROWS:, :] = v_ref[...].astype(BF16)
        rq = lax.broadcasted_iota(I32, (QBLK, KWIN), 0) // CHUNK
        ck = lax.broadcasted_iota(I32, (QBLK, KWIN), 1) // CHUNK
        in_band = (ck >= rq) & (ck <= rq + LEFT_CHUNKS)
        for a in range(2):
            _bias_rows(m_ref, a, base_ref[0, a:a + 1, :], QBLK, KWIN)
            m_ref[a] = jnp.where(in_band, m_ref[a], NEG_INF)

    start = pl.multiple_of(j * (QSTEP * QBLK), QSTEP * QBLK)
    lane = lax.broadcasted_iota(I32, (QBLK, LANES), 1)
    col = lax.broadcasted_iota(I32, (2 * QBLK, KWIN), 1)
    bias = m_ref[...].reshape(2 * QBLK, KWIN)
    for c in range(QSTEP):
        q2 = q_ref[QBLK * c:QBLK * (c + 1), :]
        kw = kb_ref[pl.ds(start + QBLK * c, KWIN), :]
        vw = vb_ref[pl.ds(start + QBLK * c, KWIN), :]
        first_valid = BAND_ROWS - (j * QSTEP + c) * QBLK
        zero = jnp.zeros_like(q2)
        qs = jnp.concatenate([jnp.where(lane < HEAD_DIM, q2, zero), jnp.where(lane >= HEAD_DIM, q2, zero)], axis=0)
        s = lax.dot_general(qs, kw, (((1,), (1,)), ((), ())), preferred_element_type=F32)
        s = jnp.where(col >= first_valid, s + bias, NEG_INF)
        p = jnp.exp(s - jnp.max(s, axis=1, keepdims=True))
        o = jnp.dot(p.astype(BF16), vw, preferred_element_type=F32) * (1.0 / jnp.sum(p, axis=1, keepdims=True))
        o_ref[QBLK * c:QBLK * (c + 1), :] = jnp.where(lane < HEAD_DIM, o[:QBLK], o[QBLK:]).astype(o_ref.dtype)


def _attn_prompt(q, k, v, base2, nb, seq):
    d_attn = q.shape[1]
    pairs = d_attn // LANES
    nq = seq // (QSTEP * QBLK)
    kv_spec = pl.BlockSpec((seq, LANES), lambda b, p, j: (b, p))
    return pl.pallas_call(
        _attn_prompt_body,
        grid=(nb, pairs, nq),
        in_specs=[pl.BlockSpec((QSTEP * QBLK, LANES), lambda b, p, j: (b * nq + j, p)), kv_spec, kv_spec,
                  pl.BlockSpec((1, 2, base2.shape[2]), lambda b, p, j: (p, 0, 0))],
        out_specs=pl.BlockSpec((QSTEP * QBLK, LANES), lambda b, p, j: (b * nq + j, p)),
        out_shape=jax.ShapeDtypeStruct((nb * seq, d_attn), BF16),
        scratch_shapes=[pltpu.VMEM((seq + BAND_ROWS, LANES), BF16), pltpu.VMEM((seq + BAND_ROWS, LANES), BF16),
                        pltpu.VMEM((2, QBLK, KWIN), F32)],
        compiler_params=_cparams(3),
        name="attn_prompt",
    )(q, k, v, base2)


def _attn_sample_body(q_ref, k_ref, v_ref, ck_ref, cv_ref, base_ref, o_ref, m_ref, *, heads, n_cache):
    @pl.when(pl.program_id(0) == 0)
    def _():
        for h in range(heads):
            _bias_rows(m_ref, h, base_ref[h:h + 1, :], CHUNK, n_cache + CHUNK)

    q = q_ref[...]
    k = k_ref[...].astype(BF16)
    v = v_ref[...].astype(BF16)
    nt = (((1,), (1,)), ((), ()))
    outs = []
    for h in range(heads):
        sl = slice(HEAD_DIM * h, HEAD_DIM * (h + 1))
        qh = q[:, sl]
        s1 = lax.dot_general(qh, ck_ref[0, h].astype(BF16), nt, preferred_element_type=F32) + m_ref[h, :, :n_cache]
        s2 = lax.dot_general(qh, k[:, sl], nt, preferred_element_type=F32) + m_ref[h, :, n_cache:]
        mx = jnp.maximum(jnp.max(s1, axis=1, keepdims=True), jnp.max(s2, axis=1, keepdims=True))
        p1, p2 = jnp.exp(s1 - mx), jnp.exp(s2 - mx)
        den = jnp.sum(p1, axis=1, keepdims=True) + jnp.sum(p2, axis=1, keepdims=True)
        o = (jnp.dot(p1.astype(BF16), cv_ref[0, h].astype(BF16), preferred_element_type=F32)
             + jnp.dot(p2.astype(BF16), v[:, sl], preferred_element_type=F32))
        outs.append(o * (1.0 / den))
    o_ref[...] = jnp.concatenate(outs, axis=1).astype(o_ref.dtype)


def _attn_sample(q, k, v, cache_k, cache_v, base, row0):
    nb, heads, n_cache, _ = cache_k.shape
    d_attn = q.shape[1]
    blk0 = row0 // CHUNK
    row = pl.BlockSpec((CHUNK, d_attn), lambda b: (blk0 + b, 0))
    cache = pl.BlockSpec((1, heads, n_cache, HEAD_DIM), lambda b: (b, 0, 0, 0))
    return pl.pallas_call(
        functools.partial(_attn_sample_body, heads=heads, n_cache=n_cache),
        grid=(nb,),
        in_specs=[row, row, row, cache, cache, _const_spec(base.shape)],
        out_specs=pl.BlockSpec((CHUNK, d_attn), lambda b: (b, 0)),
        out_shape=jax.ShapeDtypeStruct((nb * CHUNK, d_attn), BF16),
        scratch_shapes=[pltpu.VMEM((heads, CHUNK, n_cache + CHUNK), F32)],
        compiler_params=_cparams(1),
        name="attn_sample",
    )(q, k, v, cache_k, cache_v, base)


def _bias_base(table):
    heads = table.shape[0]
    far = jnp.broadcast_to(table[:, 2 * MAX_REL:], (heads, KWIN - 1 - MAX_REL + 1))
    near = table[:, 2 * MAX_REL - 1:0:-1]
    base = jnp.concatenate([far, near], axis=1)
    return jnp.pad(base, ((0, 0), (0, KWIN + QBLK - base.shape[1])))


def _post_body(*refs, n_x, n_y, n_pb, tm):
    i = pl.program_id(0)
    pos = 0

    def take(n):
        nonlocal pos
        out = refs[pos:pos + n]
        pos += n
        return out

    def merged(pair):
        if len(pair) == 1:
            return pair[0][...]
        return jnp.where(i < n_pb, pair[0][...], pair[1][...])

    x_refs = take(n_x)
    y_pairs = [take(2) for _ in range(n_y)]
    w_refs = take(n_y)
    g_ref, wrh_ref, wrl_ref, br_ref, tril_ref, cin_ref = take(6)
    xo_ref, xp_ref, mi_ref, mf_ref, cnt_ref = take(5)
    (carry_ref,) = take(1)

    x = merged(x_refs)
    for pair, w_ref in zip(y_pairs, w_refs):
        x = x + jnp.dot(merged(pair).astype(BF16), w_ref[...], preferred_element_type=F32)
    xo_ref[...] = x

    xn = _rms(x, g_ref[...])
    for c in range(ROWS_O):
        xp_ref[pl.ds(c, tm, stride=ROWS_O), :] = xn[:, LANES * c:LANES * (c + 1)]

    xh, xl = _split_bf16(xn)
    logits = (jnp.dot(xh, wrh_ref[...], preferred_element_type=F32)
              + jnp.dot(xl, wrh_ref[...], preferred_element_type=F32)
              + jnp.dot(xh, wrl_ref[...], preferred_element_type=F32)
              + jnp.dot(xl, wrl_ref[...], preferred_element_type=F32)) + br_ref[...]
    lane = lax.broadcasted_iota(I32, logits.shape, 1)

    def first_argmax(vals):
        top = jnp.max(vals, axis=1, keepdims=True)
        return top, jnp.min(jnp.where(vals == top, lane, LANES), axis=1, keepdims=True)

    is_group = lane < MOE_GROUPS
    g_top, g_idx = first_argmax(jnp.where(is_group, logits, -jnp.inf))
    g_w = 1.0 / jnp.sum(jnp.where(is_group, jnp.exp(logits - g_top), 0.0), axis=1, keepdims=True)
    first = MOE_GROUPS + EXPERTS_PER_GROUP * g_idx
    e_log = jnp.where((lane >= first) & (lane < first + EXPERTS_PER_GROUP), logits, -jnp.inf)
    v1, l1 = first_argmax(e_log)
    v2, l2 = first_argmax(jnp.where(lane == l1, -jnp.inf, e_log))
    t = jnp.exp(v2 - v1)
    w1 = 1.0 / (1.0 + t)
    w2 = t * w1
    e1, e2 = l1 - MOE_GROUPS, l2 - MOE_GROUPS

    @pl.when(i == 0)
    def _():
        carry_ref[...] = cin_ref[...]

    hit1, hit2 = lane == e1, lane == e2
    oh = jnp.where(hit1 | hit2, 1.0, 0.0)
    before = jnp.dot(tril_ref[...], oh.astype(BF16), preferred_element_type=F32) + carry_ref[...]
    r1 = jnp.sum(jnp.where(hit1, before, 0.0), axis=1, keepdims=True).astype(I32)
    r2 = jnp.sum(jnp.where(hit2, before, 0.0), axis=1, keepdims=True).astype(I32)
    carry_ref[...] = carry_ref[...] + jnp.sum(oh, axis=0, keepdims=True)
    cnt_ref[...] = carry_ref[...]

    mi_ref[...] = jnp.where(lane == 0, e1, jnp.where(lane == 1, e2, jnp.where(lane == 2, r1, r2)))
    mf_ref[...] = jnp.where(lane == 0, g_w * w1, g_w * w2)


def _post_mixer(x_src, y_pairs, ws, g, router, tril, t_p, t_s):
    wrh, wrl, br = router
    d = ws[0].shape[1]
    n_pb = t_p // TM
    n_tot = (t_p + t_s) // TM
    row = lambda width: pl.BlockSpec((TM, width), lambda i: (i, 0))
    in_specs, args = [], []
    if len(x_src) == 1:
        in_specs.append(row(d))
    else:
        in_specs += _pair_specs(TM, d, n_pb)
    args += list(x_src)
    for yp, ys in y_pairs:
        in_specs += _pair_specs(TM, yp.shape[1], n_pb)
        args += [yp, ys]
    consts = list(ws) + [g.reshape(1, d), wrh, wrl, br, tril, jnp.zeros((1, LANES), F32)]
    in_specs += [_const_spec(c.shape) for c in consts]
    t = t_p + t_s
    return pl.pallas_call(
        functools.partial(_post_body, n_x=len(x_src), n_y=len(y_pairs), n_pb=n_pb, tm=TM),
        grid=(n_tot,),
        in_specs=in_specs,
        out_specs=[row(d), pl.BlockSpec((TM * ROWS_O, LANES), lambda i: (i, 0)), row(LANES), row(LANES),
                   _const_spec((1, LANES))],
        out_shape=[jax.ShapeDtypeStruct((t, d), F32), jax.ShapeDtypeStruct((t * ROWS_O, LANES), F32),
                   jax.ShapeDtypeStruct((t, LANES), I32), jax.ShapeDtypeStruct((t, LANES), F32),
                   jax.ShapeDtypeStruct((1, LANES), F32)],
        scratch_shapes=[pltpu.VMEM((1, LANES), F32)],
        compiler_params=_cparams(1),
        name="post_mixer",
    )(*args, *consts)


def _router_weights(gate_w, gate_b, router_w, router_b):
    d = gate_w.shape[0]
    w = jnp.concatenate([gate_w, router_w.transpose(1, 0, 2).reshape(d, N_EXPERTS)], axis=1)
    b = jnp.concatenate([gate_b, router_b.reshape(N_EXPERTS)])
    w = jnp.pad(w, ((0, 0), (0, LANES - w.shape[1])))
    b = jnp.pad(b, (0, LANES - b.shape[0])).reshape(1, LANES)
    hi, lo = _split_bf16(w)
    return hi, lo, b


def _dispatch_body(slots_ref, pad_pos_ref, pad_n_ref, x_ref, xs_ref, zero_ref, sem, *, tm):
    i = pl.program_id(0)

    def zero_copy(pos, n):
        return pltpu.make_async_copy(zero_ref.at[pl.ds(0, n)], xs_ref.at[pl.ds(pos, n)], sem.at[1])

    @pl.when(i == 0)
    def _():
        zero_ref[...] = jnp.zeros(zero_ref.shape, zero_ref.dtype)
        for e in range(N_EXPERTS):
            pos, n = pad_pos_ref[e], pad_n_ref[e]
            bit = tm // 2
            while bit >= 1:
                @pl.when((n & bit) != 0)
                def _(pos=pos, bit=bit):
                    cp = zero_copy(pos, bit)
                    cp.start()
                    cp.wait()
                pos = pos + (n & bit)
                bit //= 2
        used_tiles = (pad_pos_ref[N_EXPERTS - 1] + pad_n_ref[N_EXPERTS - 1]) // tm
        all_tiles = xs_ref.shape[0] // tm
        lax.fori_loop(used_tiles, all_tiles, lambda tl, c: (zero_copy(tl * tm, tm).start(), c)[1], 0)
        lax.fori_loop(used_tiles, all_tiles, lambda tl, c: (zero_copy(tl * tm, tm).wait(), c)[1], 0)

    def start(r, c):
        for k in range(2):
            slot = slots_ref[2 * (i * tm + r) + k]
            pltpu.make_async_copy(x_ref.at[r], xs_ref.at[slot], sem.at[0]).start(priority=k)
        return c

    lax.fori_loop(0, tm, start, 0, unroll=8)
    pltpu.make_async_copy(xs_ref.at[pl.ds(0, 2 * tm)], xs_ref.at[pl.ds(0, 2 * tm)], sem.at[0]).wait()


def _dispatch(xrows, slots, pad_pos, pad_n, n_slots):
    t = xrows.shape[0]
    slab = xrows.shape[1:]
    grid_spec = pltpu.PrefetchScalarGridSpec(
        num_scalar_prefetch=3,
        grid=(t // TM_MOE,),
        in_specs=[pl.BlockSpec((TM_MOE,) + slab, lambda i, *_: (i, 0, 0))],
        out_specs=pl.BlockSpec(memory_space=pl.ANY),
        scratch_shapes=[pltpu.VMEM((TM_MOE,) + slab, xrows.dtype), pltpu.SemaphoreType.DMA((2,))],
    )
    return pl.pallas_call(
        functools.partial(_dispatch_body, tm=TM_MOE),
        grid_spec=grid_spec,
        out_shape=jax.ShapeDtypeStruct((n_slots,) + slab, xrows.dtype),
        compiler_params=_cparams(1),
        name="moe_dispatch",
    )(slots, pad_pos, pad_n, xrows)


def _moe_body(te_ref, nt_ref, x_ref, wgu_ref, wd_ref, o_ref, wgu_bf, wd_bf, *, tm):
    i = pl.program_id(0)
    prev = te_ref[jnp.maximum(i - 1, 0)]

    @pl.when((i == 0) | (te_ref[i] != prev))
    def _():
        wgu_bf[...] = wgu_ref[0, 0].astype(BF16)
        wd_bf[...] = wd_ref[0, 0].astype(BF16)

    @pl.when(i < nt_ref[0])
    def _():
        x = jnp.concatenate([x_ref[pl.ds(c, tm, stride=ROWS_O), :] for c in range(ROWS_O)], axis=1)
        h = jnp.dot(x.astype(BF16), wgu_bf[...], preferred_element_type=F32)
        d_exp = h.shape[1] // 2
        gt, up = h[:, :d_exp], h[:, d_exp:]
        act = (gt * _sigmoid(gt) * up).astype(BF16)
        out = jnp.dot(act, wd_bf[...], preferred_element_type=F32)
        for c in range(ROWS_O):
            o_ref[pl.ds(c, tm, stride=ROWS_O), :] = out[:, LANES * c:LANES * (c + 1)]

    @pl.when(i >= nt_ref[0])
    def _():
        o_ref[...] = jnp.zeros(o_ref.shape, o_ref.dtype)


def _moe_experts(xsorted, tile_expert, n_tiles, wgu, wd, layer):
    n_slots = xsorted.shape[0] // ROWS_O
    n_max = n_slots // TM_MOE
    grid_spec = pltpu.PrefetchScalarGridSpec(
        num_scalar_prefetch=2,
        grid=(n_max,),
        in_specs=[pl.BlockSpec((TM_MOE * ROWS_O, LANES), lambda i, te, nt: (jnp.minimum(i, nt[0] - 1), 0)),
                  pl.BlockSpec((1, 1) + wgu.shape[2:], lambda i, te, nt: (layer, te[i], 0, 0)),
                  pl.BlockSpec((1, 1) + wd.shape[2:], lambda i, te, nt: (layer, te[i], 0, 0))],
        out_specs=pl.BlockSpec((TM_MOE * ROWS_O, LANES), lambda i, te, nt: (i, 0)),
        scratch_shapes=[pltpu.VMEM(wgu.shape[2:], BF16), pltpu.VMEM(wd.shape[2:], BF16)],
    )
    return pl.pallas_call(
        functools.partial(_moe_body, tm=TM_MOE),
        grid_spec=grid_spec,
        out_shape=jax.ShapeDtypeStruct((n_slots * ROWS_O, LANES), F32),
        compiler_params=_cparams(1),
        name="moe_experts",
    )(tile_expert, n_tiles, xsorted, wgu, wd)


def _combine_body(slots_ref, x_ref, mf_ref, os_ref, *rest, tm, n_steps, n_pb):
    out_refs, (buf_ref, sem) = rest[:-2], rest[-2:]
    i = pl.program_id(0)

    def issue(tile, par):
        def body(r, c):
            for k in range(2):
                slot = slots_ref[2 * (tile * tm + r) + k]
                dst = buf_ref.at[par, k, pl.ds(pl.multiple_of(r * ROWS_O, ROWS_O), ROWS_O)]
                pltpu.make_async_copy(os_ref.at[slot], dst, sem.at[par]).start(priority=k)
            return c
        lax.fori_loop(0, tm, body, 0, unroll=8)

    @pl.when(i == 0)
    def _():
        issue(0, 0)

    @pl.when(i + 1 < n_steps)
    def _():
        issue(i + 1, (i + 1) % 2)

    par = i % 2
    pltpu.make_async_copy(buf_ref.at[par], buf_ref.at[par], sem.at[par]).wait()

    def write(out_ref):
        mf = mf_ref[...]
        g1, g2 = mf[:, 0:1], mf[:, 1:2]
        for c in range(ROWS_O):
            cols = slice(LANES * c, LANES * (c + 1))
            piece = lambda k: buf_ref[par, k, pl.ds(c, tm, stride=ROWS_O), :]
            out_ref[:, cols] = x_ref[:, cols] + g1 * piece(0) + g2 * piece(1)

    if len(out_refs) == 1:
        write(out_refs[0])
    else:
        pl.when(i < n_pb)(lambda: write(out_refs[0]))
        pl.when(i >= n_pb)(lambda: write(out_refs[1]))


def _combine(x, meta_f, slots, osorted, t_p, split):
    t, d = x.shape
    n_steps = t // TM_MOE
    n_pb = t_p // TM_MOE
    row = lambda width: pl.BlockSpec((TM_MOE, width), lambda i, *_: (i, 0))
    if split:
        out_specs = [pl.BlockSpec((TM_MOE, d), lambda i, *_: (jnp.minimum(i, n_pb - 1), 0)),
                     pl.BlockSpec((TM_MOE, d), lambda i, *_: (jnp.maximum(i - n_pb, 0), 0))]
        out_shape = [jax.ShapeDtypeStruct((t_p, d), F32), jax.ShapeDtypeStruct((t - t_p, d), F32)]
    else:
        out_specs = [row(d)]
        out_shape = [jax.ShapeDtypeStruct((t, d), F32)]
    grid_spec = pltpu.PrefetchScalarGridSpec(
        num_scalar_prefetch=1,
        grid=(n_steps,),
        in_specs=[row(d), row(LANES), pl.BlockSpec(memory_space=pl.ANY)],
        out_specs=out_specs,
        scratch_shapes=[pltpu.VMEM((2, 2, TM_MOE * ROWS_O, LANES), F32), pltpu.SemaphoreType.DMA((2,))],
    )
    return pl.pallas_call(
        functools.partial(_combine_body, tm=TM_MOE, n_steps=n_steps, n_pb=n_pb),
        grid_spec=grid_spec,
        out_shape=out_shape,
        compiler_params=_cparams(1),
        name="moe_combine",
    )(slots, x, meta_f, osorted)


def _moe_layer(x, xrows, meta_i, meta_f, counts, wgu, wd, layer, t_p, split):
    t = x.shape[0]
    n_max = (2 * t + N_EXPERTS * (TM_MOE - 1)) // TM_MOE
    cnt = counts[0, :N_EXPERTS].astype(I32)
    padded = (cnt + TM_MOE - 1) // TM_MOE * TM_MOE
    ends = jnp.cumsum(padded)
    offs = ends - padded
    experts = jnp.arange(N_EXPERTS, dtype=I32)
    first_slot = jnp.sum(jnp.where(meta_i[:, 0:2, None] == experts, offs, 0), axis=-1)
    slots = (first_slot + meta_i[:, 2:4]).reshape(-1)
    n_tiles = (ends[-1] // TM_MOE).reshape(1)
    tile_start = jnp.arange(n_max, dtype=I32) * TM_MOE
    tile_expert = jnp.minimum(jnp.sum((ends[None, :] <= tile_start[:, None]).astype(I32), axis=1), N_EXPERTS - 1)
    n_slots = n_max * TM_MOE
    xsorted = _dispatch(xrows.reshape(t, ROWS_O, LANES), slots, offs + cnt, padded - cnt, n_slots)
    osorted = _moe_experts(xsorted.reshape(n_slots * ROWS_O, LANES), tile_expert, n_tiles, wgu, wd, layer)
    return _combine(x, meta_f, slots, osorted.reshape(n_slots, ROWS_O, LANES), t_p, split)


def _conv_in_body(x_ref, g_ref, w_ref, u_ref):
    xn = _rms(x_ref[...], g_ref[...]).astype(BF16)
    h = jnp.dot(xn, w_ref[...], preferred_element_type=F32)
    c = h.shape[1] // 2
    u_ref[...] = h[:, :c] * _sigmoid(h[:, c:])


def _conv_in(x, g, w):
    t, d = x.shape
    c = w.shape[1] // 2
    return pl.pallas_call(
        _conv_in_body,
        grid=(t // TM,),
        in_specs=[pl.BlockSpec((TM, d), lambda i: (i, 0)), _const_spec((1, d)), _const_spec(w.shape)],
        out_specs=pl.BlockSpec((TM, c), lambda i: (i, 0)),
        out_shape=jax.ShapeDtypeStruct((t, c), F32),
        compiler_params=_cparams(1),
        name="conv_in",
    )(x, g.reshape(1, d), w.astype(BF16))


def _dwconv_body(cur_ref, prev_ref, left_ref, w_ref, b_ref, g_ref, beta_ref, o_ref, win_ref, y_ref, *, tm):
    i = pl.program_id(1)
    n_lt = cur_ref.shape[1] // LANES
    halo = jnp.where(i == 0, left_ref[0], prev_ref[...])
    cur = cur_ref[...]
    for c in range(n_lt):
        cols = slice(LANES * c, LANES * (c + 1))
        win_ref[pl.ds(c, CONV_HALO, stride=n_lt), :] = halo[:, cols]
        win_ref[pl.ds(CONV_HALO * n_lt + c, tm, stride=n_lt), :] = cur[:, cols]
    first_tap = CONV_HALO - (CONV_WIDTH - 1)
    taps = [w_ref[j] for j in range(CONV_WIDTH)]

    def block(rb, carry):
        for t in range(SUBLANES):
            frame = rb * SUBLANES + t
            acc = None
            for j in range(CONV_WIDTH):
                row = pl.multiple_of((frame + first_tap + j) * n_lt, n_lt)
                term = taps[j] * win_ref[pl.ds(row, n_lt), :]
                acc = term if acc is None else acc + term
            y_ref[pl.ds(pl.multiple_of(frame * n_lt, n_lt), n_lt), :] = acc
        return carry

    lax.fori_loop(0, tm // SUBLANES, block, 0)
    y = jnp.concatenate([y_ref[pl.ds(c, tm, stride=n_lt), :] for c in range(n_lt)], axis=1)
    y = y + b_ref[...]
    mu = jnp.mean(y, axis=1, keepdims=True)
    yc = y - mu
    yn = yc * lax.rsqrt(jnp.mean(yc * yc, axis=1, keepdims=True) + EPS) * g_ref[...] + beta_ref[...]
    o_ref[...] = (yn * _sigmoid(yn)).astype(BF16)


def _dwconv(u, left, dw_w, dw_b, ln_g, ln_b, row0, nb, seq, tm):
    c = u.shape[1]
    n_t = seq // tm
    blk0 = row0 // tm
    halo_per_tile = tm // CONV_HALO
    vec = lambda a: a.reshape(1, c)
    w = dw_w.reshape(CONV_WIDTH, c // LANES, LANES)
    return pl.pallas_call(
        functools.partial(_dwconv_body, tm=tm),
        grid=(nb, n_t),
        in_specs=[pl.BlockSpec((tm, c), lambda b, i: (blk0 + b * n_t + i, 0)),
                  pl.BlockSpec((CONV_HALO, c),
                               lambda b, i: (jnp.maximum((blk0 + b * n_t + i) * halo_per_tile - 1, 0), 0)),
                  pl.BlockSpec((1, CONV_HALO, c), lambda b, i: (b, 0, 0)),
                  _const_spec(w.shape), _const_spec((1, c)), _const_spec((1, c)), _const_spec((1, c))],
        out_specs=pl.BlockSpec((tm, c), lambda b, i: (b * n_t + i, 0)),
        out_shape=jax.ShapeDtypeStruct((nb * seq, c), BF16),
        scratch_shapes=[pltpu.VMEM(((tm + CONV_HALO) * (c // LANES), LANES), F32),
                        pltpu.VMEM((tm * (c // LANES), LANES), F32)],
        compiler_params=_cparams(2),
        name=f"dwconv_b{nb}",
    )(u, u, left, w, vec(dw_b), vec(ln_g), vec(ln_b))


def _block_diag(blocks):
    g, a, b = blocks.shape
    return jnp.einsum("gab,gh->gahb", blocks, jnp.eye(g, dtype=blocks.dtype)).reshape(g * a, g * b)


def _state_rows(re, im):
    nb = re.shape[0]
    return jnp.concatenate([re.reshape(nb, -1, LANES), im.reshape(nb, -1, LANES)], axis=1)


def kernel(x_prompt, x_sample, cache_attn_k, cache_attn_v, state_ssm_re, state_ssm_im, cache_conv, norm_mixer, norm_ffn, w_in_even, w_out_even, ssm_a_re, ssm_a_im, ssm_log_dt, ssm_b_re, ssm_b_im, ssm_c_re, ssm_c_im, ssm_d, ssm_glu_w, ssm_glu_b, attn_q_gain, attn_k_gain, attn_rel_bias, conv_w_in, conv_dw_w, conv_dw_b, conv_ln_g, conv_ln_b, conv_w_out, moe_gate_w, moe_gate_b, moe_router_w, moe_router_b, moe_w_gate_up, moe_w_down):
    nb_p, seq_p, d = x_prompt.shape
    nb_s, seq_s, _ = x_sample.shape
    t_p, t_s = nb_p * seq_p, nb_s * seq_s
    xp, xs = x_prompt.reshape(t_p, d), x_sample.reshape(t_s, d)
    depth = norm_mixer.shape[0]
    tril = jnp.tril(jnp.ones((TM, TM), F32), -1).astype(BF16)

    ks_p, vs_p, hs_p, ks_s, vs_s, hs_s, conv_p, conv_s = [], [], [], [], [], [], [], []
    x_src = (xp, xs)
    for layer in range(depth):
        j = layer // 2
        if layer % 2 == 0:
            u, q, k, v = _inproj(*x_src, norm_mixer[layer], w_in_even[j], attn_q_gain[j], attn_k_gain[j])
            d_ssm = u.shape[1]
            groups, n_state = ssm_a_re[j].shape
            bar_re, bar_im, bb_re, bb_im = _ssm_discretize(ssm_a_re[j], ssm_a_im[j], ssm_log_dt[j],
                                                           ssm_b_re[j], ssm_b_im[j])
            lam = jnp.concatenate([bar_re.reshape(-1, LANES), bar_im.reshape(-1, LANES)], axis=0)
            to_state = lambda m: _block_diag(m.reshape(groups, n_state, SSM_GROUP).transpose(0, 2, 1))
            bbig = jnp.concatenate([to_state(bb_re), to_state(bb_im)], axis=1).astype(BF16)
            cbig = jnp.concatenate([_block_diag(ssm_c_re[j].transpose(0, 2, 1)),
                                    _block_diag(-ssm_c_im[j].transpose(0, 2, 1))], axis=0).astype(BF16)
            n_q, k_q = 4, d_ssm // 2
            w_q = bbig.shape[1] // n_q
            blocks = lambda q: (slice(k_q * (q % 2), k_q * (q % 2 + 1)), slice(w_q * q, w_q * (q + 1)))
            bbig = jnp.stack([bbig[blocks(q)[0], blocks(q)[1]] for q in range(n_q)])
            cbig = jnp.stack([cbig[blocks(q)[1], blocks(q)[0]] for q in range(n_q)])
            ssm_args = (lam, bbig, cbig, ssm_d[j].reshape(1, d_ssm), _block_diag(ssm_glu_w[j]).astype(BF16),
                        ssm_glu_b[j].reshape(1, d_ssm))
            zero_state = jnp.zeros((nb_p, groups, n_state), F32)
            y_ssm_p, h_p = _ssm(u, 0, nb_p, seq_p, 256, _state_rows(zero_state, zero_state), *ssm_args)
            y_ssm_s, h_s = _ssm(u, t_p, nb_s, seq_s, seq_s, _state_rows(state_ssm_re[j], state_ssm_im[j]), *ssm_args)
            base = _bias_base(attn_rel_bias[j])
            heads = base.shape[0]
            y_att_p = _attn_prompt(q, k, v, base.reshape(heads // 2, 2, -1), nb_p, seq_p)
            y_att_s = _attn_sample(q, k, v, cache_attn_k[j], cache_attn_v[j], base, t_p)
            w_out = w_out_even[j].astype(BF16)
            y_pairs = [(y_ssm_p.reshape(t_p, d_ssm), y_ssm_s.reshape(t_s, d_ssm)), (y_att_p, y_att_s)]
            ws = [w_out[:d_ssm], w_out[d_ssm:]]

            keep = min(BAND_ROWS, seq_p)
            heads_of = lambda a, nb: a.reshape(nb, -1, heads, HEAD_DIM).transpose(0, 2, 1, 3)
            tail_p = lambda a: jnp.concatenate([a[(b + 1) * seq_p - keep:(b + 1) * seq_p] for b in range(nb_p)])
            ks_p.append(heads_of(tail_p(k), nb_p))
            vs_p.append(heads_of(tail_p(v), nb_p))
            ks_s.append(heads_of(k[t_p:], nb_s))
            vs_s.append(heads_of(v[t_p:], nb_s))
            half = h_p.shape[1] // 2
            hs_p.append((h_p[:, :half].reshape(nb_p, groups, n_state), h_p[:, half:].reshape(nb_p, groups, n_state)))
            hs_s.append((h_s[:, :half].reshape(nb_s, groups, n_state), h_s[:, half:].reshape(nb_s, groups, n_state)))
        else:
            (x_merged,) = x_src
            uc = _conv_in(x_merged, norm_mixer[layer], conv_w_in[j])
            c = uc.shape[1]
            conv_args = (conv_dw_w[j], conv_dw_b[j], conv_ln_g[j], conv_ln_b[j])
            left_p = jnp.zeros((nb_p, CONV_HALO, c), F32)
            left_s = jnp.pad(cache_conv[j], ((0, 0), (CONV_HALO - (CONV_WIDTH - 1), 0), (0, 0)))
            act_p = _dwconv(uc, left_p, *conv_args, 0, nb_p, seq_p, TM_CONV)
            act_s = _dwconv(uc, left_s, *conv_args, t_p, nb_s, seq_s, seq_s)
            y_pairs = [(act_p, act_s)]
            ws = [conv_w_out[j].astype(BF16)]
            keep = CONV_WIDTH - 1
            conv_p.append(jnp.stack([uc[(b + 1) * seq_p - keep:(b + 1) * seq_p] for b in range(nb_p)]))
            ext_s = jnp.concatenate([cache_conv[j], uc[t_p:].reshape(nb_s, seq_s, c)], axis=1)
            conv_s.append(ext_s[:, ext_s.shape[1] - keep:])

        router = _router_weights(moe_gate_w[layer], moe_gate_b[layer], moe_router_w[layer], moe_router_b[layer])
        x_new, xrows, meta_i, meta_f, counts = _post_mixer(x_src, y_pairs, ws, norm_ffn[layer], router, tril, t_p, t_s)
        last = layer == depth - 1
        out = _moe_layer(x_new, xrows, meta_i, meta_f, counts, moe_w_gate_up, moe_w_down, layer, t_p, last)
        x_src = tuple(out)

    y_prompt, y_sample = x_src
    stack_c = lambda parts, idx: jnp.stack([p[idx] for p in parts])
    return (y_prompt.reshape(nb_p, seq_p, d), y_sample.reshape(nb_s, seq_s, d),
            jnp.stack(ks_p), jnp.stack(vs_p), stack_c(hs_p, 0), stack_c(hs_p, 1), jnp.stack(conv_p),
            jnp.stack(ks_s), jnp.stack(vs_s), stack_c(hs_s, 0), stack_c(hs_s, 1), jnp.stack(conv_s))
```

```python
import functools
import math

import jax
import jax.numpy as jnp
from jax import lax
from jax.experimental import pallas as pl
from jax.experimental.pallas import tpu as pltpu

F32, BF16, I32, U32 = jnp.float32, jnp.bfloat16, jnp.int32, jnp.uint32

CHUNK = 64
SSM_GROUP = 16
SSM_STATE = 64
HEAD_DIM = 64
LEFT_CHUNKS = 8
BAND_ROWS = LEFT_CHUNKS * CHUNK
MAX_REL = 128
CONV_WIDTH = 31
MOE_GROUPS = 4
EXPERTS_PER_GROUP = 8
N_EXPERTS = MOE_GROUPS * EXPERTS_PER_GROUP
EPS = 1e-6
NEG_INF = -1e30

LANES = 128
SUBLANES = 8
VMEM_LIMIT_BYTES = 56 * 1024 * 1024

TM = 512
TM_MOE = 256
QBLK = 2 * CHUNK
KWIN = QBLK + BAND_ROWS
QSTEP = 4
D_MODEL = 1024
ROWS_O = D_MODEL // LANES
CONV_HALO = 32
TM_CONV = 512


def _cparams(n_grid):
    return pltpu.CompilerParams(dimension_semantics=("arbitrary",) * n_grid, vmem_limit_bytes=VMEM_LIMIT_BYTES)


def _sigmoid(x):
    return 1.0 / (1.0 + jnp.exp(-x))


def _rms(x, g):
    return x * lax.rsqrt(jnp.mean(x * x, axis=-1, keepdims=True) + EPS) * g


def _split_bf16(x):
    hi = lax.bitcast_convert_type(lax.bitcast_convert_type(x, U32) & jnp.uint32(0xFFFF0000), F32)
    return hi.astype(BF16), (x - hi).astype(BF16)


def _pair_specs(tm, width, n_prompt_blocks):
    return [pl.BlockSpec((tm, width), lambda i: (jnp.minimum(i, n_prompt_blocks - 1), 0)),
            pl.BlockSpec((tm, width), lambda i: (jnp.maximum(i - n_prompt_blocks, 0), 0))]


def _const_spec(shape):
    nd = len(shape)
    return pl.BlockSpec(shape, lambda *a: (0,) * nd)


def _inproj_body(xp_ref, xs_ref, g_ref, w_ref, qg_ref, kg_ref, seg_ref, u_ref, q_ref, k_ref, v_ref, *, n_pb, d_half):
    i = pl.program_id(0)
    x = jnp.where(i < n_pb, xp_ref[...], xs_ref[...])
    xn = _rms(x, g_ref[...]).astype(BF16)
    proj = jnp.dot(xn, w_ref[...], preferred_element_type=F32)
    u_ref[...] = proj[:, :d_half]
    v_ref[...] = proj[:, 3 * d_half:]

    def head_norm(t, gain):
        hi, lo = _split_bf16(t * t)
        msq = (jnp.dot(hi, seg_ref[...], preferred_element_type=F32)
               + jnp.dot(lo, seg_ref[...], preferred_element_type=F32))
        return t * lax.rsqrt(msq + EPS) * gain

    q_ref[...] = (head_norm(proj[:, d_half:2 * d_half], qg_ref[...]) * (HEAD_DIM ** -0.5)).astype(BF16)
    k_ref[...] = head_norm(proj[:, 2 * d_half:3 * d_half], kg_ref[...])


def _inproj(xp, xs, g, w, qg, kg):
    t_p, d = xp.shape
    t_s = xs.shape[0]
    d_half = w.shape[1] // 4
    n_pb = t_p // TM
    n_tot = (t_p + t_s) // TM
    heads = d_half // HEAD_DIM
    seg = jnp.kron(jnp.eye(heads, dtype=F32), jnp.full((HEAD_DIM, HEAD_DIM), 1.0 / HEAD_DIM, F32)).astype(BF16)
    out_f = jax.ShapeDtypeStruct((t_p + t_s, d_half), F32)
    out_b = jax.ShapeDtypeStruct((t_p + t_s, d_half), BF16)
    row = pl.BlockSpec((TM, d_half), lambda i: (i, 0))
    return pl.pallas_call(
        functools.partial(_inproj_body, n_pb=n_pb, d_half=d_half),
        grid=(n_tot,),
        in_specs=_pair_specs(TM, d, n_pb) + [_const_spec((1, d)), _const_spec(w.shape), _const_spec((1, d_half)),
                                             _const_spec((1, d_half)), _const_spec(seg.shape)],
        out_specs=[row, row, row, row],
        out_shape=[out_f, out_b, out_f, out_f],
        compiler_params=_cparams(1),
        name="inproj_even",
    )(xp, xs, g.reshape(1, d), w.astype(BF16), jnp.tile(qg, heads).reshape(1, d_half),
      jnp.tile(kg, heads).reshape(1, d_half), seg)


def _ssm_disc_body(are_ref, aim_ref, ldt_ref, bre_ref, bim_ref, barre_ref, barim_ref, bbre_ref, bbim_ref):
    lam_re, lam_im = are_ref[...], aim_ref[...]
    dt = jnp.exp(ldt_ref[...])
    mag = jnp.exp(lam_re * dt)
    bar_re, bar_im = mag * jnp.cos(lam_im * dt), mag * jnp.sin(lam_im * dt)
    den = lam_re * lam_re + lam_im * lam_im
    num_re = bar_re - 1.0
    coef_re = (num_re * lam_re + bar_im * lam_im) / den
    coef_im = (bar_im * lam_re - num_re * lam_im) / den
    br, bi = bre_ref[...], bim_ref[...]
    barre_ref[...] = bar_re
    barim_ref[...] = bar_im
    bbre_ref[...] = coef_re * br - coef_im * bi
    bbim_ref[...] = coef_re * bi + coef_im * br


def _ssm_discretize(a_re, a_im, log_dt, b_re, b_im):
    g, p = a_re.shape
    n = g * p
    col = jax.ShapeDtypeStruct((n, 1), F32)
    mat = jax.ShapeDtypeStruct((n, SSM_GROUP), F32)
    return pl.pallas_call(_ssm_disc_body, out_shape=[col, col, mat, mat], name="ssm_discretize")(
        a_re.reshape(n, 1), a_im.reshape(n, 1), jnp.repeat(log_dt, p).reshape(n, 1),
        b_re.reshape(n, SSM_GROUP), b_im.reshape(n, SSM_GROUP))


def _gelu_tanh(x):
    return 0.5 * x * (1.0 + jnp.tanh(math.sqrt(2.0 / math.pi) * (x + 0.044715 * (x * x * x))))


def _ssm_body(*refs, nb, lc, n_state_rows):
    u_refs = refs[:nb]
    h0_ref, lam_ref, bq_ref, cq_ref, d_ref, wglu_ref, bglu_ref, y_ref, ht_ref, s_ref, hcar_ref = refs[nb:]
    i = pl.program_id(0)
    half = n_state_rows // 2
    n_q = bq_ref.shape[0]
    tiles_q = n_state_rows // n_q
    k_q = bq_ref.shape[1]

    @pl.when(i == 0)
    def _():
        hcar_ref[...] = h0_ref[...]

    def tile_rows(tile):
        return slice(SUBLANES * tile, SUBLANES * (tile + 1))

    for b in range(nb):
        ub = u_refs[b][...].astype(BF16)
        for q in range(n_q):
            m = q % (n_q // 2)
            res = jnp.dot(ub[:, k_q * m:k_q * (m + 1)], bq_ref[q], preferred_element_type=F32)
            for c in range(tiles_q):
                s_ref[b, :, tile_rows(q * tiles_q + c), :] = (
                    res[:, LANES * c:LANES * (c + 1)].reshape(lc // SUBLANES, SUBLANES, LANES))

    lam_re, lam_im = lam_ref[:half, :], lam_ref[half:, :]

    def block(r, hs):
        hs = list(hs)
        for t in range(SUBLANES):
            for b in range(nb):
                bu = s_ref[b, r, pl.ds(t, n_state_rows, stride=SUBLANES), :]
                hr, hi = hs[2 * b], hs[2 * b + 1]
                nr = lam_re * hr - lam_im * hi + bu[:half]
                ni = lam_re * hi + lam_im * hr + bu[half:]
                s_ref[b, r, pl.ds(t, n_state_rows, stride=SUBLANES), :] = jnp.concatenate([nr, ni], axis=0)
                hs[2 * b], hs[2 * b + 1] = nr, ni
        return tuple(hs)

    init = []
    for b in range(nb):
        init += [hcar_ref[b, :half, :], hcar_ref[b, half:, :]]
    fin = lax.fori_loop(0, lc // SUBLANES, block, tuple(init))
    for b in range(nb):
        hb = jnp.concatenate([fin[2 * b], fin[2 * b + 1]], axis=0)
        hcar_ref[b] = hb
        ht_ref[b] = hb

    for b in range(nb):
        accs = []
        for m in range(n_q // 2):
            acc = None
            for q in (m, m + n_q // 2):
                tiles = [s_ref[b, :, tile_rows(q * tiles_q + c), :].reshape(lc, LANES) for c in range(tiles_q)]
                part = jnp.dot(jnp.concatenate(tiles, axis=1).astype(BF16), cq_ref[q], preferred_element_type=F32)
                acc = part if acc is None else acc + part
            accs.append(acc)
        y = jnp.concatenate(accs, axis=1) + d_ref[...] * u_refs[b][...]
        z = _gelu_tanh(y)
        gate = _sigmoid(jnp.dot(z.astype(BF16), wglu_ref[...], preferred_element_type=F32) + bglu_ref[...])
        y_ref[b] = (z * gate).astype(y_ref.dtype)


def _ssm(u, row0, nb, seq, lc, h0, lam, bbig, cbig, dvec, wglu, bglu):
    d_ssm = u.shape[1]
    n_state_rows = lam.shape[0]
    n_chunks = seq // lc
    blk0 = row0 // lc
    u_specs = [pl.BlockSpec((lc, d_ssm), functools.partial(lambda i, b: (blk0 + b * n_chunks + i, 0), b=b))
               for b in range(nb)]
    return pl.pallas_call(
        functools.partial(_ssm_body, nb=nb, lc=lc, n_state_rows=n_state_rows),
        grid=(n_chunks,),
        in_specs=u_specs + [_const_spec(h0.shape), _const_spec(lam.shape), _const_spec(bbig.shape),
                            _const_spec(cbig.shape), _const_spec(dvec.shape), _const_spec(wglu.shape),
                            _const_spec(bglu.shape)],
        out_specs=[pl.BlockSpec((nb, lc, d_ssm), lambda i: (0, i, 0)), _const_spec(h0.shape)],
        out_shape=[jax.ShapeDtypeStruct((nb, seq, d_ssm), BF16), jax.ShapeDtypeStruct(h0.shape, F32)],
        scratch_shapes=[pltpu.VMEM((nb, lc // SUBLANES, SUBLANES * n_state_rows, LANES), F32),
                        pltpu.VMEM(h0.shape, F32)],
        compiler_params=_cparams(1),
        name=f"ssm_b{nb}",
    )(*([u] * nb), h0, lam, bbig, cbig, dvec, wglu, bglu)


def _bias_rows(m_ref, lead, base, n_rows, n_cols):
    for r in range(n_rows):
        m_ref[lead, r:r + 1, :] = base[:, QBLK - 1 - r:QBLK - 1 - r + n_cols]


def _attn_prompt_body(q_ref, k_ref, v_ref, base_ref, o_ref, kb_ref, vb_ref, m_ref):
    j = pl.program_id(2)

    @pl.when(j == 0)
    def _():
        kb_ref[:BAND_ROWS, :] = jnp.zeros((BAND_ROWS, LANES), BF16)
        vb_ref[:BAND_ROWS, :] = jnp.zeros((BAND_ROWS, LANES), BF16)
        kb_ref[BAND_ROWS:, :] = k_ref[...].astype(BF16)
        vb_ref[BAND_ROWS:, :] = v_ref[...].astype(BF16)
        rq = lax.broadcasted_iota(I32, (QBLK, KWIN), 0) // CHUNK
        ck = lax.broadcasted_iota(I32, (QBLK, KWIN), 1) // CHUNK
        in_band = (ck >= rq) & (ck <= rq + LEFT_CHUNKS)
        for a in range(2):
            _bias_rows(m_ref, a, base_ref[0, a:a + 1, :], QBLK, KWIN)
            m_ref[a] = jnp.where(in_band, m_ref[a], NEG_INF)

    start = pl.multiple_of(j * (QSTEP * QBLK), QSTEP * QBLK)
    lane = lax.broadcasted_iota(I32, (QBLK, LANES), 1)
    col = lax.broadcasted_iota(I32, (2 * QBLK, KWIN), 1)
    bias = m_ref[...].reshape(2 * QBLK, KWIN)
    for c in range(QSTEP):
        q2 = q_ref[QBLK * c:QBLK * (c + 1), :]
        kw = kb_ref[pl.ds(start + QBLK * c, KWIN), :]
        vw = vb_ref[pl.ds(start + QBLK * c, KWIN), :]
        first_valid = BAND_ROWS - (j * QSTEP + c) * QBLK
        zero = jnp.zeros_like(q2)
        qs = jnp.concatenate([jnp.where(lane < HEAD_DIM, q2, zero), jnp.where(lane >= HEAD_DIM, q2, zero)], axis=0)
        s = lax.dot_general(qs, kw, (((1,), (1,)), ((), ())), preferred_element_type=F32)
        s = jnp.where(col >= first_valid, s + bias, NEG_INF)
        p = jnp.exp(s - jnp.max(s, axis=1, keepdims=True))
        o = jnp.dot(p.astype(BF16), vw, preferred_element_type=F32) * (1.0 / jnp.sum(p, axis=1, keepdims=True))
        o_ref[QBLK * c:QBLK * (c + 1), :] = jnp.where(lane < HEAD_DIM, o[:QBLK], o[QBLK:]).astype(o_ref.dtype)


def _attn_prompt(q, k, v, base2, nb, seq):
    d_attn = q.shape[1]
    pairs = d_attn // LANES
    nq = seq // (QSTEP * QBLK)
    kv_spec = pl.BlockSpec((seq, LANES), lambda b, p, j: (b, p))
    return pl.pallas_call(
        _attn_prompt_body,
        grid=(nb, pairs, nq),
        in_specs=[pl.BlockSpec((QSTEP * QBLK, LANES), lambda b, p, j: (b * nq + j, p)), kv_spec, kv_spec,
                  pl.BlockSpec((1, 2, base2.shape[2]), lambda b, p, j: (p, 0, 0))],
        out_specs=pl.BlockSpec((QSTEP * QBLK, LANES), lambda b, p, j: (b * nq + j, p)),
        out_shape=jax.ShapeDtypeStruct((nb * seq, d_attn), BF16),
        scratch_shapes=[pltpu.VMEM((seq + BAND_ROWS, LANES), BF16), pltpu.VMEM((seq + BAND_ROWS, LANES), BF16),
                        pltpu.VMEM((2, QBLK, KWIN), F32)],
        compiler_params=_cparams(3),
        name="attn_prompt",
    )(q, k, v, base2)


def _attn_sample_body(q_ref, k_ref, v_ref, ck_ref, cv_ref, base_ref, o_ref, m_ref, *, heads, n_cache):
    @pl.when(pl.program_id(0) == 0)
    def _():
        for h in range(heads):
            _bias_rows(m_ref, h, base_ref[h:h + 1, :], CHUNK, n_cache + CHUNK)

    q = q_ref[...]
    k = k_ref[...].astype(BF16)
    v = v_ref[...].astype(BF16)
    nt = (((1,), (1,)), ((), ()))
    outs = []
    for h in range(heads):
        sl = slice(HEAD_DIM * h, HEAD_DIM * (h + 1))
        qh = q[:, sl]
        s1 = lax.dot_general(qh, ck_ref[0, h].astype(BF16), nt, preferred_element_type=F32) + m_ref[h, :, :n_cache]
        s2 = lax.dot_general(qh, k[:, sl], nt, preferred_element_type=F32) + m_ref[h, :, n_cache:]
        mx = jnp.maximum(jnp.max(s1, axis=1, keepdims=True), jnp.max(s2, axis=1, keepdims=True))
        p1, p2 = jnp.exp(s1 - mx), jnp.exp(s2 - mx)
        den = jnp.sum(p1, axis=1, keepdims=True) + jnp.sum(p2, axis=1, keepdims=True)
        o = (jnp.dot(p1.astype(BF16), cv_ref[0, h].astype(BF16), preferred_element_type=F32)
             + jnp.dot(p2.astype(BF16), v[:, sl], preferred_element_type=F32))
        outs.append(o * (1.0 / den))
    o_ref[...] = jnp.concatenate(outs, axis=1).astype(o_ref.dtype)


def _attn_sample(q, k, v, cache_k, cache_v, base, row0):
    nb, heads, n_cache, _ = cache_k.shape
    d_attn = q.shape[1]
    blk0 = row0 // CHUNK
    row = pl.BlockSpec((CHUNK, d_attn), lambda b: (blk0 + b, 0))
    cache = pl.BlockSpec((1, heads, n_cache, HEAD_DIM), lambda b: (b, 0, 0, 0))
    return pl.pallas_call(
        functools.partial(_attn_sample_body, heads=heads, n_cache=n_cache),
        grid=(nb,),
        in_specs=[row, row, row, cache, cache, _const_spec(base.shape)],
        out_specs=pl.BlockSpec((CHUNK, d_attn), lambda b: (b, 0)),
        out_shape=jax.ShapeDtypeStruct((nb * CHUNK, d_attn), BF16),
        scratch_shapes=[pltpu.VMEM((heads, CHUNK, n_cache + CHUNK), F32)],
        compiler_params=_cparams(1),
        name="attn_sample",
    )(q, k, v, cache_k, cache_v, base)


def _bias_base(table):
    heads = table.shape[0]
    far = jnp.broadcast_to(table[:, 2 * MAX_REL:], (heads, KWIN - 1 - MAX_REL + 1))
    near = table[:, 2 * MAX_REL - 1:0:-1]
    base = jnp.concatenate([far, near], axis=1)
    return jnp.pad(base, ((0, 0), (0, KWIN + QBLK - base.shape[1])))


def _post_body(*refs, n_x, n_y, n_pb):
    i = pl.program_id(0)
    pos = 0

    def take(n):
        nonlocal pos
        out = refs[pos:pos + n]
        pos += n
        return out

    def merged(pair):
        if len(pair) == 1:
            return pair[0][...]
        return jnp.where(i < n_pb, pair[0][...], pair[1][...])

    x_refs = take(n_x)
    y_pairs = [take(2) for _ in range(n_y)]
    w_refs = take(n_y)
    g_ref, wrh_ref, wrl_ref, br_ref, tril_ref, cin_ref = take(6)
    xo_ref, mi_ref, mf_ref, cnt_ref = take(4)
    (carry_ref,) = take(1)

    x = merged(x_refs)
    for pair, w_ref in zip(y_pairs, w_refs):
        x = x + jnp.dot(merged(pair).astype(BF16), w_ref[...], preferred_element_type=F32)
    xo_ref[...] = x

    xn = _rms(x, g_ref[...])
    xh, xl = _split_bf16(xn)
    logits = (jnp.dot(xh, wrh_ref[...], preferred_element_type=F32)
              + jnp.dot(xl, wrh_ref[...], preferred_element_type=F32)
              + jnp.dot(xh, wrl_ref[...], preferred_element_type=F32)
              + jnp.dot(xl, wrl_ref[...], preferred_element_type=F32)) + br_ref[...]
    lane = lax.broadcasted_iota(I32, logits.shape, 1)

    def first_argmax(vals):
        top = jnp.max(vals, axis=1, keepdims=True)
        return top, jnp.min(jnp.where(vals == top, lane, LANES), axis=1, keepdims=True)

    is_group = lane < MOE_GROUPS
    g_top, g_idx = first_argmax(jnp.where(is_group, logits, -jnp.inf))
    g_w = 1.0 / jnp.sum(jnp.where(is_group, jnp.exp(logits - g_top), 0.0), axis=1, keepdims=True)
    first = MOE_GROUPS + EXPERTS_PER_GROUP * g_idx
    e_log = jnp.where((lane >= first) & (lane < first + EXPERTS_PER_GROUP), logits, -jnp.inf)
    v1, l1 = first_argmax(e_log)
    v2, l2 = first_argmax(jnp.where(lane == l1, -jnp.inf, e_log))
    t = jnp.exp(v2 - v1)
    w1 = 1.0 / (1.0 + t)
    w2 = t * w1
    e1, e2 = l1 - MOE_GROUPS, l2 - MOE_GROUPS

    @pl.when(i == 0)
    def _():
        carry_ref[...] = cin_ref[...]

    hit1, hit2 = lane == e1, lane == e2
    oh = jnp.where(hit1 | hit2, 1.0, 0.0)
    before = jnp.dot(tril_ref[...], oh.astype(BF16), preferred_element_type=F32) + carry_ref[...]
    r1 = jnp.sum(jnp.where(hit1, before, 0.0), axis=1, keepdims=True).astype(I32)
    r2 = jnp.sum(jnp.where(hit2, before, 0.0), axis=1, keepdims=True).astype(I32)
    carry_ref[...] = carry_ref[...] + jnp.sum(oh, axis=0, keepdims=True)
    cnt_ref[...] = carry_ref[...]

    mi_ref[...] = jnp.where(lane == 0, e1, jnp.where(lane == 1, e2, jnp.where(lane == 2, r1, r2)))
    mf_ref[...] = jnp.where(lane == 0, g_w * w1, g_w * w2)


def _post_mixer(x_src, y_pairs, ws, g, router, tril, t_p, t_s):
    wrh, wrl, br = router
    d = ws[0].shape[1]
    n_pb = t_p // TM
    n_tot = (t_p + t_s) // TM
    row = lambda width: pl.BlockSpec((TM, width), lambda i: (i, 0))
    in_specs, args = [], []
    if len(x_src) == 1:
        in_specs.append(row(d))
    else:
        in_specs += _pair_specs(TM, d, n_pb)
    args += list(x_src)
    for yp, ys in y_pairs:
        in_specs += _pair_specs(TM, yp.shape[1], n_pb)
        args += [yp, ys]
    consts = list(ws) + [g.reshape(1, d), wrh, wrl, br, tril, jnp.zeros((1, LANES), F32)]
    in_specs += [_const_spec(c.shape) for c in consts]
    t = t_p + t_s
    return pl.pallas_call(
        functools.partial(_post_body, n_x=len(x_src), n_y=len(y_pairs), n_pb=n_pb),
        grid=(n_tot,),
        in_specs=in_specs,
        out_specs=[row(d), row(LANES), row(LANES), _const_spec((1, LANES))],
        out_shape=[jax.ShapeDtypeStruct((t, d), F32), jax.ShapeDtypeStruct((t, LANES), I32), jax.ShapeDtypeStruct((t, LANES), F32),
                   jax.ShapeDtypeStruct((1, LANES), F32)],
        scratch_shapes=[pltpu.VMEM((1, LANES), F32)],
        compiler_params=_cparams(1),
        name="post_mixer",
    )(*args, *consts)


def _router_weights(gate_w, gate_b, router_w, router_b):
    d = gate_w.shape[0]
    w = jnp.concatenate([gate_w, router_w.transpose(1, 0, 2).reshape(d, N_EXPERTS)], axis=1)
    b = jnp.concatenate([gate_b, router_b.reshape(N_EXPERTS)])
    w = jnp.pad(w, ((0, 0), (0, LANES - w.shape[1])))
    b = jnp.pad(b, (0, LANES - b.shape[0])).reshape(1, LANES)
    hi, lo = _split_bf16(w)
    return hi, lo, b


def _dispatch_body(slots_ref, pad_pos_ref, pad_n_ref, x_ref, g_ref, xs_ref, stage_ref, zero_ref, sem, *, tm, tile):
    i = pl.program_id(0)

    def zero_copy(pos, n):
        return pltpu.make_async_copy(zero_ref.at[pl.ds(0, n)], xs_ref.at[pl.ds(pos, n)], sem.at[1])

    @pl.when(i == 0)
    def _():
        zero_ref[...] = jnp.zeros(zero_ref.shape, zero_ref.dtype)
        for e in range(N_EXPERTS):
            pos, n = pad_pos_ref[e], pad_n_ref[e]
            bit = tile // 2
            while bit >= 1:
                @pl.when((n & bit) != 0)
                def _(pos=pos, bit=bit):
                    cp = zero_copy(pos, bit)
                    cp.start()
                    cp.wait()
                pos = pos + (n & bit)
                bit //= 2
        used_tiles = (pad_pos_ref[N_EXPERTS - 1] + pad_n_ref[N_EXPERTS - 1]) // tile
        all_tiles = xs_ref.shape[0] // tile
        lax.fori_loop(used_tiles, all_tiles, lambda tl, c: (zero_copy(tl * tile, tile).start(), c)[1], 0)
        lax.fori_loop(used_tiles, all_tiles, lambda tl, c: (zero_copy(tl * tile, tile).wait(), c)[1], 0)

    xn = _rms(x_ref[...], g_ref[...])
    for c in range(ROWS_O):
        stage_ref[pl.ds(c, tm, stride=ROWS_O), :] = xn[:, LANES * c:LANES * (c + 1)]

    def start(r, c):
        src = stage_ref.at[pl.ds(pl.multiple_of(r * ROWS_O, ROWS_O), ROWS_O)]
        for k in range(2):
            slot = slots_ref[2 * (i * tm + r) + k]
            pltpu.make_async_copy(src, xs_ref.at[slot], sem.at[0]).start(priority=k)
        return c

    lax.fori_loop(0, tm, start, 0, unroll=8)
    pltpu.make_async_copy(xs_ref.at[pl.ds(0, 2 * tm)], xs_ref.at[pl.ds(0, 2 * tm)], sem.at[0]).wait()


def _dispatch(x, g, slots, pad_pos, pad_n, n_slots):
    t, d = x.shape
    slab = (ROWS_O, LANES)
    grid_spec = pltpu.PrefetchScalarGridSpec(
        num_scalar_prefetch=3,
        grid=(t // TM,),
        in_specs=[pl.BlockSpec((TM, d), lambda i, *_: (i, 0)), pl.BlockSpec((1, d), lambda i, *_: (0, 0))],
        out_specs=pl.BlockSpec(memory_space=pl.ANY),
        scratch_shapes=[pltpu.VMEM((TM * ROWS_O, LANES), F32), pltpu.VMEM((TM_MOE,) + slab, F32),
                        pltpu.SemaphoreType.DMA((2,))],
    )
    return pl.pallas_call(
        functools.partial(_dispatch_body, tm=TM, tile=TM_MOE),
        grid_spec=grid_spec,
        out_shape=jax.ShapeDtypeStruct((n_slots,) + slab, F32),
        compiler_params=_cparams(1),
        name="moe_dispatch",
    )(slots, pad_pos, pad_n, x, g.reshape(1, d))


def _moe_body(te_ref, nt_ref, x_ref, wgu_ref, wd_ref, o_ref, wgu_bf, wd_bf, *, tm):
    i = pl.program_id(0)
    prev = te_ref[jnp.maximum(i - 1, 0)]

    @pl.when((i == 0) | (te_ref[i] != prev))
    def _():
        wgu_bf[...] = wgu_ref[0, 0].astype(BF16)
        wd_bf[...] = wd_ref[0, 0].astype(BF16)

    @pl.when(i < nt_ref[0])
    def _():
        x = jnp.concatenate([x_ref[pl.ds(c, tm, stride=ROWS_O), :] for c in range(ROWS_O)], axis=1)
        h = jnp.dot(x.astype(BF16), wgu_bf[...], preferred_element_type=F32)
        d_exp = h.shape[1] // 2
        gt, up = h[:, :d_exp], h[:, d_exp:]
        act = (gt * _sigmoid(gt) * up).astype(BF16)
        out = jnp.dot(act, wd_bf[...], preferred_element_type=F32)
        for c in range(ROWS_O):
            o_ref[pl.ds(c, tm, stride=ROWS_O), :] = out[:, LANES * c:LANES * (c + 1)]

    @pl.when(i >= nt_ref[0])
    def _():
        o_ref[...] = jnp.zeros(o_ref.shape, o_ref.dtype)


def _moe_experts(xsorted, tile_expert, n_tiles, wgu, wd, layer):
    n_slots = xsorted.shape[0] // ROWS_O
    n_max = n_slots // TM_MOE
    grid_spec = pltpu.PrefetchScalarGridSpec(
        num_scalar_prefetch=2,
        grid=(n_max,),
        in_specs=[pl.BlockSpec((TM_MOE * ROWS_O, LANES), lambda i, te, nt: (jnp.minimum(i, nt[0] - 1), 0)),
                  pl.BlockSpec((1, 1) + wgu.shape[2:], lambda i, te, nt: (layer, te[i], 0, 0)),
                  pl.BlockSpec((1, 1) + wd.shape[2:], lambda i, te, nt: (layer, te[i], 0, 0))],
        out_specs=pl.BlockSpec((TM_MOE * ROWS_O, LANES), lambda i, te, nt: (i, 0)),
        scratch_shapes=[pltpu.VMEM(wgu.shape[2:], BF16), pltpu.VMEM(wd.shape[2:], BF16)],
    )
    return pl.pallas_call(
        functools.partial(_moe_body, tm=TM_MOE),
        grid_spec=grid_spec,
        out_shape=jax.ShapeDtypeStruct((n_slots * ROWS_O, LANES), F32),
        compiler_params=_cparams(1),
        name="moe_experts",
    )(tile_expert, n_tiles, xsorted, wgu, wd)


def _combine_body(slots_ref, x_ref, mf_ref, os_ref, *rest, tm, n_steps, n_pb):
    out_refs, (buf_ref, sem) = rest[:-2], rest[-2:]
    i = pl.program_id(0)

    def issue(tile, par):
        def body(r, c):
            for k in range(2):
                slot = slots_ref[2 * (tile * tm + r) + k]
                dst = buf_ref.at[par, k, pl.ds(pl.multiple_of(r * ROWS_O, ROWS_O), ROWS_O)]
                pltpu.make_async_copy(os_ref.at[slot], dst, sem.at[par]).start(priority=k)
            return c
        lax.fori_loop(0, tm, body, 0, unroll=8)

    @pl.when(i == 0)
    def _():
        issue(0, 0)

    @pl.when(i + 1 < n_steps)
    def _():
        issue(i + 1, (i + 1) % 2)

    par = i % 2
    pltpu.make_async_copy(buf_ref.at[par], buf_ref.at[par], sem.at[par]).wait()

    def write(out_ref):
        mf = mf_ref[...]
        g1, g2 = mf[:, 0:1], mf[:, 1:2]
        for c in range(ROWS_O):
            cols = slice(LANES * c, LANES * (c + 1))
            piece = lambda k: buf_ref[par, k, pl.ds(c, tm, stride=ROWS_O), :]
            out_ref[:, cols] = x_ref[:, cols] + g1 * piece(0) + g2 * piece(1)

    if len(out_refs) == 1:
        write(out_refs[0])
    else:
        pl.when(i < n_pb)(lambda: write(out_refs[0]))
        pl.when(i >= n_pb)(lambda: write(out_refs[1]))


def _combine(x, meta_f, slots, osorted, t_p, split):
    t, d = x.shape
    n_steps = t // TM
    n_pb = t_p // TM
    row = lambda width: pl.BlockSpec((TM, width), lambda i, *_: (i, 0))
    if split:
        out_specs = [pl.BlockSpec((TM, d), lambda i, *_: (jnp.minimum(i, n_pb - 1), 0)),
                     pl.BlockSpec((TM, d), lambda i, *_: (jnp.maximum(i - n_pb, 0), 0))]
        out_shape = [jax.ShapeDtypeStruct((t_p, d), F32), jax.ShapeDtypeStruct((t - t_p, d), F32)]
    else:
        out_specs = [row(d)]
        out_shape = [jax.ShapeDtypeStruct((t, d), F32)]
    grid_spec = pltpu.PrefetchScalarGridSpec(
        num_scalar_prefetch=1,
        grid=(n_steps,),
        in_specs=[row(d), row(LANES), pl.BlockSpec(memory_space=pl.ANY)],
        out_specs=out_specs,
        scratch_shapes=[pltpu.VMEM((2, 2, TM * ROWS_O, LANES), F32), pltpu.SemaphoreType.DMA((2,))],
    )
    return pl.pallas_call(
        functools.partial(_combine_body, tm=TM, n_steps=n_steps, n_pb=n_pb),
        grid_spec=grid_spec,
        out_shape=out_shape,
        compiler_params=_cparams(1),
        name="moe_combine",
    )(slots, x, meta_f, osorted)


def _moe_layer(x, g, meta_i, meta_f, counts, wgu, wd, layer, t_p, split):
    t = x.shape[0]
    n_max = (2 * t + N_EXPERTS * (TM_MOE - 1)) // TM_MOE
    cnt = counts[0, :N_EXPERTS].astype(I32)
    padded = (cnt + TM_MOE - 1) // TM_MOE * TM_MOE
    ends = jnp.cumsum(padded)
    offs = ends - padded
    experts = jnp.arange(N_EXPERTS, dtype=I32)
    first_slot = jnp.sum(jnp.where(meta_i[:, 0:2, None] == experts, offs, 0), axis=-1)
    slots = (first_slot + meta_i[:, 2:4]).reshape(-1)
    n_tiles = (ends[-1] // TM_MOE).reshape(1)
    tile_start = jnp.arange(n_max, dtype=I32) * TM_MOE
    tile_expert = jnp.minimum(jnp.sum((ends[None, :] <= tile_start[:, None]).astype(I32), axis=1), N_EXPERTS - 1)
    n_slots = n_max * TM_MOE
    xsorted = _dispatch(x, g, slots, offs + cnt, padded - cnt, n_slots)
    osorted = _moe_experts(xsorted.reshape(n_slots * ROWS_O, LANES), tile_expert, n_tiles, wgu, wd, layer)
    return _combine(x, meta_f, slots, osorted.reshape(n_slots, ROWS_O, LANES), t_p, split)


def _conv_in_body(x_ref, g_ref, w_ref, u_ref):
    xn = _rms(x_ref[...], g_ref[...]).astype(BF16)
    h = jnp.dot(xn, w_ref[...], preferred_element_type=F32)
    c = h.shape[1] // 2
    u_ref[...] = h[:, :c] * _sigmoid(h[:, c:])


def _conv_in(x, g, w):
    t, d = x.shape
    c = w.shape[1] // 2
    return pl.pallas_call(
        _conv_in_body,
        grid=(t // TM,),
        in_specs=[pl.BlockSpec((TM, d), lambda i: (i, 0)), _const_spec((1, d)), _const_spec(w.shape)],
        out_specs=pl.BlockSpec((TM, c), lambda i: (i, 0)),
        out_shape=jax.ShapeDtypeStruct((t, c), F32),
        compiler_params=_cparams(1),
        name="conv_in",
    )(x, g.reshape(1, d), w.astype(BF16))


def _dwconv_body(cur_ref, prev_ref, left_ref, w_ref, b_ref, g_ref, beta_ref, o_ref, win_ref, y_ref, *, tm):
    i = pl.program_id(1)
    n_lt = cur_ref.shape[1] // LANES
    halo = jnp.where(i == 0, left_ref[0], prev_ref[...])
    cur = cur_ref[...]
    for c in range(n_lt):
        cols = slice(LANES * c, LANES * (c + 1))
        win_ref[pl.ds(c, CONV_HALO, stride=n_lt), :] = halo[:, cols]
        win_ref[pl.ds(CONV_HALO * n_lt + c, tm, stride=n_lt), :] = cur[:, cols]
    first_tap = CONV_HALO - (CONV_WIDTH - 1)
    taps = [w_ref[j] for j in range(CONV_WIDTH)]

    def block(rb, carry):
        for t in range(SUBLANES):
            frame = rb * SUBLANES + t
            acc = None
            for j in range(CONV_WIDTH):
                row = pl.multiple_of((frame + first_tap + j) * n_lt, n_lt)
                term = taps[j] * win_ref[pl.ds(row, n_lt), :]
                acc = term if acc is None else acc + term
            y_ref[pl.ds(pl.multiple_of(frame * n_lt, n_lt), n_lt), :] = acc
        return carry

    lax.fori_loop(0, tm // SUBLANES, block, 0)
    y = jnp.concatenate([y_ref[pl.ds(c, tm, stride=n_lt), :] for c in range(n_lt)], axis=1)
    y = y + b_ref[...]
    mu = jnp.mean(y, axis=1, keepdims=True)
    yc = y - mu
    yn = yc * lax.rsqrt(jnp.mean(yc * yc, axis=1, keepdims=True) + EPS) * g_ref[...] + beta_ref[...]
    o_ref[...] = (yn * _sigmoid(yn)).astype(BF16)


def _dwconv(u, left, dw_w, dw_b, ln_g, ln_b, row0, nb, seq, tm):
    c = u.shape[1]
    n_t = seq // tm
    blk0 = row0 // tm
    halo_per_tile = tm // CONV_HALO
    vec = lambda a: a.reshape(1, c)
    w = dw_w.reshape(CONV_WIDTH, c // LANES, LANES)
    return pl.pallas_call(
        functools.partial(_dwconv_body, tm=tm),
        grid=(nb, n_t),
        in_specs=[pl.BlockSpec((tm, c), lambda b, i: (blk0 + b * n_t + i, 0)),
                  pl.BlockSpec((CONV_HALO, c),
                               lambda b, i: (jnp.maximum((blk0 + b * n_t + i) * halo_per_tile - 1, 0), 0)),
                  pl.BlockSpec((1, CONV_HALO, c), lambda b, i: (b, 0, 0)),
                  _const_spec(w.shape), _const_spec((1, c)), _const_spec((1, c)), _const_spec((1, c))],
        out_specs=pl.BlockSpec((tm, c), lambda b, i: (b * n_t + i, 0)),
        out_shape=jax.ShapeDtypeStruct((nb * seq, c), BF16),
        scratch_shapes=[pltpu.VMEM(((tm + CONV_HALO) * (c // LANES), LANES), F32),
                        pltpu.VMEM((tm * (c // LANES), LANES), F32)],
        compiler_params=_cparams(2),
        name=f"dwconv_b{nb}",
    )(u, u, left, w, vec(dw_b), vec(ln_g), vec(ln_b))


def _block_diag(blocks):
    g, a, b = blocks.shape
    return jnp.einsum("gab,gh->gahb", blocks, jnp.eye(g, dtype=blocks.dtype)).reshape(g * a, g * b)


def _state_rows(re, im):
    nb = re.shape[0]
    return jnp.concatenate([re.reshape(nb, -1, LANES), im.reshape(nb, -1, LANES)], axis=1)


def kernel(x_prompt, x_sample, cache_attn_k, cache_attn_v, state_ssm_re, state_ssm_im, cache_conv, norm_mixer, norm_ffn, w_in_even, w_out_even, ssm_a_re, ssm_a_im, ssm_log_dt, ssm_b_re, ssm_b_im, ssm_c_re, ssm_c_im, ssm_d, ssm_glu_w, ssm_glu_b, attn_q_gain, attn_k_gain, attn_rel_bias, conv_w_in, conv_dw_w, conv_dw_b, conv_ln_g, conv_ln_b, conv_w_out, moe_gate_w, moe_gate_b, moe_router_w, moe_router_b, moe_w_gate_up, moe_w_down):
    nb_p, seq_p, d = x_prompt.shape
    nb_s, seq_s, _ = x_sample.shape
    t_p, t_s = nb_p * seq_p, nb_s * seq_s
    xp, xs = x_prompt.reshape(t_p, d), x_sample.reshape(t_s, d)
    depth = norm_mixer.shape[0]
    tril = jnp.tril(jnp.ones((TM, TM), F32), -1).astype(BF16)

    ks_p, vs_p, hs_p, ks_s, vs_s, hs_s, conv_p, conv_s = [], [], [], [], [], [], [], []
    x_src = (xp, xs)
    for layer in range(depth):
        j = layer // 2
        if layer % 2 == 0:
            u, q, k, v = _inproj(*x_src, norm_mixer[layer], w_in_even[j], attn_q_gain[j], attn_k_gain[j])
            d_ssm = u.shape[1]
            groups, n_state = ssm_a_re[j].shape
            bar_re, bar_im, bb_re, bb_im = _ssm_discretize(ssm_a_re[j], ssm_a_im[j], ssm_log_dt[j],
                                                           ssm_b_re[j], ssm_b_im[j])
            lam = jnp.concatenate([bar_re.reshape(-1, LANES), bar_im.reshape(-1, LANES)], axis=0)
            to_state = lambda m: _block_diag(m.reshape(groups, n_state, SSM_GROUP).transpose(0, 2, 1))
            bbig = jnp.concatenate([to_state(bb_re), to_state(bb_im)], axis=1).astype(BF16)
            cbig = jnp.concatenate([_block_diag(ssm_c_re[j].transpose(0, 2, 1)),
                                    _block_diag(-ssm_c_im[j].transpose(0, 2, 1))], axis=0).astype(BF16)
            n_q, k_q = 4, d_ssm // 2
            w_q = bbig.shape[1] // n_q
            blocks = lambda q: (slice(k_q * (q % 2), k_q * (q % 2 + 1)), slice(w_q * q, w_q * (q + 1)))
            bbig = jnp.stack([bbig[blocks(q)[0], blocks(q)[1]] for q in range(n_q)])
            cbig = jnp.stack([cbig[blocks(q)[1], blocks(q)[0]] for q in range(n_q)])
            ssm_args = (lam, bbig, cbig, ssm_d[j].reshape(1, d_ssm), _block_diag(ssm_glu_w[j]).astype(BF16),
                        ssm_glu_b[j].reshape(1, d_ssm))
            zero_state = jnp.zeros((nb_p, groups, n_state), F32)
            y_ssm_p, h_p = _ssm(u, 0, nb_p, seq_p, 256, _state_rows(zero_state, zero_state), *ssm_args)
            y_ssm_s, h_s = _ssm(u, t_p, nb_s, seq_s, seq_s, _state_rows(state_ssm_re[j], state_ssm_im[j]), *ssm_args)
            base = _bias_base(attn_rel_bias[j])
            heads = base.shape[0]
            y_att_p = _attn_prompt(q, k, v, base.reshape(heads // 2, 2, -1), nb_p, seq_p)
            y_att_s = _attn_sample(q, k, v, cache_attn_k[j], cache_attn_v[j], base, t_p)
            w_out = w_out_even[j].astype(BF16)
            y_pairs = [(y_ssm_p.reshape(t_p, d_ssm), y_ssm_s.reshape(t_s, d_ssm)), (y_att_p, y_att_s)]
            ws = [w_out[:d_ssm], w_out[d_ssm:]]

            keep = min(BAND_ROWS, seq_p)
            heads_of = lambda a, nb: a.reshape(nb, -1, heads, HEAD_DIM).transpose(0, 2, 1, 3)
            tail_p = lambda a: jnp.concatenate([a[(b + 1) * seq_p - keep:(b + 1) * seq_p] for b in range(nb_p)])
            ks_p.append(heads_of(tail_p(k), nb_p))
            vs_p.append(heads_of(tail_p(v), nb_p))
            ks_s.append(heads_of(k[t_p:], nb_s))
            vs_s.append(heads_of(v[t_p:], nb_s))
            half = h_p.shape[1] // 2
            hs_p.append((h_p[:, :half].reshape(nb_p, groups, n_state), h_p[:, half:].reshape(nb_p, groups, n_state)))
            hs_s.append((h_s[:, :half].reshape(nb_s, groups, n_state), h_s[:, half:].reshape(nb_s, groups, n_state)))
        else:
            (x_merged,) = x_src
            uc = _conv_in(x_merged, norm_mixer[layer], conv_w_in[j])
            c = uc.shape[1]
            conv_args = (conv_dw_w[j], conv_dw_b[j], conv_ln_g[j], conv_ln_b[j])
            left_p = jnp.zeros((nb_p, CONV_HALO, c), F32)
            left_s = jnp.pad(cache_conv[j], ((0, 0), (CONV_HALO - (CONV_WIDTH - 1), 0), (0, 0)))
            act_p = _dwconv(uc, left_p, *conv_args, 0, nb_p, seq_p, TM_CONV)
            act_s = _dwconv(uc, left_s, *conv_args, t_p, nb_s, seq_s, seq_s)
            y_pairs = [(act_p, act_s)]
            ws = [conv_w_out[j].astype(BF16)]
            keep = CONV_WIDTH - 1
            conv_p.append(jnp.stack([uc[(b + 1) * seq_p - keep:(b + 1) * seq_p] for b in range(nb_p)]))
            ext_s = jnp.concatenate([cache_conv[j], uc[t_p:].reshape(nb_s, seq_s, c)], axis=1)
            conv_s.append(ext_s[:, ext_s.shape[1] - keep:])

        router = _router_weights(moe_gate_w[layer], moe_gate_b[layer], moe_router_w[layer], moe_router_b[layer])
        x_new, meta_i, meta_f, counts = _post_mixer(x_src, y_pairs, ws, norm_ffn[layer], router, tril, t_p, t_s)
        last = layer == depth - 1
        out = _moe_layer(x_new, norm_ffn[layer], meta_i, meta_f, counts, moe_w_gate_up, moe_w_down, layer, t_p, last)
        x_src = tuple(out)

    y_prompt, y_sample = x_src
    stack_c = lambda parts, idx: jnp.stack([p[idx] for p in parts])
    return (y_prompt.reshape(nb_p, seq_p, d), y_sample.reshape(nb_s, seq_s, d),
            jnp.stack(ks_p), jnp.stack(vs_p), stack_c(hs_p, 0), stack_c(hs_p, 1), jnp.stack(conv_p),
            jnp.stack(ks_s), jnp.stack(vs_s), stack_c(hs_s, 0), stack_c(hs_s, 1), jnp.stack(conv_s))
```

```python
import functools
import math

import jax
import jax.numpy as jnp
from jax import lax
from jax.experimental import pallas as pl
from jax.experimental.pallas import tpu as pltpu

F32, BF16, I32, U32 = jnp.float32, jnp.bfloat16, jnp.int32, jnp.uint32

CHUNK = 64
SSM_GROUP = 16
SSM_STATE = 64
HEAD_DIM = 64
LEFT_CHUNKS = 8
BAND_ROWS = LEFT_CHUNKS * CHUNK
MAX_REL = 128
CONV_WIDTH = 31
MOE_GROUPS = 4
EXPERTS_PER_GROUP = 8
N_EXPERTS = MOE_GROUPS * EXPERTS_PER_GROUP
EPS = 1e-6
NEG_INF = -1e30

LANES = 128
SUBLANES = 8
VMEM_LIMIT_BYTES = 56 * 1024 * 1024

TM = 512
TM_MOE = 256
QBLK = 2 * CHUNK
KWIN = QBLK + BAND_ROWS
QSTEP = 8
D_MODEL = 1024
ROWS_O = D_MODEL // LANES
CONV_HALO = 32
TM_CONV = 512


def _cparams(n_grid):
    return pltpu.CompilerParams(dimension_semantics=("arbitrary",) * n_grid, vmem_limit_bytes=VMEM_LIMIT_BYTES)


def _sigmoid(x):
    return 1.0 / (1.0 + jnp.exp(-x))


def _rms(x, g):
    return x * lax.rsqrt(jnp.mean(x * x, axis=-1, keepdims=True) + EPS) * g


def _split_bf16(x):
    hi = lax.bitcast_convert_type(lax.bitcast_convert_type(x, U32) & jnp.uint32(0xFFFF0000), F32)
    return hi.astype(BF16), (x - hi).astype(BF16)


def _pair_specs(tm, width, n_prompt_blocks):
    return [pl.BlockSpec((tm, width), lambda i: (jnp.minimum(i, n_prompt_blocks - 1), 0)),
            pl.BlockSpec((tm, width), lambda i: (jnp.maximum(i - n_prompt_blocks, 0), 0))]


def _const_spec(shape):
    nd = len(shape)
    return pl.BlockSpec(shape, lambda *a: (0,) * nd)


def _inproj_body(xp_ref, xs_ref, g_ref, w_ref, qg_ref, kg_ref, seg_ref, u_ref, q_ref, k_ref, v_ref, *, n_pb, d_half):
    i = pl.program_id(0)
    x = jnp.where(i < n_pb, xp_ref[...], xs_ref[...])
    xn = _rms(x, g_ref[...]).astype(BF16)
    proj = jnp.dot(xn, w_ref[...], preferred_element_type=F32)
    u_ref[...] = proj[:, :d_half]
    v_ref[...] = proj[:, 3 * d_half:]

    def head_norm(t, gain):
        hi, lo = _split_bf16(t * t)
        msq = (jnp.dot(hi, seg_ref[...], preferred_element_type=F32)
               + jnp.dot(lo, seg_ref[...], preferred_element_type=F32))
        return t * lax.rsqrt(msq + EPS) * gain

    q_ref[...] = (head_norm(proj[:, d_half:2 * d_half], qg_ref[...]) * (HEAD_DIM ** -0.5)).astype(BF16)
    k_ref[...] = head_norm(proj[:, 2 * d_half:3 * d_half], kg_ref[...])


def _inproj(xp, xs, g, w, qg, kg):
    t_p, d = xp.shape
    t_s = xs.shape[0]
    d_half = w.shape[1] // 4
    n_pb = t_p // TM
    n_tot = (t_p + t_s) // TM
    heads = d_half // HEAD_DIM
    seg = jnp.kron(jnp.eye(heads, dtype=F32), jnp.full((HEAD_DIM, HEAD_DIM), 1.0 / HEAD_DIM, F32)).astype(BF16)
    out_f = jax.ShapeDtypeStruct((t_p + t_s, d_half), F32)
    out_b = jax.ShapeDtypeStruct((t_p + t_s, d_half), BF16)
    row = pl.BlockSpec((TM, d_half), lambda i: (i, 0))
    return pl.pallas_call(
        functools.partial(_inproj_body, n_pb=n_pb, d_half=d_half),
        grid=(n_tot,),
        in_specs=_pair_specs(TM, d, n_pb) + [_const_spec((1, d)), _const_spec(w.shape), _const_spec((1, d_half)),
                                             _const_spec((1, d_half)), _const_spec(seg.shape)],
        out_specs=[row, row, row, row],
        out_shape=[out_f, out_b, out_f, out_f],
        compiler_params=_cparams(1),
        name="inproj_even",
    )(xp, xs, g.reshape(1, d), w.astype(BF16), jnp.tile(qg, heads).reshape(1, d_half),
      jnp.tile(kg, heads).reshape(1, d_half), seg)


def _ssm_disc_body(are_ref, aim_ref, ldt_ref, bre_ref, bim_ref, barre_ref, barim_ref, bbre_ref, bbim_ref):
    lam_re, lam_im = are_ref[...], aim_ref[...]
    dt = jnp.exp(ldt_ref[...])
    mag = jnp.exp(lam_re * dt)
    bar_re, bar_im = mag * jnp.cos(lam_im * dt), mag * jnp.sin(lam_im * dt)
    den = lam_re * lam_re + lam_im * lam_im
    num_re = bar_re - 1.0
    coef_re = (num_re * lam_re + bar_im * lam_im) / den
    coef_im = (bar_im * lam_re - num_re * lam_im) / den
    br, bi = bre_ref[...], bim_ref[...]
    barre_ref[...] = bar_re
    barim_ref[...] = bar_im
    bbre_ref[...] = coef_re * br - coef_im * bi
    bbim_ref[...] = coef_re * bi + coef_im * br


def _ssm_discretize(a_re, a_im, log_dt, b_re, b_im):
    g, p = a_re.shape
    n = g * p
    col = jax.ShapeDtypeStruct((n, 1), F32)
    mat = jax.ShapeDtypeStruct((n, SSM_GROUP), F32)
    return pl.pallas_call(_ssm_disc_body, out_shape=[col, col, mat, mat], name="ssm_discretize")(
        a_re.reshape(n, 1), a_im.reshape(n, 1), jnp.repeat(log_dt, p).reshape(n, 1),
        b_re.reshape(n, SSM_GROUP), b_im.reshape(n, SSM_GROUP))


def _gelu_tanh(x):
    return 0.5 * x * (1.0 + jnp.tanh(math.sqrt(2.0 / math.pi) * (x + 0.044715 * (x * x * x))))


def _ssm_body(*refs, nb, lc, n_state_rows):
    u_refs = refs[:nb]
    h0_ref, lam_ref, bq_ref, cq_ref, d_ref, wglu_ref, bglu_ref, y_ref, ht_ref, s_ref, hcar_ref = refs[nb:]
    i = pl.program_id(0)
    half = n_state_rows // 2
    n_q = bq_ref.shape[0]
    tiles_q = n_state_rows // n_q
    k_q = bq_ref.shape[1]

    @pl.when(i == 0)
    def _():
        hcar_ref[...] = h0_ref[...]

    def tile_rows(tile):
        return slice(SUBLANES * tile, SUBLANES * (tile + 1))

    for b in range(nb):
        ub = u_refs[b][...].astype(BF16)
        for q in range(n_q):
            m = q % (n_q // 2)
            res = jnp.dot(ub[:, k_q * m:k_q * (m + 1)], bq_ref[q], preferred_element_type=F32)
            for c in range(tiles_q):
                s_ref[b, :, tile_rows(q * tiles_q + c), :] = (
                    res[:, LANES * c:LANES * (c + 1)].reshape(lc // SUBLANES, SUBLANES, LANES))

    lam_re, lam_im = lam_ref[:half, :], lam_ref[half:, :]

    def block(r, hs):
        hs = list(hs)
        for t in range(SUBLANES):
            for b in range(nb):
                bu = s_ref[b, r, pl.ds(t, n_state_rows, stride=SUBLANES), :]
                hr, hi = hs[2 * b], hs[2 * b + 1]
                nr = lam_re * hr - lam_im * hi + bu[:half]
                ni = lam_re * hi + lam_im * hr + bu[half:]
                s_ref[b, r, pl.ds(t, n_state_rows, stride=SUBLANES), :] = jnp.concatenate([nr, ni], axis=0)
                hs[2 * b], hs[2 * b + 1] = nr, ni
        return tuple(hs)

    init = []
    for b in range(nb):
        init += [hcar_ref[b, :half, :], hcar_ref[b, half:, :]]
    fin = lax.fori_loop(0, lc // SUBLANES, block, tuple(init))
    for b in range(nb):
        hb = jnp.concatenate([fin[2 * b], fin[2 * b + 1]], axis=0)
        hcar_ref[b] = hb
        ht_ref[b] = hb

    for b in range(nb):
        accs = []
        for m in range(n_q // 2):
            acc = None
            for q in (m, m + n_q // 2):
                tiles = [s_ref[b, :, tile_rows(q * tiles_q + c), :].reshape(lc, LANES) for c in range(tiles_q)]
                part = jnp.dot(jnp.concatenate(tiles, axis=1).astype(BF16), cq_ref[q], preferred_element_type=F32)
                acc = part if acc is None else acc + part
            accs.append(acc)
        y = jnp.concatenate(accs, axis=1) + d_ref[...] * u_refs[b][...]
        z = _gelu_tanh(y)
        gate = _sigmoid(jnp.dot(z.astype(BF16), wglu_ref[...], preferred_element_type=F32) + bglu_ref[...])
        y_ref[b] = (z * gate).astype(y_ref.dtype)


def _ssm(u, row0, nb, seq, lc, h0, lam, bbig, cbig, dvec, wglu, bglu):
    d_ssm = u.shape[1]
    n_state_rows = lam.shape[0]
    n_chunks = seq // lc
    blk0 = row0 // lc
    u_specs = [pl.BlockSpec((lc, d_ssm), functools.partial(lambda i, b: (blk0 + b * n_chunks + i, 0), b=b))
               for b in range(nb)]
    return pl.pallas_call(
        functools.partial(_ssm_body, nb=nb, lc=lc, n_state_rows=n_state_rows),
        grid=(n_chunks,),
        in_specs=u_specs + [_const_spec(h0.shape), _const_spec(lam.shape), _const_spec(bbig.shape),
                            _const_spec(cbig.shape), _const_spec(dvec.shape), _const_spec(wglu.shape),
                            _const_spec(bglu.shape)],
        out_specs=[pl.BlockSpec((nb, lc, d_ssm), lambda i: (0, i, 0)), _const_spec(h0.shape)],
        out_shape=[jax.ShapeDtypeStruct((nb, seq, d_ssm), BF16), jax.ShapeDtypeStruct(h0.shape, F32)],
        scratch_shapes=[pltpu.VMEM((nb, lc // SUBLANES, SUBLANES * n_state_rows, LANES), F32),
                        pltpu.VMEM(h0.shape, F32)],
        compiler_params=_cparams(1),
        name=f"ssm_b{nb}",
    )(*([u] * nb), h0, lam, bbig, cbig, dvec, wglu, bglu)


def _bias_rows(m_ref, lead, base, n_rows, n_cols):
    for r in range(n_rows):
        m_ref[lead, r:r + 1, :] = base[:, QBLK - 1 - r:QBLK - 1 - r + n_cols]


def _attn_prompt_body(q_ref, k_ref, v_ref, base_ref, o_ref, kb_ref, vb_ref, m_ref):
    j = pl.program_id(2)

    @pl.when(j == 0)
    def _():
        kb_ref[:BAND_ROWS, :] = jnp.zeros((BAND_ROWS, LANES), BF16)
        vb_ref[:BAND_ROWS, :] = jnp.zeros((BAND_ROWS, LANES), BF16)
        kb_ref[BAND_ROWS:, :] = k_ref[...].astype(BF16)
        vb_ref[BAND_ROWS:, :] = v_ref[...].astype(BF16)
        rq = lax.broadcasted_iota(I32, (QBLK, KWIN), 0) // CHUNK
        ck = lax.broadcasted_iota(I32, (QBLK, KWIN), 1) // CHUNK
        in_band = (ck >= rq) & (ck <= rq + LEFT_CHUNKS)
        for a in range(2):
            _bias_rows(m_ref, a, base_ref[0, a:a + 1, :], QBLK, KWIN)
            m_ref[a] = jnp.where(in_band, m_ref[a], NEG_INF)

    start = pl.multiple_of(j * (QSTEP * QBLK), QSTEP * QBLK)
    lane = lax.broadcasted_iota(I32, (QBLK, LANES), 1)
    col = lax.broadcasted_iota(I32, (2 * QBLK, KWIN), 1)
    bias = m_ref[...].reshape(2 * QBLK, KWIN)
    for c in range(QSTEP):
        q2 = q_ref[QBLK * c:QBLK * (c + 1), :]
        kw = kb_ref[pl.ds(start + QBLK * c, KWIN), :]
        vw = vb_ref[pl.ds(start + QBLK * c, KWIN), :]
        first_valid = BAND_ROWS - (j * QSTEP + c) * QBLK
        zero = jnp.zeros_like(q2)
        qs = jnp.concatenate([jnp.where(lane < HEAD_DIM, q2, zero), jnp.where(lane >= HEAD_DIM, q2, zero)], axis=0)
        s = lax.dot_general(qs, kw, (((1,), (1,)), ((), ())), preferred_element_type=F32)
        s = jnp.where(col >= first_valid, s + bias, NEG_INF)
        p = jnp.exp(s - jnp.max(s, axis=1, keepdims=True))
        o = jnp.dot(p.astype(BF16), vw, preferred_element_type=F32) * (1.0 / jnp.sum(p, axis=1, keepdims=True))
        o_ref[QBLK * c:QBLK * (c + 1), :] = jnp.where(lane < HEAD_DIM, o[:QBLK], o[QBLK:]).astype(o_ref.dtype)


def _attn_prompt(q, k, v, base2, nb, seq):
    d_attn = q.shape[1]
    pairs = d_attn // LANES
    nq = seq // (QSTEP * QBLK)
    kv_spec = pl.BlockSpec((seq, LANES), lambda b, p, j: (b, p))
    return pl.pallas_call(
        _attn_prompt_body,
        grid=(nb, pairs, nq),
        in_specs=[pl.BlockSpec((QSTEP * QBLK, LANES), lambda b, p, j: (b * nq + j, p)), kv_spec, kv_spec,
                  pl.BlockSpec((1, 2, base2.shape[2]), lambda b, p, j: (p, 0, 0))],
        out_specs=pl.BlockSpec((QSTEP * QBLK, LANES), lambda b, p, j: (b * nq + j, p)),
        out_shape=jax.ShapeDtypeStruct((nb * seq, d_attn), BF16),
        scratch_shapes=[pltpu.VMEM((seq + BAND_ROWS, LANES), BF16), pltpu.VMEM((seq + BAND_ROWS, LANES), BF16),
                        pltpu.VMEM((2, QBLK, KWIN), F32)],
        compiler_params=_cparams(3),
        name="attn_prompt",
    )(q, k, v, base2)


def _attn_sample_body(q_ref, k_ref, v_ref, ck_ref, cv_ref, base_ref, o_ref, m_ref, *, heads, n_cache):
    @pl.when(pl.program_id(0) == 0)
    def _():
        for h in range(heads):
            _bias_rows(m_ref, h, base_ref[h:h + 1, :], CHUNK, n_cache + CHUNK)

    q = q_ref[...]
    k = k_ref[...].astype(BF16)
    v = v_ref[...].astype(BF16)
    nt = (((1,), (1,)), ((), ()))
    outs = []
    for h in range(heads):
        sl = slice(HEAD_DIM * h, HEAD_DIM * (h + 1))
        qh = q[:, sl]
        s1 = lax.dot_general(qh, ck_ref[0, h].astype(BF16), nt, preferred_element_type=F32) + m_ref[h, :, :n_cache]
        s2 = lax.dot_general(qh, k[:, sl], nt, preferred_element_type=F32) + m_ref[h, :, n_cache:]
        mx = jnp.maximum(jnp.max(s1, axis=1, keepdims=True), jnp.max(s2, axis=1, keepdims=True))
        p1, p2 = jnp.exp(s1 - mx), jnp.exp(s2 - mx)
        den = jnp.sum(p1, axis=1, keepdims=True) + jnp.sum(p2, axis=1, keepdims=True)
        o = (jnp.dot(p1.astype(BF16), cv_ref[0, h].astype(BF16), preferred_element_type=F32)
             + jnp.dot(p2.astype(BF16), v[:, sl], preferred_element_type=F32))
        outs.append(o * (1.0 / den))
    o_ref[...] = jnp.concatenate(outs, axis=1).astype(o_ref.dtype)


def _attn_sample(q, k, v, cache_k, cache_v, base, row0):
    nb, heads, n_cache, _ = cache_k.shape
    d_attn = q.shape[1]
    blk0 = row0 // CHUNK
    row = pl.BlockSpec((CHUNK, d_attn), lambda b: (blk0 + b, 0))
    cache = pl.BlockSpec((1, heads, n_cache, HEAD_DIM), lambda b: (b, 0, 0, 0))
    return pl.pallas_call(
        functools.partial(_attn_sample_body, heads=heads, n_cache=n_cache),
        grid=(nb,),
        in_specs=[row, row, row, cache, cache, _const_spec(base.shape)],
        out_specs=pl.BlockSpec((CHUNK, d_attn), lambda b: (b, 0)),
        out_shape=jax.ShapeDtypeStruct((nb * CHUNK, d_attn), BF16),
        scratch_shapes=[pltpu.VMEM((heads, CHUNK, n_cache + CHUNK), F32)],
        compiler_params=_cparams(1),
        name="attn_sample",
    )(q, k, v, cache_k, cache_v, base)


def _bias_base(table):
    heads = table.shape[0]
    far = jnp.broadcast_to(table[:, 2 * MAX_REL:], (heads, KWIN - 1 - MAX_REL + 1))
    near = table[:, 2 * MAX_REL - 1:0:-1]
    base = jnp.concatenate([far, near], axis=1)
    return jnp.pad(base, ((0, 0), (0, KWIN + QBLK - base.shape[1])))


def _post_body(*refs, n_x, n_y, n_pb):
    i = pl.program_id(0)
    pos = 0

    def take(n):
        nonlocal pos
        out = refs[pos:pos + n]
        pos += n
        return out

    def merged(pair):
        if len(pair) == 1:
            return pair[0][...]
        return jnp.where(i < n_pb, pair[0][...], pair[1][...])

    x_refs = take(n_x)
    y_pairs = [take(2) for _ in range(n_y)]
    w_refs = take(n_y)
    g_ref, wr_ref, br_ref, tril_ref, cin_ref = take(5)
    xo_ref, mi_ref, mf_ref, cnt_ref = take(4)
    (carry_ref,) = take(1)

    x = merged(x_refs)
    for pair, w_ref in zip(y_pairs, w_refs):
        x = x + jnp.dot(merged(pair).astype(BF16), w_ref[...], preferred_element_type=F32)
    xo_ref[...] = x

    xn = _rms(x, g_ref[...])
    xh, xl = _split_bf16(xn)
    ph = jnp.dot(xh, wr_ref[...], preferred_element_type=F32)
    pl_ = jnp.dot(xl, wr_ref[...], preferred_element_type=F32)
    logits = (ph[:, :LANES] + pl_[:, :LANES] + ph[:, LANES:] + pl_[:, LANES:]) + br_ref[...]
    lane = lax.broadcasted_iota(I32, logits.shape, 1)

    def first_argmax(vals):
        top = jnp.max(vals, axis=1, keepdims=True)
        return top, jnp.min(jnp.where(vals == top, lane, LANES), axis=1, keepdims=True)

    is_group = lane < MOE_GROUPS
    g_top, g_idx = first_argmax(jnp.where(is_group, logits, -jnp.inf))
    g_w = 1.0 / jnp.sum(jnp.where(is_group, jnp.exp(logits - g_top), 0.0), axis=1, keepdims=True)
    first = MOE_GROUPS + EXPERTS_PER_GROUP * g_idx
    e_log = jnp.where((lane >= first) & (lane < first + EXPERTS_PER_GROUP), logits, -jnp.inf)
    v1, l1 = first_argmax(e_log)
    v2, l2 = first_argmax(jnp.where(lane == l1, -jnp.inf, e_log))
    t = jnp.exp(v2 - v1)
    w1 = 1.0 / (1.0 + t)
    w2 = t * w1
    e1, e2 = l1 - MOE_GROUPS, l2 - MOE_GROUPS

    @pl.when(i == 0)
    def _():
        carry_ref[...] = cin_ref[...]

    hit1, hit2 = lane == e1, lane == e2
    oh = jnp.where(hit1 | hit2, 1.0, 0.0)
    before = jnp.dot(tril_ref[...], oh.astype(BF16), preferred_element_type=F32) + carry_ref[...]
    r1 = jnp.sum(jnp.where(hit1, before, 0.0), axis=1, keepdims=True).astype(I32)
    r2 = jnp.sum(jnp.where(hit2, before, 0.0), axis=1, keepdims=True).astype(I32)
    carry_ref[...] = carry_ref[...] + jnp.sum(oh, axis=0, keepdims=True)
    cnt_ref[...] = carry_ref[...]

    mi_ref[...] = jnp.where(lane == 0, e1, jnp.where(lane == 1, e2, jnp.where(lane == 2, r1, r2)))
    mf_ref[...] = jnp.where(lane == 0, g_w * w1, g_w * w2)


def _post_mixer(x_src, y_pairs, ws, g, router, tril, t_p, t_s):
    wr, br = router
    d = ws[0].shape[1]
    n_pb = t_p // TM
    n_tot = (t_p + t_s) // TM
    row = lambda width: pl.BlockSpec((TM, width), lambda i: (i, 0))
    in_specs, args = [], []
    if len(x_src) == 1:
        in_specs.append(row(d))
    else:
        in_specs += _pair_specs(TM, d, n_pb)
    args += list(x_src)
    for yp, ys in y_pairs:
        in_specs += _pair_specs(TM, yp.shape[1], n_pb)
        args += [yp, ys]
    consts = list(ws) + [g.reshape(1, d), wr, br, tril, jnp.zeros((1, LANES), F32)]
    in_specs += [_const_spec(c.shape) for c in consts]
    t = t_p + t_s
    return pl.pallas_call(
        functools.partial(_post_body, n_x=len(x_src), n_y=len(y_pairs), n_pb=n_pb),
        grid=(n_tot,),
        in_specs=in_specs,
        out_specs=[row(d), row(LANES), row(LANES), _const_spec((1, LANES))],
        out_shape=[jax.ShapeDtypeStruct((t, d), F32), jax.ShapeDtypeStruct((t, LANES), I32), jax.ShapeDtypeStruct((t, LANES), F32),
                   jax.ShapeDtypeStruct((1, LANES), F32)],
        scratch_shapes=[pltpu.VMEM((1, LANES), F32)],
        compiler_params=_cparams(1),
        name="post_mixer",
    )(*args, *consts)


def _router_weights(gate_w, gate_b, router_w, router_b):
    d = gate_w.shape[0]
    w = jnp.concatenate([gate_w, router_w.transpose(1, 0, 2).reshape(d, N_EXPERTS)], axis=1)
    b = jnp.concatenate([gate_b, router_b.reshape(N_EXPERTS)])
    w = jnp.pad(w, ((0, 0), (0, LANES - w.shape[1])))
    b = jnp.pad(b, (0, LANES - b.shape[0])).reshape(1, LANES)
    hi, lo = _split_bf16(w)
    return jnp.concatenate([hi, lo], axis=1), b


def _dispatch_body(slots_ref, pad_pos_ref, pad_n_ref, x_ref, g_ref, xs_ref, stage_ref, zero_ref, sem, *, tm, tile):
    i = pl.program_id(0)

    def zero_copy(pos, n):
        return pltpu.make_async_copy(zero_ref.at[pl.ds(0, n)], xs_ref.at[pl.ds(pos, n)], sem.at[1])

    @pl.when(i == 0)
    def _():
        zero_ref[...] = jnp.zeros(zero_ref.shape, zero_ref.dtype)
        for e in range(N_EXPERTS):
            pos, n = pad_pos_ref[e], pad_n_ref[e]
            bit = tile // 2
            while bit >= 1:
                @pl.when((n & bit) != 0)
                def _(pos=pos, bit=bit):
                    cp = zero_copy(pos, bit)
                    cp.start()
                    cp.wait()
                pos = pos + (n & bit)
                bit //= 2
        used_tiles = (pad_pos_ref[N_EXPERTS - 1] + pad_n_ref[N_EXPERTS - 1]) // tile
        all_tiles = xs_ref.shape[0] // tile
        lax.fori_loop(used_tiles, all_tiles, lambda tl, c: (zero_copy(tl * tile, tile).start(), c)[1], 0)
        lax.fori_loop(used_tiles, all_tiles, lambda tl, c: (zero_copy(tl * tile, tile).wait(), c)[1], 0)

    xn = _rms(x_ref[...], g_ref[...])
    for c in range(ROWS_O):
        stage_ref[pl.ds(c, tm, stride=ROWS_O), :] = xn[:, LANES * c:LANES * (c + 1)]

    def start(r, c):
        src = stage_ref.at[pl.ds(pl.multiple_of(r * ROWS_O, ROWS_O), ROWS_O)]
        for k in range(2):
            slot = slots_ref[2 * (i * tm + r) + k]
            pltpu.make_async_copy(src, xs_ref.at[slot], sem.at[0]).start(priority=k)
        return c

    lax.fori_loop(0, tm, start, 0, unroll=8)
    pltpu.make_async_copy(xs_ref.at[pl.ds(0, 2 * tm)], xs_ref.at[pl.ds(0, 2 * tm)], sem.at[0]).wait()


def _dispatch(x, g, slots, pad_pos, pad_n, n_slots):
    t, d = x.shape
    slab = (ROWS_O, LANES)
    grid_spec = pltpu.PrefetchScalarGridSpec(
        num_scalar_prefetch=3,
        grid=(t // TM,),
        in_specs=[pl.BlockSpec((TM, d), lambda i, *_: (i, 0)), pl.BlockSpec((1, d), lambda i, *_: (0, 0))],
        out_specs=pl.BlockSpec(memory_space=pl.ANY),
        scratch_shapes=[pltpu.VMEM((TM * ROWS_O, LANES), F32), pltpu.VMEM((TM_MOE,) + slab, F32),
                        pltpu.SemaphoreType.DMA((2,))],
    )
    return pl.pallas_call(
        functools.partial(_dispatch_body, tm=TM, tile=TM_MOE),
        grid_spec=grid_spec,
        out_shape=jax.ShapeDtypeStruct((n_slots,) + slab, F32),
        compiler_params=_cparams(1),
        name="moe_dispatch",
    )(slots, pad_pos, pad_n, x, g.reshape(1, d))


def _moe_body(te_ref, nt_ref, x_ref, wgu_ref, wd_ref, o_ref, wgu_bf, wd_bf, *, tm):
    i = pl.program_id(0)
    prev = te_ref[jnp.maximum(i - 1, 0)]

    @pl.when((i == 0) | (te_ref[i] != prev))
    def _():
        wgu_bf[...] = wgu_ref[0, 0].astype(BF16)
        wd_bf[...] = wd_ref[0, 0].astype(BF16)

    @pl.when(i < nt_ref[0])
    def _():
        x = jnp.concatenate([x_ref[pl.ds(c, tm, stride=ROWS_O), :] for c in range(ROWS_O)], axis=1)
        h = jnp.dot(x.astype(BF16), wgu_bf[...], preferred_element_type=F32)
        d_exp = h.shape[1] // 2
        gt, up = h[:, :d_exp], h[:, d_exp:]
        act = (gt * _sigmoid(gt) * up).astype(BF16)
        out = jnp.dot(act, wd_bf[...], preferred_element_type=F32)
        for c in range(ROWS_O):
            o_ref[pl.ds(c, tm, stride=ROWS_O), :] = out[:, LANES * c:LANES * (c + 1)]

    @pl.when(i >= nt_ref[0])
    def _():
        o_ref[...] = jnp.zeros(o_ref.shape, o_ref.dtype)


def _moe_experts(xsorted, tile_expert, n_tiles, wgu, wd, layer):
    n_slots = xsorted.shape[0] // ROWS_O
    n_max = n_slots // TM_MOE
    grid_spec = pltpu.PrefetchScalarGridSpec(
        num_scalar_prefetch=2,
        grid=(n_max,),
        in_specs=[pl.BlockSpec((TM_MOE * ROWS_O, LANES), lambda i, te, nt: (jnp.minimum(i, nt[0] - 1), 0)),
                  pl.BlockSpec((1, 1) + wgu.shape[2:], lambda i, te, nt: (layer, te[i], 0, 0)),
                  pl.BlockSpec((1, 1) + wd.shape[2:], lambda i, te, nt: (layer, te[i], 0, 0))],
        out_specs=pl.BlockSpec((TM_MOE * ROWS_O, LANES), lambda i, te, nt: (i, 0)),
        scratch_shapes=[pltpu.VMEM(wgu.shape[2:], BF16), pltpu.VMEM(wd.shape[2:], BF16)],
    )
    return pl.pallas_call(
        functools.partial(_moe_body, tm=TM_MOE),
        grid_spec=grid_spec,
        out_shape=jax.ShapeDtypeStruct((n_slots * ROWS_O, LANES), F32),
        compiler_params=_cparams(1),
        name="moe_experts",
    )(tile_expert, n_tiles, xsorted, wgu, wd)


def _combine_body(slots_ref, x_ref, mf_ref, os_ref, *rest, tm, n_steps, n_pb):
    out_refs, (buf_ref, sem) = rest[:-2], rest[-2:]
    i = pl.program_id(0)

    def issue(tile, par):
        def body(r, c):
            for k in range(2):
                slot = slots_ref[2 * (tile * tm + r) + k]
                dst = buf_ref.at[par, k, pl.ds(pl.multiple_of(r * ROWS_O, ROWS_O), ROWS_O)]
                pltpu.make_async_copy(os_ref.at[slot], dst, sem.at[par]).start(priority=k)
            return c
        lax.fori_loop(0, tm, body, 0, unroll=8)

    @pl.when(i == 0)
    def _():
        issue(0, 0)

    @pl.when(i + 1 < n_steps)
    def _():
        issue(i + 1, (i + 1) % 2)

    par = i % 2
    pltpu.make_async_copy(buf_ref.at[par], buf_ref.at[par], sem.at[par]).wait()

    def write(out_ref):
        mf = mf_ref[...]
        g1, g2 = mf[:, 0:1], mf[:, 1:2]
        for c in range(ROWS_O):
            cols = slice(LANES * c, LANES * (c + 1))
            piece = lambda k: buf_ref[par, k, pl.ds(c, tm, stride=ROWS_O), :]
            out_ref[:, cols] = x_ref[:, cols] + g1 * piece(0) + g2 * piece(1)

    if len(out_refs) == 1:
        write(out_refs[0])
    else:
        pl.when(i < n_pb)(lambda: write(out_refs[0]))
        pl.when(i >= n_pb)(lambda: write(out_refs[1]))


def _combine(x, meta_f, slots, osorted, t_p, split):
    t, d = x.shape
    n_steps = t // TM
    n_pb = t_p // TM
    row = lambda width: pl.BlockSpec((TM, width), lambda i, *_: (i, 0))
    if split:
        out_specs = [pl.BlockSpec((TM, d), lambda i, *_: (jnp.minimum(i, n_pb - 1), 0)),
                     pl.BlockSpec((TM, d), lambda i, *_: (jnp.maximum(i - n_pb, 0), 0))]
        out_shape = [jax.ShapeDtypeStruct((t_p, d), F32), jax.ShapeDtypeStruct((t - t_p, d), F32)]
    else:
        out_specs = [row(d)]
        out_shape = [jax.ShapeDtypeStruct((t, d), F32)]
    grid_spec = pltpu.PrefetchScalarGridSpec(
        num_scalar_prefetch=1,
        grid=(n_steps,),
        in_specs=[row(d), row(LANES), pl.BlockSpec(memory_space=pl.ANY)],
        out_specs=out_specs,
        scratch_shapes=[pltpu.VMEM((2, 2, TM * ROWS_O, LANES), F32), pltpu.SemaphoreType.DMA((2,))],
    )
    return pl.pallas_call(
        functools.partial(_combine_body, tm=TM, n_steps=n_steps, n_pb=n_pb),
        grid_spec=grid_spec,
        out_shape=out_shape,
        compiler_params=_cparams(1),
        name="moe_combine",
    )(slots, x, meta_f, osorted)


def _moe_layer(x, g, meta_i, meta_f, counts, wgu, wd, layer, t_p, split):
    t = x.shape[0]
    n_max = (2 * t + N_EXPERTS * (TM_MOE - 1)) // TM_MOE
    cnt = counts[0, :N_EXPERTS].astype(I32)
    padded = (cnt + TM_MOE - 1) // TM_MOE * TM_MOE
    ends = jnp.cumsum(padded)
    offs = ends - padded
    experts = jnp.arange(N_EXPERTS, dtype=I32)
    first_slot = jnp.sum(jnp.where(meta_i[:, 0:2, None] == experts, offs, 0), axis=-1)
    slots = (first_slot + meta_i[:, 2:4]).reshape(-1)
    n_tiles = (ends[-1] // TM_MOE).reshape(1)
    tile_start = jnp.arange(n_max, dtype=I32) * TM_MOE
    tile_expert = jnp.minimum(jnp.sum((ends[None, :] <= tile_start[:, None]).astype(I32), axis=1), N_EXPERTS - 1)
    n_slots = n_max * TM_MOE
    xsorted = _dispatch(x, g, slots, offs + cnt, padded - cnt, n_slots)
    osorted = _moe_experts(xsorted.reshape(n_slots * ROWS_O, LANES), tile_expert, n_tiles, wgu, wd, layer)
    return _combine(x, meta_f, slots, osorted.reshape(n_slots, ROWS_O, LANES), t_p, split)


def _conv_in_body(x_ref, g_ref, w_ref, u_ref):
    xn = _rms(x_ref[...], g_ref[...]).astype(BF16)
    h = jnp.dot(xn, w_ref[...], preferred_element_type=F32)
    c = h.shape[1] // 2
    u_ref[...] = h[:, :c] * _sigmoid(h[:, c:])


def _conv_in(x, g, w):
    t, d = x.shape
    c = w.shape[1] // 2
    return pl.pallas_call(
        _conv_in_body,
        grid=(t // TM,),
        in_specs=[pl.BlockSpec((TM, d), lambda i: (i, 0)), _const_spec((1, d)), _const_spec(w.shape)],
        out_specs=pl.BlockSpec((TM, c), lambda i: (i, 0)),
        out_shape=jax.ShapeDtypeStruct((t, c), F32),
        compiler_params=_cparams(1),
        name="conv_in",
    )(x, g.reshape(1, d), w.astype(BF16))


def _dwconv_body(cur_ref, prev_ref, left_ref, w_ref, b_ref, g_ref, beta_ref, o_ref, win_ref, y_ref, *, tm):
    i = pl.program_id(1)
    n_lt = cur_ref.shape[1] // LANES
    halo = jnp.where(i == 0, left_ref[0], prev_ref[...])
    cur = cur_ref[...]
    for c in range(n_lt):
        cols = slice(LANES * c, LANES * (c + 1))
        win_ref[pl.ds(c, CONV_HALO, stride=n_lt), :] = halo[:, cols]
        win_ref[pl.ds(CONV_HALO * n_lt + c, tm, stride=n_lt), :] = cur[:, cols]
    first_tap = CONV_HALO - (CONV_WIDTH - 1)
    taps = [w_ref[j] for j in range(CONV_WIDTH)]

    def block(rb, carry):
        for t in range(SUBLANES):
            frame = rb * SUBLANES + t
            acc = None
            for j in range(CONV_WIDTH):
                row = pl.multiple_of((frame + first_tap + j) * n_lt, n_lt)
                term = taps[j] * win_ref[pl.ds(row, n_lt), :]
                acc = term if acc is None else acc + term
            y_ref[pl.ds(pl.multiple_of(frame * n_lt, n_lt), n_lt), :] = acc
        return carry

    lax.fori_loop(0, tm // SUBLANES, block, 0)
    y = jnp.concatenate([y_ref[pl.ds(c, tm, stride=n_lt), :] for c in range(n_lt)], axis=1)
    y = y + b_ref[...]
    mu = jnp.mean(y, axis=1, keepdims=True)
    yc = y - mu
    yn = yc * lax.rsqrt(jnp.mean(yc * yc, axis=1, keepdims=True) + EPS) * g_ref[...] + beta_ref[...]
    o_ref[...] = (yn * _sigmoid(yn)).astype(BF16)


def _dwconv(u, left, dw_w, dw_b, ln_g, ln_b, row0, nb, seq, tm):
    c = u.shape[1]
    n_t = seq // tm
    blk0 = row0 // tm
    halo_per_tile = tm // CONV_HALO
    vec = lambda a: a.reshape(1, c)
    w = dw_w.reshape(CONV_WIDTH, c // LANES, LANES)
    return pl.pallas_call(
        functools.partial(_dwconv_body, tm=tm),
        grid=(nb, n_t),
        in_specs=[pl.BlockSpec((tm, c), lambda b, i: (blk0 + b * n_t + i, 0)),
                  pl.BlockSpec((CONV_HALO, c),
                               lambda b, i: (jnp.maximum((blk0 + b * n_t + i) * halo_per_tile - 1, 0), 0)),
                  pl.BlockSpec((1, CONV_HALO, c), lambda b, i: (b, 0, 0)),
                  _const_spec(w.shape), _const_spec((1, c)), _const_spec((1, c)), _const_spec((1, c))],
        out_specs=pl.BlockSpec((tm, c), lambda b, i: (b * n_t + i, 0)),
        out_shape=jax.ShapeDtypeStruct((nb * seq, c), BF16),
        scratch_shapes=[pltpu.VMEM(((tm + CONV_HALO) * (c // LANES), LANES), F32),
                        pltpu.VMEM((tm * (c // LANES), LANES), F32)],
        compiler_params=_cparams(2),
        name=f"dwconv_b{nb}",
    )(u, u, left, w, vec(dw_b), vec(ln_g), vec(ln_b))


def _block_diag(blocks):
    g, a, b = blocks.shape
    return jnp.einsum("gab,gh->gahb", blocks, jnp.eye(g, dtype=blocks.dtype)).reshape(g * a, g * b)


def _state_rows(re, im):
    nb = re.shape[0]
    return jnp.concatenate([re.reshape(nb, -1, LANES), im.reshape(nb, -1, LANES)], axis=1)


def kernel(x_prompt, x_sample, cache_attn_k, cache_attn_v, state_ssm_re, state_ssm_im, cache_conv, norm_mixer, norm_ffn, w_in_even, w_out_even, ssm_a_re, ssm_a_im, ssm_log_dt, ssm_b_re, ssm_b_im, ssm_c_re, ssm_c_im, ssm_d, ssm_glu_w, ssm_glu_b, attn_q_gain, attn_k_gain, attn_rel_bias, conv_w_in, conv_dw_w, conv_dw_b, conv_ln_g, conv_ln_b, conv_w_out, moe_gate_w, moe_gate_b, moe_router_w, moe_router_b, moe_w_gate_up, moe_w_down):
    nb_p, seq_p, d = x_prompt.shape
    nb_s, seq_s, _ = x_sample.shape
    t_p, t_s = nb_p * seq_p, nb_s * seq_s
    xp, xs = x_prompt.reshape(t_p, d), x_sample.reshape(t_s, d)
    depth = norm_mixer.shape[0]
    tril = jnp.tril(jnp.ones((TM, TM), F32), -1).astype(BF16)

    ks_p, vs_p, hs_p, ks_s, vs_s, hs_s, conv_p, conv_s = [], [], [], [], [], [], [], []
    x_src = (xp, xs)
    for layer in range(depth):
        j = layer // 2
        if layer % 2 == 0:
            u, q, k, v = _inproj(*x_src, norm_mixer[layer], w_in_even[j], attn_q_gain[j], attn_k_gain[j])
            d_ssm = u.shape[1]
            groups, n_state = ssm_a_re[j].shape
            bar_re, bar_im, bb_re, bb_im = _ssm_discretize(ssm_a_re[j], ssm_a_im[j], ssm_log_dt[j],
                                                           ssm_b_re[j], ssm_b_im[j])
            lam = jnp.concatenate([bar_re.reshape(-1, LANES), bar_im.reshape(-1, LANES)], axis=0)
            to_state = lambda m: _block_diag(m.reshape(groups, n_state, SSM_GROUP).transpose(0, 2, 1))
            bbig = jnp.concatenate([to_state(bb_re), to_state(bb_im)], axis=1).astype(BF16)
            cbig = jnp.concatenate([_block_diag(ssm_c_re[j].transpose(0, 2, 1)),
                                    _block_diag(-ssm_c_im[j].transpose(0, 2, 1))], axis=0).astype(BF16)
            n_q, k_q = 4, d_ssm // 2
            w_q = bbig.shape[1] // n_q
            blocks = lambda q: (slice(k_q * (q % 2), k_q * (q % 2 + 1)), slice(w_q * q, w_q * (q + 1)))
            bbig = jnp.stack([bbig[blocks(q)[0], blocks(q)[1]] for q in range(n_q)])
            cbig = jnp.stack([cbig[blocks(q)[1], blocks(q)[0]] for q in range(n_q)])
            ssm_args = (lam, bbig, cbig, ssm_d[j].reshape(1, d_ssm), _block_diag(ssm_glu_w[j]).astype(BF16),
                        ssm_glu_b[j].reshape(1, d_ssm))
            zero_state = jnp.zeros((nb_p, groups, n_state), F32)
            y_ssm_p, h_p = _ssm(u, 0, nb_p, seq_p, 256, _state_rows(zero_state, zero_state), *ssm_args)
            y_ssm_s, h_s = _ssm(u, t_p, nb_s, seq_s, seq_s, _state_rows(state_ssm_re[j], state_ssm_im[j]), *ssm_args)
            base = _bias_base(attn_rel_bias[j])
            heads = base.shape[0]
            y_att_p = _attn_prompt(q, k, v, base.reshape(heads // 2, 2, -1), nb_p, seq_p)
            y_att_s = _attn_sample(q, k, v, cache_attn_k[j], cache_attn_v[j], base, t_p)
            w_out = w_out_even[j].astype(BF16)
            y_pairs = [(y_ssm_p.reshape(t_p, d_ssm), y_ssm_s.reshape(t_s, d_ssm)), (y_att_p, y_att_s)]
            ws = [w_out[:d_ssm], w_out[d_ssm:]]

            keep = min(BAND_ROWS, seq_p)
            heads_of = lambda a, nb: a.reshape(nb, -1, heads, HEAD_DIM).transpose(0, 2, 1, 3)
            tail_p = lambda a: jnp.concatenate([a[(b + 1) * seq_p - keep:(b + 1) * seq_p] for b in range(nb_p)])
            ks_p.append(heads_of(tail_p(k), nb_p))
            vs_p.append(heads_of(tail_p(v), nb_p))
            ks_s.append(heads_of(k[t_p:], nb_s))
            vs_s.append(heads_of(v[t_p:], nb_s))
            half = h_p.shape[1] // 2
            hs_p.append((h_p[:, :half].reshape(nb_p, groups, n_state), h_p[:, half:].reshape(nb_p, groups, n_state)))
            hs_s.append((h_s[:, :half].reshape(nb_s, groups, n_state), h_s[:, half:].reshape(nb_s, groups, n_state)))
        else:
            (x_merged,) = x_src
            uc = _conv_in(x_merged, norm_mixer[layer], conv_w_in[j])
            c = uc.shape[1]
            conv_args = (conv_dw_w[j], conv_dw_b[j], conv_ln_g[j], conv_ln_b[j])
            left_p = jnp.zeros((nb_p, CONV_HALO, c), F32)
            left_s = jnp.pad(cache_conv[j], ((0, 0), (CONV_HALO - (CONV_WIDTH - 1), 0), (0, 0)))
            act_p = _dwconv(uc, left_p, *conv_args, 0, nb_p, seq_p, TM_CONV)
            act_s = _dwconv(uc, left_s, *conv_args, t_p, nb_s, seq_s, seq_s)
            y_pairs = [(act_p, act_s)]
            ws = [conv_w_out[j].astype(BF16)]
            keep = CONV_WIDTH - 1
            conv_p.append(jnp.stack([uc[(b + 1) * seq_p - keep:(b + 1) * seq_p] for b in range(nb_p)]))
            ext_s = jnp.concatenate([cache_conv[j], uc[t_p:].reshape(nb_s, seq_s, c)], axis=1)
            conv_s.append(ext_s[:, ext_s.shape[1] - keep:])

        router = _router_weights(moe_gate_w[layer], moe_gate_b[layer], moe_router_w[layer], moe_router_b[layer])
        x_new, meta_i, meta_f, counts = _post_mixer(x_src, y_pairs, ws, norm_ffn[layer], router, tril, t_p, t_s)
        last = layer == depth - 1
        out = _moe_layer(x_new, norm_ffn[layer], meta_i, meta_f, counts, moe_w_gate_up, moe_w_down, layer, t_p, last)
        x_src = tuple(out)

    y_prompt, y_sample = x_src
    stack_c = lambda parts, idx: jnp.stack([p[idx] for p in parts])
    return (y_prompt.reshape(nb_p, seq_p, d), y_sample.reshape(nb_s, seq_s, d),
            jnp.stack(ks_p), jnp.stack(vs_p), stack_c(hs_p, 0), stack_c(hs_p, 1), jnp.stack(conv_p),
            jnp.stack(ks_s), jnp.stack(vs_s), stack_c(hs_s, 0), stack_c(hs_s, 1), jnp.stack(conv_s))
```

```python
import functools
import math

import jax
import jax.numpy as jnp
from jax import lax
from jax.experimental import pallas as pl
from jax.experimental.pallas import tpu as pltpu

F32, BF16, I32, U32 = jnp.float32, jnp.bfloat16, jnp.int32, jnp.uint32

CHUNK = 64
SSM_GROUP = 16
SSM_STATE = 64
HEAD_DIM = 64
LEFT_CHUNKS = 8
BAND_ROWS = LEFT_CHUNKS * CHUNK
MAX_REL = 128
CONV_WIDTH = 31
MOE_GROUPS = 4
EXPERTS_PER_GROUP = 8
N_EXPERTS = MOE_GROUPS * EXPERTS_PER_GROUP
EPS = 1e-6
NEG_INF = -1e30

LANES = 128
SUBLANES = 8
VMEM_LIMIT_BYTES = 56 * 1024 * 1024

TM = 512
TM_MOE = 512
QBLK = 2 * CHUNK
KWIN = QBLK + BAND_ROWS
QSTEP = 16
SSM_CHUNK = 512
D_MODEL = 1024
ROWS_O = D_MODEL // LANES
CONV_HALO = 32
TM_CONV = 512


def _cparams(n_grid):
    return pltpu.CompilerParams(dimension_semantics=("arbitrary",) * n_grid, vmem_limit_bytes=VMEM_LIMIT_BYTES)


def _sigmoid(x):
    return 1.0 / (1.0 + jnp.exp(-x))


def _rms(x, g):
    return x * lax.rsqrt(jnp.mean(x * x, axis=-1, keepdims=True) + EPS) * g


def _split_bf16(x):
    hi = lax.bitcast_convert_type(lax.bitcast_convert_type(x, U32) & jnp.uint32(0xFFFF0000), F32)
    return hi.astype(BF16), (x - hi).astype(BF16)


def _pair_specs(tm, width, n_prompt_blocks):
    return [pl.BlockSpec((tm, width), lambda i: (jnp.minimum(i, n_prompt_blocks - 1), 0)),
            pl.BlockSpec((tm, width), lambda i: (jnp.maximum(i - n_prompt_blocks, 0), 0))]


def _const_spec(shape):
    nd = len(shape)
    return pl.BlockSpec(shape, lambda *a: (0,) * nd)


def _inproj_body(xp_ref, xs_ref, g_ref, w_ref, qg_ref, kg_ref, seg_ref, u_ref, q_ref, k_ref, v_ref, *, n_pb, d_half):
    i = pl.program_id(0)
    x = jnp.where(i < n_pb, xp_ref[...], xs_ref[...])
    xn = _rms(x, g_ref[...]).astype(BF16)
    proj = jnp.dot(xn, w_ref[...], preferred_element_type=F32)
    u_ref[...] = proj[:, :d_half]
    v_ref[...] = proj[:, 3 * d_half:]

    def head_norm(t, gain):
        hi, lo = _split_bf16(t * t)
        msq = (jnp.dot(hi, seg_ref[...], preferred_element_type=F32)
               + jnp.dot(lo, seg_ref[...], preferred_element_type=F32))
        return t * lax.rsqrt(msq + EPS) * gain

    q_ref[...] = (head_norm(proj[:, d_half:2 * d_half], qg_ref[...]) * (HEAD_DIM ** -0.5)).astype(BF16)
    k_ref[...] = head_norm(proj[:, 2 * d_half:3 * d_half], kg_ref[...])


def _inproj(xp, xs, g, w, qg, kg):
    t_p, d = xp.shape
    t_s = xs.shape[0]
    d_half = w.shape[1] // 4
    n_pb = t_p // TM
    n_tot = (t_p + t_s) // TM
    heads = d_half // HEAD_DIM
    seg = jnp.kron(jnp.eye(heads, dtype=F32), jnp.full((HEAD_DIM, HEAD_DIM), 1.0 / HEAD_DIM, F32)).astype(BF16)
    out_f = jax.ShapeDtypeStruct((t_p + t_s, d_half), F32)
    out_b = jax.ShapeDtypeStruct((t_p + t_s, d_half), BF16)
    row = pl.BlockSpec((TM, d_half), lambda i: (i, 0))
    return pl.pallas_call(
        functools.partial(_inproj_body, n_pb=n_pb, d_half=d_half),
        grid=(n_tot,),
        in_specs=_pair_specs(TM, d, n_pb) + [_const_spec((1, d)), _const_spec(w.shape), _const_spec((1, d_half)),
                                             _const_spec((1, d_half)), _const_spec(seg.shape)],
        out_specs=[row, row, row, row],
        out_shape=[out_f, out_b, out_f, out_f],
        compiler_params=_cparams(1),
        name="inproj_even",
    )(xp, xs, g.reshape(1, d), w.astype(BF16), jnp.tile(qg, heads).reshape(1, d_half),
      jnp.tile(kg, heads).reshape(1, d_half), seg)


def _ssm_disc_body(are_ref, aim_ref, ldt_ref, bre_ref, bim_ref, barre_ref, barim_ref, bbre_ref, bbim_ref):
    lam_re, lam_im = are_ref[...], aim_ref[...]
    dt = jnp.exp(ldt_ref[...])
    mag = jnp.exp(lam_re * dt)
    bar_re, bar_im = mag * jnp.cos(lam_im * dt), mag * jnp.sin(lam_im * dt)
    den = lam_re * lam_re + lam_im * lam_im
    num_re = bar_re - 1.0
    coef_re = (num_re * lam_re + bar_im * lam_im) / den
    coef_im = (bar_im * lam_re - num_re * lam_im) / den
    br, bi = bre_ref[...], bim_ref[...]
    barre_ref[...] = bar_re
    barim_ref[...] = bar_im
    bbre_ref[...] = coef_re * br - coef_im * bi
    bbim_ref[...] = coef_re * bi + coef_im * br


def _ssm_discretize(a_re, a_im, log_dt, b_re, b_im):
    g, p = a_re.shape
    n = g * p
    col = jax.ShapeDtypeStruct((n, 1), F32)
    mat = jax.ShapeDtypeStruct((n, SSM_GROUP), F32)
    return pl.pallas_call(_ssm_disc_body, out_shape=[col, col, mat, mat], name="ssm_discretize")(
        a_re.reshape(n, 1), a_im.reshape(n, 1), jnp.repeat(log_dt, p).reshape(n, 1),
        b_re.reshape(n, SSM_GROUP), b_im.reshape(n, SSM_GROUP))


def _gelu_tanh(x):
    return 0.5 * x * (1.0 + jnp.tanh(math.sqrt(2.0 / math.pi) * (x + 0.044715 * (x * x * x))))


def _ssm_body(*refs, nb, lc, n_state_rows):
    u_refs = refs[:nb]
    h0_ref, lam_ref, bq_ref, cq_ref, d_ref, wglu_ref, bglu_ref, y_ref, ht_ref, s_ref, hcar_ref = refs[nb:]
    i = pl.program_id(0)
    half = n_state_rows // 2
    n_q = bq_ref.shape[0]
    tiles_q = n_state_rows // n_q
    k_q = bq_ref.shape[1]

    @pl.when(i == 0)
    def _():
        hcar_ref[...] = h0_ref[...]

    def tile_rows(tile):
        return slice(SUBLANES * tile, SUBLANES * (tile + 1))

    for b in range(nb):
        ub = u_refs[b][...].astype(BF16)
        for q in range(n_q):
            m = q % (n_q // 2)
            res = jnp.dot(ub[:, k_q * m:k_q * (m + 1)], bq_ref[q], preferred_element_type=F32)
            for c in range(tiles_q):
                s_ref[b, :, tile_rows(q * tiles_q + c), :] = (
                    res[:, LANES * c:LANES * (c + 1)].reshape(lc // SUBLANES, SUBLANES, LANES))

    lam_re, lam_im = lam_ref[:half, :], lam_ref[half:, :]

    def block(r, hs):
        hs = list(hs)
        for t in range(SUBLANES):
            for b in range(nb):
                bu = s_ref[b, r, pl.ds(t, n_state_rows, stride=SUBLANES), :]
                hr, hi = hs[2 * b], hs[2 * b + 1]
                nr = lam_re * hr - lam_im * hi + bu[:half]
                ni = lam_re * hi + lam_im * hr + bu[half:]
                s_ref[b, r, pl.ds(t, n_state_rows, stride=SUBLANES), :] = jnp.concatenate([nr, ni], axis=0)
                hs[2 * b], hs[2 * b + 1] = nr, ni
        return tuple(hs)

    init = []
    for b in range(nb):
        init += [hcar_ref[b, :half, :], hcar_ref[b, half:, :]]
    fin = lax.fori_loop(0, lc // SUBLANES, block, tuple(init))
    for b in range(nb):
        hb = jnp.concatenate([fin[2 * b], fin[2 * b + 1]], axis=0)
        hcar_ref[b] = hb
        ht_ref[b] = hb

    for b in range(nb):
        accs = []
        for m in range(n_q // 2):
            acc = None
            for q in (m, m + n_q // 2):
                tiles = [s_ref[b, :, tile_rows(q * tiles_q + c), :].reshape(lc, LANES) for c in range(tiles_q)]
                part = jnp.dot(jnp.concatenate(tiles, axis=1).astype(BF16), cq_ref[q], preferred_element_type=F32)
                acc = part if acc is None else acc + part
            accs.append(acc)
        y = jnp.concatenate(accs, axis=1) + d_ref[...] * u_refs[b][...]
        z = _gelu_tanh(y)
        gate = _sigmoid(jnp.dot(z.astype(BF16), wglu_ref[...], preferred_element_type=F32) + bglu_ref[...])
        y_ref[b] = (z * gate).astype(y_ref.dtype)


def _ssm(u, row0, nb, seq, lc, h0, lam, bbig, cbig, dvec, wglu, bglu):
    d_ssm = u.shape[1]
    n_state_rows = lam.shape[0]
    n_chunks = seq // lc
    blk0 = row0 // lc
    u_specs = [pl.BlockSpec((lc, d_ssm), functools.partial(lambda i, b: (blk0 + b * n_chunks + i, 0), b=b))
               for b in range(nb)]
    return pl.pallas_call(
        functools.partial(_ssm_body, nb=nb, lc=lc, n_state_rows=n_state_rows),
        grid=(n_chunks,),
        in_specs=u_specs + [_const_spec(h0.shape), _const_spec(lam.shape), _const_spec(bbig.shape),
                            _const_spec(cbig.shape), _const_spec(dvec.shape), _const_spec(wglu.shape),
                            _const_spec(bglu.shape)],
        out_specs=[pl.BlockSpec((nb, lc, d_ssm), lambda i: (0, i, 0)), _const_spec(h0.shape)],
        out_shape=[jax.ShapeDtypeStruct((nb, seq, d_ssm), BF16), jax.ShapeDtypeStruct(h0.shape, F32)],
        scratch_shapes=[pltpu.VMEM((nb, lc // SUBLANES, SUBLANES * n_state_rows, LANES), F32),
                        pltpu.VMEM(h0.shape, F32)],
        compiler_params=_cparams(1),
        name=f"ssm_b{nb}",
    )(*([u] * nb), h0, lam, bbig, cbig, dvec, wglu, bglu)


def _bias_rows(m_ref, lead, base, n_rows, n_cols):
    for r in range(n_rows):
        m_ref[lead, r:r + 1, :] = base[:, QBLK - 1 - r:QBLK - 1 - r + n_cols]


def _attn_prompt_body(q_ref, k_ref, v_ref, base_ref, o_ref, kb_ref, vb_ref, m_ref):
    j = pl.program_id(2)

    @pl.when(j == 0)
    def _():
        kb_ref[:BAND_ROWS, :] = jnp.zeros((BAND_ROWS, LANES), BF16)
        vb_ref[:BAND_ROWS, :] = jnp.zeros((BAND_ROWS, LANES), BF16)
        kb_ref[BAND_ROWS:, :] = k_ref[...].astype(BF16)
        vb_ref[BAND_ROWS:, :] = v_ref[...].astype(BF16)
        rq = lax.broadcasted_iota(I32, (QBLK, KWIN), 0) // CHUNK
        ck = lax.broadcasted_iota(I32, (QBLK, KWIN), 1) // CHUNK
        in_band = (ck >= rq) & (ck <= rq + LEFT_CHUNKS)
        for a in range(2):
            _bias_rows(m_ref, a, base_ref[0, a:a + 1, :], QBLK, KWIN)
            m_ref[a] = jnp.where(in_band, m_ref[a], NEG_INF)

    start = pl.multiple_of(j * (QSTEP * QBLK), QSTEP * QBLK)
    lane = lax.broadcasted_iota(I32, (QBLK, LANES), 1)
    col = lax.broadcasted_iota(I32, (2 * QBLK, KWIN), 1)
    bias = m_ref[...].reshape(2 * QBLK, KWIN)
    for c in range(QSTEP):
        q2 = q_ref[QBLK * c:QBLK * (c + 1), :]
        kw = kb_ref[pl.ds(start + QBLK * c, KWIN), :]
        vw = vb_ref[pl.ds(start + QBLK * c, KWIN), :]
        first_valid = BAND_ROWS - (j * QSTEP + c) * QBLK
        zero = jnp.zeros_like(q2)
        qs = jnp.concatenate([jnp.where(lane < HEAD_DIM, q2, zero), jnp.where(lane >= HEAD_DIM, q2, zero)], axis=0)
        s = lax.dot_general(qs, kw, (((1,), (1,)), ((), ())), preferred_element_type=F32)
        s = jnp.where(col >= first_valid, s + bias, NEG_INF)
        p = jnp.exp(s - jnp.max(s, axis=1, keepdims=True))
        o = jnp.dot(p.astype(BF16), vw, preferred_element_type=F32) * (1.0 / jnp.sum(p, axis=1, keepdims=True))
        o_ref[QBLK * c:QBLK * (c + 1), :] = jnp.where(lane < HEAD_DIM, o[:QBLK], o[QBLK:]).astype(o_ref.dtype)


def _attn_prompt(q, k, v, base2, nb, seq):
    d_attn = q.shape[1]
    pairs = d_attn // LANES
    nq = seq // (QSTEP * QBLK)
    kv_spec = pl.BlockSpec((seq, LANES), lambda b, p, j: (b, p))
    return pl.pallas_call(
        _attn_prompt_body,
        grid=(nb, pairs, nq),
        in_specs=[pl.BlockSpec((QSTEP * QBLK, LANES), lambda b, p, j: (b * nq + j, p)), kv_spec, kv_spec,
                  pl.BlockSpec((1, 2, base2.shape[2]), lambda b, p, j: (p, 0, 0))],
        out_specs=pl.BlockSpec((QSTEP * QBLK, LANES), lambda b, p, j: (b * nq + j, p)),
        out_shape=jax.ShapeDtypeStruct((nb * seq, d_attn), BF16),
        scratch_shapes=[pltpu.VMEM((seq + BAND_ROWS, LANES), BF16), pltpu.VMEM((seq + BAND_ROWS, LANES), BF16),
                        pltpu.VMEM((2, QBLK, KWIN), F32)],
        compiler_params=_cparams(3),
        name="attn_prompt",
    )(q, k, v, base2)


def _attn_sample_body(q_ref, k_ref, v_ref, ck_ref, cv_ref, base_ref, o_ref, m_ref, *, heads, n_cache):
    @pl.when(pl.program_id(0) == 0)
    def _():
        for h in range(heads):
            _bias_rows(m_ref, h, base_ref[h:h + 1, :], CHUNK, n_cache + CHUNK)

    q = q_ref[...]
    k = k_ref[...].astype(BF16)
    v = v_ref[...].astype(BF16)
    nt = (((1,), (1,)), ((), ()))
    outs = []
    for h in range(heads):
        sl = slice(HEAD_DIM * h, HEAD_DIM * (h + 1))
        qh = q[:, sl]
        s1 = lax.dot_general(qh, ck_ref[0, h].astype(BF16), nt, preferred_element_type=F32) + m_ref[h, :, :n_cache]
        s2 = lax.dot_general(qh, k[:, sl], nt, preferred_element_type=F32) + m_ref[h, :, n_cache:]
        mx = jnp.maximum(jnp.max(s1, axis=1, keepdims=True), jnp.max(s2, axis=1, keepdims=True))
        p1, p2 = jnp.exp(s1 - mx), jnp.exp(s2 - mx)
        den = jnp.sum(p1, axis=1, keepdims=True) + jnp.sum(p2, axis=1, keepdims=True)
        o = (jnp.dot(p1.astype(BF16), cv_ref[0, h].astype(BF16), preferred_element_type=F32)
             + jnp.dot(p2.astype(BF16), v[:, sl], preferred_element_type=F32))
        outs.append(o * (1.0 / den))
    o_ref[...] = jnp.concatenate(outs, axis=1).astype(o_ref.dtype)


def _attn_sample(q, k, v, cache_k, cache_v, base, row0):
    nb, heads, n_cache, _ = cache_k.shape
    d_attn = q.shape[1]
    blk0 = row0 // CHUNK
    row = pl.BlockSpec((CHUNK, d_attn), lambda b: (blk0 + b, 0))
    cache = pl.BlockSpec((1, heads, n_cache, HEAD_DIM), lambda b: (b, 0, 0, 0))
    return pl.pallas_call(
        functools.partial(_attn_sample_body, heads=heads, n_cache=n_cache),
        grid=(nb,),
        in_specs=[row, row, row, cache, cache, _const_spec(base.shape)],
        out_specs=pl.BlockSpec((CHUNK, d_attn), lambda b: (b, 0)),
        out_shape=jax.ShapeDtypeStruct((nb * CHUNK, d_attn), BF16),
        scratch_shapes=[pltpu.VMEM((heads, CHUNK, n_cache + CHUNK), F32)],
        compiler_params=_cparams(1),
        name="attn_sample",
    )(q, k, v, cache_k, cache_v, base)


def _bias_base(table):
    heads = table.shape[0]
    far = jnp.broadcast_to(table[:, 2 * MAX_REL:], (heads, KWIN - 1 - MAX_REL + 1))
    near = table[:, 2 * MAX_REL - 1:0:-1]
    base = jnp.concatenate([far, near], axis=1)
    return jnp.pad(base, ((0, 0), (0, KWIN + QBLK - base.shape[1])))


def _post_body(*refs, n_x, n_y, n_pb):
    i = pl.program_id(0)
    pos = 0

    def take(n):
        nonlocal pos
        out = refs[pos:pos + n]
        pos += n
        return out

    def merged(pair):
        if len(pair) == 1:
            return pair[0][...]
        return jnp.where(i < n_pb, pair[0][...], pair[1][...])

    x_refs = take(n_x)
    y_pairs = [take(2) for _ in range(n_y)]
    w_refs = take(n_y)
    g_ref, wr_ref, br_ref, tril_ref, cin_ref = take(5)
    xo_ref, mi_ref, mf_ref, cnt_ref = take(4)
    (carry_ref,) = take(1)

    x = merged(x_refs)
    for pair, w_ref in zip(y_pairs, w_refs):
        x = x + jnp.dot(merged(pair).astype(BF16), w_ref[...], preferred_element_type=F32)
    xo_ref[...] = x

    xn = _rms(x, g_ref[...])
    xh, xl = _split_bf16(xn)
    ph = jnp.dot(xh, wr_ref[...], preferred_element_type=F32)
    pl_ = jnp.dot(xl, wr_ref[...], preferred_element_type=F32)
    logits = (ph[:, :LANES] + pl_[:, :LANES] + ph[:, LANES:] + pl_[:, LANES:]) + br_ref[...]
    lane = lax.broadcasted_iota(I32, logits.shape, 1)

    def first_argmax(vals):
        top = jnp.max(vals, axis=1, keepdims=True)
        return top, jnp.min(jnp.where(vals == top, lane, LANES), axis=1, keepdims=True)

    is_group = lane < MOE_GROUPS
    g_top, g_idx = first_argmax(jnp.where(is_group, logits, -jnp.inf))
    g_w = 1.0 / jnp.sum(jnp.where(is_group, jnp.exp(logits - g_top), 0.0), axis=1, keepdims=True)
    first = MOE_GROUPS + EXPERTS_PER_GROUP * g_idx
    e_log = jnp.where((lane >= first) & (lane < first + EXPERTS_PER_GROUP), logits, -jnp.inf)
    v1, l1 = first_argmax(e_log)
    v2, l2 = first_argmax(jnp.where(lane == l1, -jnp.inf, e_log))
    t = jnp.exp(v2 - v1)
    w1 = 1.0 / (1.0 + t)
    w2 = t * w1
    e1, e2 = l1 - MOE_GROUPS, l2 - MOE_GROUPS

    @pl.when(i == 0)
    def _():
        carry_ref[...] = cin_ref[...]

    hit1, hit2 = lane == e1, lane == e2
    oh = jnp.where(hit1 | hit2, 1.0, 0.0)
    before = jnp.dot(tril_ref[...], oh.astype(BF16), preferred_element_type=F32) + carry_ref[...]
    r1 = jnp.sum(jnp.where(hit1, before, 0.0), axis=1, keepdims=True).astype(I32)
    r2 = jnp.sum(jnp.where(hit2, before, 0.0), axis=1, keepdims=True).astype(I32)
    carry_ref[...] = carry_ref[...] + jnp.sum(oh, axis=0, keepdims=True)
    cnt_ref[...] = carry_ref[...]

    mi_ref[...] = jnp.where(lane == 0, e1, jnp.where(lane == 1, e2, jnp.where(lane == 2, r1, r2)))
    mf_ref[...] = jnp.where(lane == 0, g_w * w1, g_w * w2)


def _post_mixer(x_src, y_pairs, ws, g, router, tril, t_p, t_s):
    wr, br = router
    d = ws[0].shape[1]
    n_pb = t_p // TM
    n_tot = (t_p + t_s) // TM
    row = lambda width: pl.BlockSpec((TM, width), lambda i: (i, 0))
    in_specs, args = [], []
    if len(x_src) == 1:
        in_specs.append(row(d))
    else:
        in_specs += _pair_specs(TM, d, n_pb)
    args += list(x_src)
    for yp, ys in y_pairs:
        in_specs += _pair_specs(TM, yp.shape[1], n_pb)
        args += [yp, ys]
    consts = list(ws) + [g.reshape(1, d), wr, br, tril, jnp.zeros((1, LANES), F32)]
    in_specs += [_const_spec(c.shape) for c in consts]
    t = t_p + t_s
    return pl.pallas_call(
        functools.partial(_post_body, n_x=len(x_src), n_y=len(y_pairs), n_pb=n_pb),
        grid=(n_tot,),
        in_specs=in_specs,
        out_specs=[row(d), row(LANES), row(LANES), _const_spec((1, LANES))],
        out_shape=[jax.ShapeDtypeStruct((t, d), F32), jax.ShapeDtypeStruct((t, LANES), I32), jax.ShapeDtypeStruct((t, LANES), F32),
                   jax.ShapeDtypeStruct((1, LANES), F32)],
        scratch_shapes=[pltpu.VMEM((1, LANES), F32)],
        compiler_params=_cparams(1),
        name="post_mixer",
    )(*args, *consts)


def _router_weights(gate_w, gate_b, router_w, router_b):
    d = gate_w.shape[0]
    w = jnp.concatenate([gate_w, router_w.transpose(1, 0, 2).reshape(d, N_EXPERTS)], axis=1)
    b = jnp.concatenate([gate_b, router_b.reshape(N_EXPERTS)])
    w = jnp.pad(w, ((0, 0), (0, LANES - w.shape[1])))
    b = jnp.pad(b, (0, LANES - b.shape[0])).reshape(1, LANES)
    hi, lo = _split_bf16(w)
    return jnp.concatenate([hi, lo], axis=1), b


def _dispatch_body(slots_ref, pad_pos_ref, pad_n_ref, x_ref, g_ref, xs_ref, stage_ref, zero_ref, sem, *, tm, tile):
    i = pl.program_id(0)

    def zero_copy(pos, n):
        return pltpu.make_async_copy(zero_ref.at[pl.ds(0, n)], xs_ref.at[pl.ds(pos, n)], sem.at[1])

    @pl.when(i == 0)
    def _():
        zero_ref[...] = jnp.zeros(zero_ref.shape, zero_ref.dtype)
        for e in range(N_EXPERTS):
            pos, n = pad_pos_ref[e], pad_n_ref[e]
            bit = tile // 2
            while bit >= 1:
                @pl.when((n & bit) != 0)
                def _(pos=pos, bit=bit):
                    cp = zero_copy(pos, bit)
                    cp.start()
                    cp.wait()
                pos = pos + (n & bit)
                bit //= 2
        used_tiles = (pad_pos_ref[N_EXPERTS - 1] + pad_n_ref[N_EXPERTS - 1]) // tile
        all_tiles = xs_ref.shape[0] // tile
        lax.fori_loop(used_tiles, all_tiles, lambda tl, c: (zero_copy(tl * tile, tile).start(), c)[1], 0)
        lax.fori_loop(used_tiles, all_tiles, lambda tl, c: (zero_copy(tl * tile, tile).wait(), c)[1], 0)

    xn = _rms(x_ref[...], g_ref[...])
    for c in range(ROWS_O):
        stage_ref[pl.ds(c, tm, stride=ROWS_O), :] = xn[:, LANES * c:LANES * (c + 1)]

    def start(r, c):
        src = stage_ref.at[pl.ds(pl.multiple_of(r * ROWS_O, ROWS_O), ROWS_O)]
        for k in range(2):
            slot = slots_ref[2 * (i * tm + r) + k]
            pltpu.make_async_copy(src, xs_ref.at[slot], sem.at[0]).start(priority=k)
        return c

    lax.fori_loop(0, tm, start, 0, unroll=8)
    pltpu.make_async_copy(xs_ref.at[pl.ds(0, 2 * tm)], xs_ref.at[pl.ds(0, 2 * tm)], sem.at[0]).wait()


def _dispatch(x, g, slots, pad_pos, pad_n, n_slots):
    t, d = x.shape
    slab = (ROWS_O, LANES)
    grid_spec = pltpu.PrefetchScalarGridSpec(
        num_scalar_prefetch=3,
        grid=(t // TM,),
        in_specs=[pl.BlockSpec((TM, d), lambda i, *_: (i, 0)), pl.BlockSpec((1, d), lambda i, *_: (0, 0))],
        out_specs=pl.BlockSpec(memory_space=pl.ANY),
        scratch_shapes=[pltpu.VMEM((TM * ROWS_O, LANES), F32), pltpu.VMEM((TM_MOE,) + slab, F32),
                        pltpu.SemaphoreType.DMA((2,))],
    )
    return pl.pallas_call(
        functools.partial(_dispatch_body, tm=TM, tile=TM_MOE),
        grid_spec=grid_spec,
        out_shape=jax.ShapeDtypeStruct((n_slots,) + slab, F32),
        compiler_params=_cparams(1),
        name="moe_dispatch",
    )(slots, pad_pos, pad_n, x, g.reshape(1, d))


def _moe_body(te_ref, nt_ref, x_ref, wgu_ref, wd_ref, o_ref, wgu_bf, wd_bf, *, tm):
    i = pl.program_id(0)
    prev = te_ref[jnp.maximum(i - 1, 0)]

    @pl.when((i == 0) | (te_ref[i] != prev))
    def _():
        wgu_bf[...] = wgu_ref[0, 0].astype(BF16)
        wd_bf[...] = wd_ref[0, 0].astype(BF16)

    @pl.when(i < nt_ref[0])
    def _():
        x = jnp.concatenate([x_ref[pl.ds(c, tm, stride=ROWS_O), :] for c in range(ROWS_O)], axis=1)
        h = jnp.dot(x.astype(BF16), wgu_bf[...], preferred_element_type=F32)
        d_exp = h.shape[1] // 2
        gt, up = h[:, :d_exp], h[:, d_exp:]
        act = (gt * _sigmoid(gt) * up).astype(BF16)
        out = jnp.dot(act, wd_bf[...], preferred_element_type=F32)
        for c in range(ROWS_O):
            o_ref[pl.ds(c, tm, stride=ROWS_O), :] = out[:, LANES * c:LANES * (c + 1)]

    @pl.when(i >= nt_ref[0])
    def _():
        o_ref[...] = jnp.zeros(o_ref.shape, o_ref.dtype)


def _moe_experts(xsorted, tile_expert, n_tiles, wgu, wd, layer):
    n_slots = xsorted.shape[0] // ROWS_O
    n_max = n_slots // TM_MOE
    grid_spec = pltpu.PrefetchScalarGridSpec(
        num_scalar_prefetch=2,
        grid=(n_max,),
        in_specs=[pl.BlockSpec((TM_MOE * ROWS_O, LANES), lambda i, te, nt: (jnp.minimum(i, nt[0] - 1), 0)),
                  pl.BlockSpec((1, 1) + wgu.shape[2:], lambda i, te, nt: (layer, te[i], 0, 0)),
                  pl.BlockSpec((1, 1) + wd.shape[2:], lambda i, te, nt: (layer, te[i], 0, 0))],
        out_specs=pl.BlockSpec((TM_MOE * ROWS_O, LANES), lambda i, te, nt: (i, 0)),
        scratch_shapes=[pltpu.VMEM(wgu.shape[2:], BF16), pltpu.VMEM(wd.shape[2:], BF16)],
    )
    return pl.pallas_call(
        functools.partial(_moe_body, tm=TM_MOE),
        grid_spec=grid_spec,
        out_shape=jax.ShapeDtypeStruct((n_slots * ROWS_O, LANES), F32),
        compiler_params=_cparams(1),
        name="moe_experts",
    )(tile_expert, n_tiles, xsorted, wgu, wd)


def _combine_body(slots_ref, x_ref, mf_ref, os_ref, *rest, tm, n_steps, n_pb):
    out_refs, (buf_ref, sem) = rest[:-2], rest[-2:]
    i = pl.program_id(0)

    def issue(tile, par):
        def body(r, c):
            for k in range(2):
                slot = slots_ref[2 * (tile * tm + r) + k]
                dst = buf_ref.at[par, k, pl.ds(pl.multiple_of(r * ROWS_O, ROWS_O), ROWS_O)]
                pltpu.make_async_copy(os_ref.at[slot], dst, sem.at[par]).start(priority=k)
            return c
        lax.fori_loop(0, tm, body, 0, unroll=8)

    @pl.when(i == 0)
    def _():
        issue(0, 0)

    @pl.when(i + 1 < n_steps)
    def _():
        issue(i + 1, (i + 1) % 2)

    par = i % 2
    pltpu.make_async_copy(buf_ref.at[par], buf_ref.at[par], sem.at[par]).wait()

    def write(out_ref):
        mf = mf_ref[...]
        g1, g2 = mf[:, 0:1], mf[:, 1:2]
        for c in range(ROWS_O):
            cols = slice(LANES * c, LANES * (c + 1))
            piece = lambda k: buf_ref[par, k, pl.ds(c, tm, stride=ROWS_O), :]
            out_ref[:, cols] = x_ref[:, cols] + g1 * piece(0) + g2 * piece(1)

    if len(out_refs) == 1:
        write(out_refs[0])
    else:
        pl.when(i < n_pb)(lambda: write(out_refs[0]))
        pl.when(i >= n_pb)(lambda: write(out_refs[1]))


def _combine(x, meta_f, slots, osorted, t_p, split):
    t, d = x.shape
    n_steps = t // TM
    n_pb = t_p // TM
    row = lambda width: pl.BlockSpec((TM, width), lambda i, *_: (i, 0))
    if split:
        out_specs = [pl.BlockSpec((TM, d), lambda i, *_: (jnp.minimum(i, n_pb - 1), 0)),
                     pl.BlockSpec((TM, d), lambda i, *_: (jnp.maximum(i - n_pb, 0), 0))]
        out_shape = [jax.ShapeDtypeStruct((t_p, d), F32), jax.ShapeDtypeStruct((t - t_p, d), F32)]
    else:
        out_specs = [row(d)]
        out_shape = [jax.ShapeDtypeStruct((t, d), F32)]
    grid_spec = pltpu.PrefetchScalarGridSpec(
        num_scalar_prefetch=1,
        grid=(n_steps,),
        in_specs=[row(d), row(LANES), pl.BlockSpec(memory_space=pl.ANY)],
        out_specs=out_specs,
        scratch_shapes=[pltpu.VMEM((2, 2, TM * ROWS_O, LANES), F32), pltpu.SemaphoreType.DMA((2,))],
    )
    return pl.pallas_call(
        functools.partial(_combine_body, tm=TM, n_steps=n_steps, n_pb=n_pb),
        grid_spec=grid_spec,
        out_shape=out_shape,
        compiler_params=_cparams(1),
        name="moe_combine",
    )(slots, x, meta_f, osorted)


def _moe_layer(x, g, meta_i, meta_f, counts, wgu, wd, layer, t_p, split):
    t = x.shape[0]
    n_max = (2 * t + N_EXPERTS * (TM_MOE - 1)) // TM_MOE
    cnt = counts[0, :N_EXPERTS].astype(I32)
    padded = (cnt + TM_MOE - 1) // TM_MOE * TM_MOE
    ends = jnp.cumsum(padded)
    offs = ends - padded
    experts = jnp.arange(N_EXPERTS, dtype=I32)
    first_slot = jnp.sum(jnp.where(meta_i[:, 0:2, None] == experts, offs, 0), axis=-1)
    slots = (first_slot + meta_i[:, 2:4]).reshape(-1)
    n_tiles = (ends[-1] // TM_MOE).reshape(1)
    tile_start = jnp.arange(n_max, dtype=I32) * TM_MOE
    tile_expert = jnp.minimum(jnp.sum((ends[None, :] <= tile_start[:, None]).astype(I32), axis=1), N_EXPERTS - 1)
    n_slots = n_max * TM_MOE
    xsorted = _dispatch(x, g, slots, offs + cnt, padded - cnt, n_slots)
    osorted = _moe_experts(xsorted.reshape(n_slots * ROWS_O, LANES), tile_expert, n_tiles, wgu, wd, layer)
    return _combine(x, meta_f, slots, osorted.reshape(n_slots, ROWS_O, LANES), t_p, split)


def _conv_in_body(x_ref, g_ref, w_ref, u_ref):
    xn = _rms(x_ref[...], g_ref[...]).astype(BF16)
    h = jnp.dot(xn, w_ref[...], preferred_element_type=F32)
    c = h.shape[1] // 2
    u_ref[...] = h[:, :c] * _sigmoid(h[:, c:])


def _conv_in(x, g, w):
    t, d = x.shape
    c = w.shape[1] // 2
    return pl.pallas_call(
        _conv_in_body,
        grid=(t // TM,),
        in_specs=[pl.BlockSpec((TM, d), lambda i: (i, 0)), _const_spec((1, d)), _const_spec(w.shape)],
        out_specs=pl.BlockSpec((TM, c), lambda i: (i, 0)),
        out_shape=jax.ShapeDtypeStruct((t, c), F32),
        compiler_params=_cparams(1),
        name="conv_in",
    )(x, g.reshape(1, d), w.astype(BF16))


def _dwconv_body(cur_ref, prev_ref, left_ref, w_ref, b_ref, g_ref, beta_ref, o_ref, win_ref, y_ref, *, tm):
    i = pl.program_id(1)
    n_lt = cur_ref.shape[1] // LANES
    halo = jnp.where(i == 0, left_ref[0], prev_ref[...])
    cur = cur_ref[...]
    for c in range(n_lt):
        cols = slice(LANES * c, LANES * (c + 1))
        win_ref[pl.ds(c, CONV_HALO, stride=n_lt), :] = halo[:, cols]
        win_ref[pl.ds(CONV_HALO * n_lt + c, tm, stride=n_lt), :] = cur[:, cols]
    first_tap = CONV_HALO - (CONV_WIDTH - 1)
    taps = [w_ref[j] for j in range(CONV_WIDTH)]

    def block(rb, carry):
        for t in range(SUBLANES):
            frame = rb * SUBLANES + t
            acc = None
            for j in range(CONV_WIDTH):
                row = pl.multiple_of((frame + first_tap + j) * n_lt, n_lt)
                term = taps[j] * win_ref[pl.ds(row, n_lt), :]
                acc = term if acc is None else acc + term
            y_ref[pl.ds(pl.multiple_of(frame * n_lt, n_lt), n_lt), :] = acc
        return carry

    lax.fori_loop(0, tm // SUBLANES, block, 0)
    y = jnp.concatenate([y_ref[pl.ds(c, tm, stride=n_lt), :] for c in range(n_lt)], axis=1)
    y = y + b_ref[...]
    mu = jnp.mean(y, axis=1, keepdims=True)
    yc = y - mu
    yn = yc * lax.rsqrt(jnp.mean(yc * yc, axis=1, keepdims=True) + EPS) * g_ref[...] + beta_ref[...]
    o_ref[...] = (yn * _sigmoid(yn)).astype(BF16)


def _dwconv(u, left, dw_w, dw_b, ln_g, ln_b, row0, nb, seq, tm):
    c = u.shape[1]
    n_t = seq // tm
    blk0 = row0 // tm
    halo_per_tile = tm // CONV_HALO
    vec = lambda a: a.reshape(1, c)
    w = dw_w.reshape(CONV_WIDTH, c // LANES, LANES)
    return pl.pallas_call(
        functools.partial(_dwconv_body, tm=tm),
        grid=(nb, n_t),
        in_specs=[pl.BlockSpec((tm, c), lambda b, i: (blk0 + b * n_t + i, 0)),
                  pl.BlockSpec((CONV_HALO, c),
                               lambda b, i: (jnp.maximum((blk0 + b * n_t + i) * halo_per_tile - 1, 0), 0)),
                  pl.BlockSpec((1, CONV_HALO, c), lambda b, i: (b, 0, 0)),
                  _const_spec(w.shape), _const_spec((1, c)), _const_spec((1, c)), _const_spec((1, c))],
        out_specs=pl.BlockSpec((tm, c), lambda b, i: (b * n_t + i, 0)),
        out_shape=jax.ShapeDtypeStruct((nb * seq, c), BF16),
        scratch_shapes=[pltpu.VMEM(((tm + CONV_HALO) * (c // LANES), LANES), F32),
                        pltpu.VMEM((tm * (c // LANES), LANES), F32)],
        compiler_params=_cparams(2),
        name=f"dwconv_b{nb}",
    )(u, u, left, w, vec(dw_b), vec(ln_g), vec(ln_b))


def _block_diag(blocks):
    g, a, b = blocks.shape
    return jnp.einsum("gab,gh->gahb", blocks, jnp.eye(g, dtype=blocks.dtype)).reshape(g * a, g * b)


def _state_rows(re, im):
    nb = re.shape[0]
    return jnp.concatenate([re.reshape(nb, -1, LANES), im.reshape(nb, -1, LANES)], axis=1)


def kernel(x_prompt, x_sample, cache_attn_k, cache_attn_v, state_ssm_re, state_ssm_im, cache_conv, norm_mixer, norm_ffn, w_in_even, w_out_even, ssm_a_re, ssm_a_im, ssm_log_dt, ssm_b_re, ssm_b_im, ssm_c_re, ssm_c_im, ssm_d, ssm_glu_w, ssm_glu_b, attn_q_gain, attn_k_gain, attn_rel_bias, conv_w_in, conv_dw_w, conv_dw_b, conv_ln_g, conv_ln_b, conv_w_out, moe_gate_w, moe_gate_b, moe_router_w, moe_router_b, moe_w_gate_up, moe_w_down):
    nb_p, seq_p, d = x_prompt.shape
    nb_s, seq_s, _ = x_sample.shape
    t_p, t_s = nb_p * seq_p, nb_s * seq_s
    xp, xs = x_prompt.reshape(t_p, d), x_sample.reshape(t_s, d)
    depth = norm_mixer.shape[0]
    tril = jnp.tril(jnp.ones((TM, TM), F32), -1).astype(BF16)

    ks_p, vs_p, hs_p, ks_s, vs_s, hs_s, conv_p, conv_s = [], [], [], [], [], [], [], []
    x_src = (xp, xs)
    for layer in range(depth):
        j = layer // 2
        if layer % 2 == 0:
            u, q, k, v = _inproj(*x_src, norm_mixer[layer], w_in_even[j], attn_q_gain[j], attn_k_gain[j])
            d_ssm = u.shape[1]
            groups, n_state = ssm_a_re[j].shape
            bar_re, bar_im, bb_re, bb_im = _ssm_discretize(ssm_a_re[j], ssm_a_im[j], ssm_log_dt[j],
                                                           ssm_b_re[j], ssm_b_im[j])
            lam = jnp.concatenate([bar_re.reshape(-1, LANES), bar_im.reshape(-1, LANES)], axis=0)
            to_state = lambda m: _block_diag(m.reshape(groups, n_state, SSM_GROUP).transpose(0, 2, 1))
            bbig = jnp.concatenate([to_state(bb_re), to_state(bb_im)], axis=1).astype(BF16)
            cbig = jnp.concatenate([_block_diag(ssm_c_re[j].transpose(0, 2, 1)),
                                    _block_diag(-ssm_c_im[j].transpose(0, 2, 1))], axis=0).astype(BF16)
            n_q, k_q = 4, d_ssm // 2
            w_q = bbig.shape[1] // n_q
            blocks = lambda q: (slice(k_q * (q % 2), k_q * (q % 2 + 1)), slice(w_q * q, w_q * (q + 1)))
            bbig = jnp.stack([bbig[blocks(q)[0], blocks(q)[1]] for q in range(n_q)])
            cbig = jnp.stack([cbig[blocks(q)[1], blocks(q)[0]] for q in range(n_q)])
            ssm_args = (lam, bbig, cbig, ssm_d[j].reshape(1, d_ssm), _block_diag(ssm_glu_w[j]).astype(BF16),
                        ssm_glu_b[j].reshape(1, d_ssm))
            zero_state = jnp.zeros((nb_p, groups, n_state), F32)
            y_ssm_p, h_p = _ssm(u, 0, nb_p, seq_p, SSM_CHUNK, _state_rows(zero_state, zero_state), *ssm_args)
            y_ssm_s, h_s = _ssm(u, t_p, nb_s, seq_s, seq_s, _state_rows(state_ssm_re[j], state_ssm_im[j]), *ssm_args)
            base = _bias_base(attn_rel_bias[j])
            heads = base.shape[0]
            y_att_p = _attn_prompt(q, k, v, base.reshape(heads // 2, 2, -1), nb_p, seq_p)
            y_att_s = _attn_sample(q, k, v, cache_attn_k[j], cache_attn_v[j], base, t_p)
            w_out = w_out_even[j].astype(BF16)
            y_pairs = [(y_ssm_p.reshape(t_p, d_ssm), y_ssm_s.reshape(t_s, d_ssm)), (y_att_p, y_att_s)]
            ws = [w_out[:d_ssm], w_out[d_ssm:]]

            keep = min(BAND_ROWS, seq_p)
            heads_of = lambda a, nb: a.reshape(nb, -1, heads, HEAD_DIM).transpose(0, 2, 1, 3)
            tail_p = lambda a: jnp.concatenate([a[(b + 1) * seq_p - keep:(b + 1) * seq_p] for b in range(nb_p)])
            ks_p.append(heads_of(tail_p(k), nb_p))
            vs_p.append(heads_of(tail_p(v), nb_p))
            ks_s.append(heads_of(k[t_p:], nb_s))
            vs_s.append(heads_of(v[t_p:], nb_s))
            half = h_p.shape[1] // 2
            hs_p.append((h_p[:, :half].reshape(nb_p, groups, n_state), h_p[:, half:].reshape(nb_p, groups, n_state)))
            hs_s.append((h_s[:, :half].reshape(nb_s, groups, n_state), h_s[:, half:].reshape(nb_s, groups, n_state)))
        else:
            (x_merged,) = x_src
            uc = _conv_in(x_merged, norm_mixer[layer], conv_w_in[j])
            c = uc.shape[1]
            conv_args = (conv_dw_w[j], conv_dw_b[j], conv_ln_g[j], conv_ln_b[j])
            left_p = jnp.zeros((nb_p, CONV_HALO, c), F32)
            left_s = jnp.pad(cache_conv[j], ((0, 0), (CONV_HALO - (CONV_WIDTH - 1), 0), (0, 0)))
            act_p = _dwconv(uc, left_p, *conv_args, 0, nb_p, seq_p, TM_CONV)
            act_s = _dwconv(uc, left_s, *conv_args, t_p, nb_s, seq_s, seq_s)
            y_pairs = [(act_p, act_s)]
            ws = [conv_w_out[j].astype(BF16)]
            keep = CONV_WIDTH - 1
            conv_p.append(jnp.stack([uc[(b + 1) * seq_p - keep:(b + 1) * seq_p] for b in range(nb_p)]))
            ext_s = jnp.concatenate([cache_conv[j], uc[t_p:].reshape(nb_s, seq_s, c)], axis=1)
            conv_s.append(ext_s[:, ext_s.shape[1] - keep:])

        router = _router_weights(moe_gate_w[layer], moe_gate_b[layer], moe_router_w[layer], moe_router_b[layer])
        x_new, meta_i, meta_f, counts = _post_mixer(x_src, y_pairs, ws, norm_ffn[layer], router, tril, t_p, t_s)
        last = layer == depth - 1
        out = _moe_layer(x_new, norm_ffn[layer], meta_i, meta_f, counts, moe_w_gate_up, moe_w_down, layer, t_p, last)
        x_src = tuple(out)

    y_prompt, y_sample = x_src
    stack_c = lambda parts, idx: jnp.stack([p[idx] for p in parts])
    return (y_prompt.reshape(nb_p, seq_p, d), y_sample.reshape(nb_s, seq_s, d),
            jnp.stack(ks_p), jnp.stack(vs_p), stack_c(hs_p, 0), stack_c(hs_p, 1), jnp.stack(conv_p),
            jnp.stack(ks_s), jnp.stack(vs_s), stack_c(hs_s, 0), stack_c(hs_s, 1), jnp.stack(conv_s))
```

```python
import functools
import math

import jax
import jax.numpy as jnp
from jax import lax
from jax.experimental import pallas as pl
from jax.experimental.pallas import tpu as pltpu

F32, BF16, I32, U32 = jnp.float32, jnp.bfloat16, jnp.int32, jnp.uint32

CHUNK = 64
SSM_GROUP = 16
SSM_STATE = 64
HEAD_DIM = 64
LEFT_CHUNKS = 8
BAND_ROWS = LEFT_CHUNKS * CHUNK
MAX_REL = 128
CONV_WIDTH = 31
MOE_GROUPS = 4
EXPERTS_PER_GROUP = 8
N_EXPERTS = MOE_GROUPS * EXPERTS_PER_GROUP
EPS = 1e-6
NEG_INF = -1e30

LANES = 128
SUBLANES = 8
VMEM_LIMIT_BYTES = 56 * 1024 * 1024

TM = 512
TM_MOE = 512
QBLK = 2 * CHUNK
KWIN = QBLK + BAND_ROWS
QSTEP = 16
SSM_CHUNK = 512
D_MODEL = 1024
ROWS_O = D_MODEL // LANES
CONV_HALO = 32
TM_CONV = 512


def _cparams(n_grid):
    return pltpu.CompilerParams(dimension_semantics=("arbitrary",) * n_grid, vmem_limit_bytes=VMEM_LIMIT_BYTES)


def _sigmoid(x):
    return 1.0 / (1.0 + jnp.exp(-x))


def _rms(x, g):
    return x * lax.rsqrt(jnp.mean(x * x, axis=-1, keepdims=True) + EPS) * g


def _split_bf16(x):
    hi = lax.bitcast_convert_type(lax.bitcast_convert_type(x, U32) & jnp.uint32(0xFFFF0000), F32)
    return hi.astype(BF16), (x - hi).astype(BF16)


def _pair_specs(tm, width, n_prompt_blocks):
    return [pl.BlockSpec((tm, width), lambda i: (jnp.minimum(i, n_prompt_blocks - 1), 0)),
            pl.BlockSpec((tm, width), lambda i: (jnp.maximum(i - n_prompt_blocks, 0), 0))]


def _const_spec(shape):
    nd = len(shape)
    return pl.BlockSpec(shape, lambda *a: (0,) * nd)


def _inproj_body(xp_ref, xs_ref, g_ref, w_ref, qg_ref, kg_ref, seg_ref, u_ref, q_ref, k_ref, v_ref, *, n_pb, d_half):
    i = pl.program_id(0)
    x = jnp.where(i < n_pb, xp_ref[...], xs_ref[...])
    xn = _rms(x, g_ref[...]).astype(BF16)
    proj = jnp.dot(xn, w_ref[...], preferred_element_type=F32)
    u_ref[...] = proj[:, :d_half]
    v_ref[...] = proj[:, 3 * d_half:]

    def head_norm(t, gain):
        hi, lo = _split_bf16(t * t)
        msq = (jnp.dot(hi, seg_ref[...], preferred_element_type=F32)
               + jnp.dot(lo, seg_ref[...], preferred_element_type=F32))
        return t * lax.rsqrt(msq + EPS) * gain

    q_ref[...] = (head_norm(proj[:, d_half:2 * d_half], qg_ref[...]) * (HEAD_DIM ** -0.5)).astype(BF16)
    k_ref[...] = head_norm(proj[:, 2 * d_half:3 * d_half], kg_ref[...])


def _inproj(xp, xs, g, w, qg, kg):
    t_p, d = xp.shape
    t_s = xs.shape[0]
    d_half = w.shape[1] // 4
    n_pb = t_p // TM
    n_tot = (t_p + t_s) // TM
    heads = d_half // HEAD_DIM
    seg = jnp.kron(jnp.eye(heads, dtype=F32), jnp.full((HEAD_DIM, HEAD_DIM), 1.0 / HEAD_DIM, F32)).astype(BF16)
    out_f = jax.ShapeDtypeStruct((t_p + t_s, d_half), F32)
    out_b = jax.ShapeDtypeStruct((t_p + t_s, d_half), BF16)
    row = pl.BlockSpec((TM, d_half), lambda i: (i, 0))
    return pl.pallas_call(
        functools.partial(_inproj_body, n_pb=n_pb, d_half=d_half),
        grid=(n_tot,),
        in_specs=_pair_specs(TM, d, n_pb) + [_const_spec((1, d)), _const_spec(w.shape), _const_spec((1, d_half)),
                                             _const_spec((1, d_half)), _const_spec(seg.shape)],
        out_specs=[row, row, row, row],
        out_shape=[out_f, out_b, out_f, out_f],
        compiler_params=_cparams(1),
        name="inproj_even",
    )(xp, xs, g.reshape(1, d), w.astype(BF16), jnp.tile(qg, heads).reshape(1, d_half),
      jnp.tile(kg, heads).reshape(1, d_half), seg)


def _ssm_disc_body(are_ref, aim_ref, ldt_ref, bre_ref, bim_ref, barre_ref, barim_ref, bbre_ref, bbim_ref):
    lam_re, lam_im = are_ref[...], aim_ref[...]
    dt = jnp.exp(ldt_ref[...])
    mag = jnp.exp(lam_re * dt)
    bar_re, bar_im = mag * jnp.cos(lam_im * dt), mag * jnp.sin(lam_im * dt)
    den = lam_re * lam_re + lam_im * lam_im
    num_re = bar_re - 1.0
    coef_re = (num_re * lam_re + bar_im * lam_im) / den
    coef_im = (bar_im * lam_re - num_re * lam_im) / den
    br, bi = bre_ref[...], bim_ref[...]
    barre_ref[...] = bar_re
    barim_ref[...] = bar_im
    bbre_ref[...] = coef_re * br - coef_im * bi
    bbim_ref[...] = coef_re * bi + coef_im * br


def _ssm_discretize(a_re, a_im, log_dt, b_re, b_im):
    g, p = a_re.shape
    n = g * p
    col = jax.ShapeDtypeStruct((n, 1), F32)
    mat = jax.ShapeDtypeStruct((n, SSM_GROUP), F32)
    return pl.pallas_call(_ssm_disc_body, out_shape=[col, col, mat, mat], name="ssm_discretize")(
        a_re.reshape(n, 1), a_im.reshape(n, 1), jnp.repeat(log_dt, p).reshape(n, 1),
        b_re.reshape(n, SSM_GROUP), b_im.reshape(n, SSM_GROUP))


def _gelu_tanh(x):
    return 0.5 * x * (1.0 + jnp.tanh(math.sqrt(2.0 / math.pi) * (x + 0.044715 * (x * x * x))))


def _ssm_body(*refs, nb, lc, n_state_rows):
    u_refs = refs[:nb]
    h0_ref, lam_ref, bq_ref, cq_ref, d_ref, wglu_ref, bglu_ref, y_ref, ht_ref, s_ref, hcar_ref = refs[nb:]
    i = pl.program_id(0)
    half = n_state_rows // 2
    n_q = bq_ref.shape[0]
    tiles_q = n_state_rows // n_q
    k_q = bq_ref.shape[1]

    @pl.when(i == 0)
    def _():
        hcar_ref[...] = h0_ref[...]

    def tile_rows(tile):
        return slice(SUBLANES * tile, SUBLANES * (tile + 1))

    for b in range(nb):
        ub = u_refs[b][...].astype(BF16)
        for q in range(n_q):
            m = q % (n_q // 2)
            res = jnp.dot(ub[:, k_q * m:k_q * (m + 1)], bq_ref[q], preferred_element_type=F32)
            for c in range(tiles_q):
                s_ref[b, :, tile_rows(q * tiles_q + c), :] = (
                    res[:, LANES * c:LANES * (c + 1)].reshape(lc // SUBLANES, SUBLANES, LANES))

    lam_re, lam_im = lam_ref[:half, :], lam_ref[half:, :]

    def block(r, hs):
        hs = list(hs)
        for t in range(SUBLANES):
            for b in range(nb):
                bu = s_ref[b, r, pl.ds(t, n_state_rows, stride=SUBLANES), :]
                hr, hi = hs[2 * b], hs[2 * b + 1]
                nr = lam_re * hr - lam_im * hi + bu[:half]
                ni = lam_re * hi + lam_im * hr + bu[half:]
                s_ref[b, r, pl.ds(t, n_state_rows, stride=SUBLANES), :] = jnp.concatenate([nr, ni], axis=0)
                hs[2 * b], hs[2 * b + 1] = nr, ni
        return tuple(hs)

    init = []
    for b in range(nb):
        init += [hcar_ref[b, :half, :], hcar_ref[b, half:, :]]
    fin = lax.fori_loop(0, lc // SUBLANES, block, tuple(init))
    for b in range(nb):
        hb = jnp.concatenate([fin[2 * b], fin[2 * b + 1]], axis=0)
        hcar_ref[b] = hb
        ht_ref[b] = hb

    for b in range(nb):
        accs = []
        for m in range(n_q // 2):
            acc = None
            for q in (m, m + n_q // 2):
                tiles = [s_ref[b, :, tile_rows(q * tiles_q + c), :].reshape(lc, LANES) for c in range(tiles_q)]
                part = jnp.dot(jnp.concatenate(tiles, axis=1).astype(BF16), cq_ref[q], preferred_element_type=F32)
                acc = part if acc is None else acc + part
            accs.append(acc)
        y = jnp.concatenate(accs, axis=1) + d_ref[...] * u_refs[b][...]
        z = _gelu_tanh(y)
        gate = _sigmoid(jnp.dot(z.astype(BF16), wglu_ref[...], preferred_element_type=F32) + bglu_ref[...])
        y_ref[b] = (z * gate).astype(y_ref.dtype)


def _ssm(u, row0, nb, seq, lc, h0, lam, bbig, cbig, dvec, wglu, bglu):
    d_ssm = u.shape[1]
    n_state_rows = lam.shape[0]
    n_chunks = seq // lc
    blk0 = row0 // lc
    u_specs = [pl.BlockSpec((lc, d_ssm), functools.partial(lambda i, b: (blk0 + b * n_chunks + i, 0), b=b))
               for b in range(nb)]
    return pl.pallas_call(
        functools.partial(_ssm_body, nb=nb, lc=lc, n_state_rows=n_state_rows),
        grid=(n_chunks,),
        in_specs=u_specs + [_const_spec(h0.shape), _const_spec(lam.shape), _const_spec(bbig.shape),
                            _const_spec(cbig.shape), _const_spec(dvec.shape), _const_spec(wglu.shape),
                            _const_spec(bglu.shape)],
        out_specs=[pl.BlockSpec((nb, lc, d_ssm), lambda i: (0, i, 0)), _const_spec(h0.shape)],
        out_shape=[jax.ShapeDtypeStruct((nb, seq, d_ssm), BF16), jax.ShapeDtypeStruct(h0.shape, F32)],
        scratch_shapes=[pltpu.VMEM((nb, lc // SUBLANES, SUBLANES * n_state_rows, LANES), F32),
                        pltpu.VMEM(h0.shape, F32)],
        compiler_params=_cparams(1),
        name=f"ssm_b{nb}",
    )(*([u] * nb), h0, lam, bbig, cbig, dvec, wglu, bglu)


def _bias_rows(m_ref, lead, base, n_rows, n_cols):
    for r in range(n_rows):
        m_ref[lead, r:r + 1, :] = base[:, QBLK - 1 - r:QBLK - 1 - r + n_cols]


def _attn_prompt_body(q_ref, k_ref, v_ref, base_ref, o_ref, kb_ref, vb_ref, m_ref):
    j = pl.program_id(2)

    @pl.when(j == 0)
    def _():
        kb_ref[:BAND_ROWS, :] = jnp.zeros((BAND_ROWS, LANES), BF16)
        vb_ref[:BAND_ROWS, :] = jnp.zeros((BAND_ROWS, LANES), BF16)
        kb_ref[BAND_ROWS:, :] = k_ref[...].astype(BF16)
        vb_ref[BAND_ROWS:, :] = v_ref[...].astype(BF16)
        rq = lax.broadcasted_iota(I32, (QBLK, KWIN), 0) // CHUNK
        ck = lax.broadcasted_iota(I32, (QBLK, KWIN), 1) // CHUNK
        in_band = (ck >= rq) & (ck <= rq + LEFT_CHUNKS)
        for a in range(2):
            _bias_rows(m_ref, a, base_ref[0, a:a + 1, :], QBLK, KWIN)
            m_ref[a] = jnp.where(in_band, m_ref[a], NEG_INF)

    start = pl.multiple_of(j * (QSTEP * QBLK), QSTEP * QBLK)
    lane = lax.broadcasted_iota(I32, (QBLK, LANES), 1)
    col = lax.broadcasted_iota(I32, (2 * QBLK, KWIN), 1)
    bias = m_ref[...].reshape(2 * QBLK, KWIN)
    for c in range(QSTEP):
        q2 = q_ref[QBLK * c:QBLK * (c + 1), :]
        kw = kb_ref[pl.ds(start + QBLK * c, KWIN), :]
        vw = vb_ref[pl.ds(start + QBLK * c, KWIN), :]
        first_valid = BAND_ROWS - (j * QSTEP + c) * QBLK
        zero = jnp.zeros_like(q2)
        qs = jnp.concatenate([jnp.where(lane < HEAD_DIM, q2, zero), jnp.where(lane >= HEAD_DIM, q2, zero)], axis=0)
        s = lax.dot_general(qs, kw, (((1,), (1,)), ((), ())), preferred_element_type=F32)
        s = jnp.where(col >= first_valid, s + bias, NEG_INF)
        p = jnp.exp(s - jnp.max(s, axis=1, keepdims=True))
        o = jnp.dot(p.astype(BF16), vw, preferred_element_type=F32) * (1.0 / jnp.sum(p, axis=1, keepdims=True))
        o_ref[QBLK * c:QBLK * (c + 1), :] = jnp.where(lane < HEAD_DIM, o[:QBLK], o[QBLK:]).astype(o_ref.dtype)


def _attn_prompt(q, k, v, base2, nb, seq):
    d_attn = q.shape[1]
    pairs = d_attn // LANES
    nq = seq // (QSTEP * QBLK)
    kv_spec = pl.BlockSpec((seq, LANES), lambda b, p, j: (b, p))
    return pl.pallas_call(
        _attn_prompt_body,
        grid=(nb, pairs, nq),
        in_specs=[pl.BlockSpec((QSTEP * QBLK, LANES), lambda b, p, j: (b * nq + j, p)), kv_spec, kv_spec,
                  pl.BlockSpec((1, 2, base2.shape[2]), lambda b, p, j: (p, 0, 0))],
        out_specs=pl.BlockSpec((QSTEP * QBLK, LANES), lambda b, p, j: (b * nq + j, p)),
        out_shape=jax.ShapeDtypeStruct((nb * seq, d_attn), BF16),
        scratch_shapes=[pltpu.VMEM((seq + BAND_ROWS, LANES), BF16), pltpu.VMEM((seq + BAND_ROWS, LANES), BF16),
                        pltpu.VMEM((2, QBLK, KWIN), F32)],
        compiler_params=_cparams(3),
        name="attn_prompt",
    )(q, k, v, base2)


def _attn_sample_body(q_ref, k_ref, v_ref, ck_ref, cv_ref, base_ref, o_ref, m_ref, *, heads, n_cache):
    @pl.when(pl.program_id(0) == 0)
    def _():
        for h in range(heads):
            _bias_rows(m_ref, h, base_ref[h:h + 1, :], CHUNK, n_cache + CHUNK)

    q = q_ref[...]
    k = k_ref[...].astype(BF16)
    v = v_ref[...].astype(BF16)
    nt = (((1,), (1,)), ((), ()))
    outs = []
    for h in range(heads):
        sl = slice(HEAD_DIM * h, HEAD_DIM * (h + 1))
        qh = q[:, sl]
        s1 = lax.dot_general(qh, ck_ref[0, h].astype(BF16), nt, preferred_element_type=F32) + m_ref[h, :, :n_cache]
        s2 = lax.dot_general(qh, k[:, sl], nt, preferred_element_type=F32) + m_ref[h, :, n_cache:]
        mx = jnp.maximum(jnp.max(s1, axis=1, keepdims=True), jnp.max(s2, axis=1, keepdims=True))
        p1, p2 = jnp.exp(s1 - mx), jnp.exp(s2 - mx)
        den = jnp.sum(p1, axis=1, keepdims=True) + jnp.sum(p2, axis=1, keepdims=True)
        o = (jnp.dot(p1.astype(BF16), cv_ref[0, h].astype(BF16), preferred_element_type=F32)
             + jnp.dot(p2.astype(BF16), v[:, sl], preferred_element_type=F32))
        outs.append(o * (1.0 / den))
    o_ref[...] = jnp.concatenate(outs, axis=1).astype(o_ref.dtype)


def _attn_sample(q, k, v, cache_k, cache_v, base, row0):
    nb, heads, n_cache, _ = cache_k.shape
    d_attn = q.shape[1]
    blk0 = row0 // CHUNK
    row = pl.BlockSpec((CHUNK, d_attn), lambda b: (blk0 + b, 0))
    cache = pl.BlockSpec((1, heads, n_cache, HEAD_DIM), lambda b: (b, 0, 0, 0))
    return pl.pallas_call(
        functools.partial(_attn_sample_body, heads=heads, n_cache=n_cache),
        grid=(nb,),
        in_specs=[row, row, row, cache, cache, _const_spec(base.shape)],
        out_specs=pl.BlockSpec((CHUNK, d_attn), lambda b: (b, 0)),
        out_shape=jax.ShapeDtypeStruct((nb * CHUNK, d_attn), BF16),
        scratch_shapes=[pltpu.VMEM((heads, CHUNK, n_cache + CHUNK), F32)],
        compiler_params=_cparams(1),
        name="attn_sample",
    )(q, k, v, cache_k, cache_v, base)


def _bias_base(table):
    heads = table.shape[0]
    far = jnp.broadcast_to(table[:, 2 * MAX_REL:], (heads, KWIN - 1 - MAX_REL + 1))
    near = table[:, 2 * MAX_REL - 1:0:-1]
    base = jnp.concatenate([far, near], axis=1)
    return jnp.pad(base, ((0, 0), (0, KWIN + QBLK - base.shape[1])))


def _post_body(*refs, n_x, n_y, n_pb):
    i = pl.program_id(0)
    pos = 0

    def take(n):
        nonlocal pos
        out = refs[pos:pos + n]
        pos += n
        return out

    def merged(pair):
        if len(pair) == 1:
            return pair[0][...]
        return jnp.where(i < n_pb, pair[0][...], pair[1][...])

    x_refs = take(n_x)
    y_pairs = [take(2) for _ in range(n_y)]
    w_refs = take(n_y)
    g_ref, wr_ref, br_ref, tril_ref, cin_ref = take(5)
    xo_ref, mi_ref, mf_ref, cnt_ref = take(4)
    (carry_ref,) = take(1)

    x = merged(x_refs)
    for pair, w_ref in zip(y_pairs, w_refs):
        x = x + jnp.dot(merged(pair).astype(BF16), w_ref[...], preferred_element_type=F32)
    xo_ref[...] = x

    xn = _rms(x, g_ref[...])
    xh, xl = _split_bf16(xn)
    ph = jnp.dot(xh, wr_ref[...], preferred_element_type=F32)
    pl_ = jnp.dot(xl, wr_ref[...], preferred_element_type=F32)
    logits = (ph[:, :LANES] + pl_[:, :LANES] + ph[:, LANES:] + pl_[:, LANES:]) + br_ref[...]
    lane = lax.broadcasted_iota(I32, logits.shape, 1)

    def first_argmax(vals):
        top = jnp.max(vals, axis=1, keepdims=True)
        return top, jnp.min(jnp.where(vals == top, lane, LANES), axis=1, keepdims=True)

    is_group = lane < MOE_GROUPS
    g_top, g_idx = first_argmax(jnp.where(is_group, logits, -jnp.inf))
    g_w = 1.0 / jnp.sum(jnp.where(is_group, jnp.exp(logits - g_top), 0.0), axis=1, keepdims=True)
    first = MOE_GROUPS + EXPERTS_PER_GROUP * g_idx
    e_log = jnp.where((lane >= first) & (lane < first + EXPERTS_PER_GROUP), logits, -jnp.inf)
    v1, l1 = first_argmax(e_log)
    v2, l2 = first_argmax(jnp.where(lane == l1, -jnp.inf, e_log))
    t = jnp.exp(v2 - v1)
    w1 = 1.0 / (1.0 + t)
    w2 = t * w1
    e1, e2 = l1 - MOE_GROUPS, l2 - MOE_GROUPS

    @pl.when(i == 0)
    def _():
        carry_ref[...] = cin_ref[...]

    hit1, hit2 = lane == e1, lane == e2
    oh = jnp.where(hit1 | hit2, 1.0, 0.0)
    before = jnp.dot(tril_ref[...], oh.astype(BF16), preferred_element_type=F32) + carry_ref[...]
    r1 = jnp.sum(jnp.where(hit1, before, 0.0), axis=1, keepdims=True).astype(I32)
    r2 = jnp.sum(jnp.where(hit2, before, 0.0), axis=1, keepdims=True).astype(I32)
    carry_ref[...] = carry_ref[...] + jnp.sum(oh, axis=0, keepdims=True)
    cnt_ref[...] = carry_ref[...]

    mi_ref[...] = jnp.where(lane == 0, e1, jnp.where(lane == 1, e2, jnp.where(lane == 2, r1, r2)))
    mf_ref[...] = jnp.where(lane == 0, g_w * w1, g_w * w2)


def _post_mixer(x_src, y_pairs, ws, g, router, tril, t_p, t_s):
    wr, br = router
    d = ws[0].shape[1]
    n_pb = t_p // TM
    n_tot = (t_p + t_s) // TM
    row = lambda width: pl.BlockSpec((TM, width), lambda i: (i, 0))
    in_specs, args = [], []
    if len(x_src) == 1:
        in_specs.append(row(d))
    else:
        in_specs += _pair_specs(TM, d, n_pb)
    args += list(x_src)
    for yp, ys in y_pairs:
        in_specs += _pair_specs(TM, yp.shape[1], n_pb)
        args += [yp, ys]
    consts = list(ws) + [g.reshape(1, d), wr, br, tril, jnp.zeros((1, LANES), F32)]
    in_specs += [_const_spec(c.shape) for c in consts]
    t = t_p + t_s
    return pl.pallas_call(
        functools.partial(_post_body, n_x=len(x_src), n_y=len(y_pairs), n_pb=n_pb),
        grid=(n_tot,),
        in_specs=in_specs,
        out_specs=[row(d), row(LANES), row(LANES), _const_spec((1, LANES))],
        out_shape=[jax.ShapeDtypeStruct((t, d), F32), jax.ShapeDtypeStruct((t, LANES), I32), jax.ShapeDtypeStruct((t, LANES), F32),
                   jax.ShapeDtypeStruct((1, LANES), F32)],
        scratch_shapes=[pltpu.VMEM((1, LANES), F32)],
        compiler_params=_cparams(1),
        name="post_mixer",
    )(*args, *consts)


def _router_weights(gate_w, gate_b, router_w, router_b):
    d = gate_w.shape[0]
    w = jnp.concatenate([gate_w, router_w.transpose(1, 0, 2).reshape(d, N_EXPERTS)], axis=1)
    b = jnp.concatenate([gate_b, router_b.reshape(N_EXPERTS)])
    w = jnp.pad(w, ((0, 0), (0, LANES - w.shape[1])))
    b = jnp.pad(b, (0, LANES - b.shape[0])).reshape(1, LANES)
    hi, lo = _split_bf16(w)
    return jnp.concatenate([hi, lo], axis=1), b


def _dispatch_body(slots_ref, pad_pos_ref, pad_n_ref, x_ref, g_ref, xs_ref, stage_ref, zero_ref, sem, *, tm, tile):
    i = pl.program_id(0)

    def zero_copy(pos, n):
        return pltpu.make_async_copy(zero_ref.at[pl.ds(0, n)], xs_ref.at[pl.ds(pos, n)], sem.at[1])

    @pl.when(i == 0)
    def _():
        zero_ref[...] = jnp.zeros(zero_ref.shape, zero_ref.dtype)

        def tails(act):
            for e in range(N_EXPERTS):
                pos, n = pad_pos_ref[e], pad_n_ref[e]
                bit = tile // 2
                while bit >= 1:
                    pl.when((n & bit) != 0)(functools.partial(act, pos, bit))
                    pos = pos + (n & bit)
                    bit //= 2

        used_tiles = (pad_pos_ref[N_EXPERTS - 1] + pad_n_ref[N_EXPERTS - 1]) // tile
        all_tiles = xs_ref.shape[0] // tile
        tails(lambda pos, n: zero_copy(pos, n).start())
        lax.fori_loop(used_tiles, all_tiles, lambda tl, c: (zero_copy(tl * tile, tile).start(), c)[1], 0)
        tails(lambda pos, n: zero_copy(pos, n).wait())
        lax.fori_loop(used_tiles, all_tiles, lambda tl, c: (zero_copy(tl * tile, tile).wait(), c)[1], 0)

    xn = _rms(x_ref[...], g_ref[...])
    for c in range(ROWS_O):
        stage_ref[pl.ds(c, tm, stride=ROWS_O), :] = xn[:, LANES * c:LANES * (c + 1)]

    def start(r, c):
        src = stage_ref.at[pl.ds(pl.multiple_of(r * ROWS_O, ROWS_O), ROWS_O)]
        for k in range(2):
            slot = slots_ref[2 * (i * tm + r) + k]
            pltpu.make_async_copy(src, xs_ref.at[slot], sem.at[0]).start(priority=k)
        return c

    lax.fori_loop(0, tm, start, 0, unroll=8)
    pltpu.make_async_copy(xs_ref.at[pl.ds(0, 2 * tm)], xs_ref.at[pl.ds(0, 2 * tm)], sem.at[0]).wait()


def _dispatch(x, g, slots, pad_pos, pad_n, n_slots):
    t, d = x.shape
    slab = (ROWS_O, LANES)
    grid_spec = pltpu.PrefetchScalarGridSpec(
        num_scalar_prefetch=3,
        grid=(t // TM,),
        in_specs=[pl.BlockSpec((TM, d), lambda i, *_: (i, 0)), pl.BlockSpec((1, d), lambda i, *_: (0, 0))],
        out_specs=pl.BlockSpec(memory_space=pl.ANY),
        scratch_shapes=[pltpu.VMEM((TM * ROWS_O, LANES), F32), pltpu.VMEM((TM_MOE,) + slab, F32),
                        pltpu.SemaphoreType.DMA((2,))],
    )
    return pl.pallas_call(
        functools.partial(_dispatch_body, tm=TM, tile=TM_MOE),
        grid_spec=grid_spec,
        out_shape=jax.ShapeDtypeStruct((n_slots,) + slab, F32),
        compiler_params=_cparams(1),
        name="moe_dispatch",
    )(slots, pad_pos, pad_n, x, g.reshape(1, d))


def _moe_body(te_ref, nt_ref, x_ref, wgu_ref, wd_ref, o_ref, wgu_bf, wd_bf, *, tm):
    i = pl.program_id(0)
    prev = te_ref[jnp.maximum(i - 1, 0)]

    @pl.when((i == 0) | (te_ref[i] != prev))
    def _():
        wgu_bf[...] = wgu_ref[0, 0].astype(BF16)
        wd_bf[...] = wd_ref[0, 0].astype(BF16)

    @pl.when(i < nt_ref[0])
    def _():
        x = jnp.concatenate([x_ref[pl.ds(c, tm, stride=ROWS_O), :] for c in range(ROWS_O)], axis=1)
        h = jnp.dot(x.astype(BF16), wgu_bf[...], preferred_element_type=F32)
        d_exp = h.shape[1] // 2
        gt, up = h[:, :d_exp], h[:, d_exp:]
        act = (gt * _sigmoid(gt) * up).astype(BF16)
        out = jnp.dot(act, wd_bf[...], preferred_element_type=F32)
        for c in range(ROWS_O):
            o_ref[pl.ds(c, tm, stride=ROWS_O), :] = out[:, LANES * c:LANES * (c + 1)]

    @pl.when(i >= nt_ref[0])
    def _():
        o_ref[...] = jnp.zeros(o_ref.shape, o_ref.dtype)


def _moe_experts(xsorted, tile_expert, n_tiles, wgu, wd, layer):
    n_slots = xsorted.shape[0] // ROWS_O
    n_max = n_slots // TM_MOE
    grid_spec = pltpu.PrefetchScalarGridSpec(
        num_scalar_prefetch=2,
        grid=(n_max,),
        in_specs=[pl.BlockSpec((TM_MOE * ROWS_O, LANES), lambda i, te, nt: (jnp.minimum(i, nt[0] - 1), 0)),
                  pl.BlockSpec((1, 1) + wgu.shape[2:], lambda i, te, nt: (layer, te[i], 0, 0)),
                  pl.BlockSpec((1, 1) + wd.shape[2:], lambda i, te, nt: (layer, te[i], 0, 0))],
        out_specs=pl.BlockSpec((TM_MOE * ROWS_O, LANES), lambda i, te, nt: (i, 0)),
        scratch_shapes=[pltpu.VMEM(wgu.shape[2:], BF16), pltpu.VMEM(wd.shape[2:], BF16)],
    )
    return pl.pallas_call(
        functools.partial(_moe_body, tm=TM_MOE),
        grid_spec=grid_spec,
        out_shape=jax.ShapeDtypeStruct((n_slots * ROWS_O, LANES), F32),
        compiler_params=_cparams(1),
        name="moe_experts",
    )(tile_expert, n_tiles, xsorted, wgu, wd)


def _combine_body(slots_ref, x_ref, mf_ref, os_ref, *rest, tm, n_steps, n_pb):
    out_refs, (buf_ref, sem) = rest[:-2], rest[-2:]
    i = pl.program_id(0)

    def issue(tile, par):
        def body(r, c):
            for k in range(2):
                slot = slots_ref[2 * (tile * tm + r) + k]
                dst = buf_ref.at[par, k, pl.ds(pl.multiple_of(r * ROWS_O, ROWS_O), ROWS_O)]
                pltpu.make_async_copy(os_ref.at[slot], dst, sem.at[par]).start(priority=k)
            return c
        lax.fori_loop(0, tm, body, 0, unroll=8)

    @pl.when(i == 0)
    def _():
        issue(0, 0)

    @pl.when(i + 1 < n_steps)
    def _():
        issue(i + 1, (i + 1) % 2)

    par = i % 2
    pltpu.make_async_copy(buf_ref.at[par], buf_ref.at[par], sem.at[par]).wait()

    def write(out_ref):
        mf = mf_ref[...]
        g1, g2 = mf[:, 0:1], mf[:, 1:2]
        for c in range(ROWS_O):
            cols = slice(LANES * c, LANES * (c + 1))
            piece = lambda k: buf_ref[par, k, pl.ds(c, tm, stride=ROWS_O), :]
            out_ref[:, cols] = x_ref[:, cols] + g1 * piece(0) + g2 * piece(1)

    if len(out_refs) == 1:
        write(out_refs[0])
    else:
        pl.when(i < n_pb)(lambda: write(out_refs[0]))
        pl.when(i >= n_pb)(lambda: write(out_refs[1]))


def _combine(x, meta_f, slots, osorted, t_p, split):
    t, d = x.shape
    n_steps = t // TM
    n_pb = t_p // TM
    row = lambda width: pl.BlockSpec((TM, width), lambda i, *_: (i, 0))
    if split:
        out_specs = [pl.BlockSpec((TM, d), lambda i, *_: (jnp.minimum(i, n_pb - 1), 0)),
                     pl.BlockSpec((TM, d), lambda i, *_: (jnp.maximum(i - n_pb, 0), 0))]
        out_shape = [jax.ShapeDtypeStruct((t_p, d), F32), jax.ShapeDtypeStruct((t - t_p, d), F32)]
    else:
        out_specs = [row(d)]
        out_shape = [jax.ShapeDtypeStruct((t, d), F32)]
    grid_spec = pltpu.PrefetchScalarGridSpec(
        num_scalar_prefetch=1,
        grid=(n_steps,),
        in_specs=[row(d), row(LANES), pl.BlockSpec(memory_space=pl.ANY)],
        out_specs=out_specs,
        scratch_shapes=[pltpu.VMEM((2, 2, TM * ROWS_O, LANES), F32), pltpu.SemaphoreType.DMA((2,))],
    )
    return pl.pallas_call(
        functools.partial(_combine_body, tm=TM, n_steps=n_steps, n_pb=n_pb),
        grid_spec=grid_spec,
        out_shape=out_shape,
        compiler_params=_cparams(1),
        name="moe_combine",
    )(slots, x, meta_f, osorted)


def _moe_layer(x, g, meta_i, meta_f, counts, wgu, wd, layer, t_p, split):
    t = x.shape[0]
    n_max = (2 * t + N_EXPERTS * (TM_MOE - 1)) // TM_MOE
    cnt = counts[0, :N_EXPERTS].astype(I32)
    padded = (cnt + TM_MOE - 1) // TM_MOE * TM_MOE
    ends = jnp.cumsum(padded)
    offs = ends - padded
    experts = jnp.arange(N_EXPERTS, dtype=I32)
    first_slot = jnp.sum(jnp.where(meta_i[:, 0:2, None] == experts, offs, 0), axis=-1)
    slots = (first_slot + meta_i[:, 2:4]).reshape(-1)
    n_tiles = (ends[-1] // TM_MOE).reshape(1)
    tile_start = jnp.arange(n_max, dtype=I32) * TM_MOE
    tile_expert = jnp.minimum(jnp.sum((ends[None, :] <= tile_start[:, None]).astype(I32), axis=1), N_EXPERTS - 1)
    n_slots = n_max * TM_MOE
    xsorted = _dispatch(x, g, slots, offs + cnt, padded - cnt, n_slots)
    osorted = _moe_experts(xsorted.reshape(n_slots * ROWS_O, LANES), tile_expert, n_tiles, wgu, wd, layer)
    return _combine(x, meta_f, slots, osorted.reshape(n_slots, ROWS_O, LANES), t_p, split)


def _conv_in_body(x_ref, g_ref, w_ref, u_ref):
    xn = _rms(x_ref[...], g_ref[...]).astype(BF16)
    h = jnp.dot(xn, w_ref[...], preferred_element_type=F32)
    c = h.shape[1] // 2
    u_ref[...] = h[:, :c] * _sigmoid(h[:, c:])


def _conv_in(x, g, w):
    t, d = x.shape
    c = w.shape[1] // 2
    return pl.pallas_call(
        _conv_in_body,
        grid=(t // TM,),
        in_specs=[pl.BlockSpec((TM, d), lambda i: (i, 0)), _const_spec((1, d)), _const_spec(w.shape)],
        out_specs=pl.BlockSpec((TM, c), lambda i: (i, 0)),
        out_shape=jax.ShapeDtypeStruct((t, c), F32),
        compiler_params=_cparams(1),
        name="conv_in",
    )(x, g.reshape(1, d), w.astype(BF16))


def _dwconv_body(cur_ref, prev_ref, left_ref, w_ref, b_ref, g_ref, beta_ref, o_ref, win_ref, y_ref, *, tm):
    i = pl.program_id(1)
    n_lt = cur_ref.shape[1] // LANES
    halo = jnp.where(i == 0, left_ref[0], prev_ref[...])
    cur = cur_ref[...]
    for c in range(n_lt):
        cols = slice(LANES * c, LANES * (c + 1))
        win_ref[pl.ds(c, CONV_HALO, stride=n_lt), :] = halo[:, cols]
        win_ref[pl.ds(CONV_HALO * n_lt + c, tm, stride=n_lt), :] = cur[:, cols]
    first_tap = CONV_HALO - (CONV_WIDTH - 1)
    taps = [w_ref[j] for j in range(CONV_WIDTH)]

    def block(rb, carry):
        for t in range(SUBLANES):
            frame = rb * SUBLANES + t
            acc = None
            for j in range(CONV_WIDTH):
                row = pl.multiple_of((frame + first_tap + j) * n_lt, n_lt)
                term = taps[j] * win_ref[pl.ds(row, n_lt), :]
                acc = term if acc is None else acc + term
            y_ref[pl.ds(pl.multiple_of(frame * n_lt, n_lt), n_lt), :] = acc
        return carry

    lax.fori_loop(0, tm // SUBLANES, block, 0)
    y = jnp.concatenate([y_ref[pl.ds(c, tm, stride=n_lt), :] for c in range(n_lt)], axis=1)
    y = y + b_ref[...]
    mu = jnp.mean(y, axis=1, keepdims=True)
    yc = y - mu
    yn = yc * lax.rsqrt(jnp.mean(yc * yc, axis=1, keepdims=True) + EPS) * g_ref[...] + beta_ref[...]
    o_ref[...] = (yn * _sigmoid(yn)).astype(BF16)


def _dwconv(u, left, dw_w, dw_b, ln_g, ln_b, row0, nb, seq, tm):
    c = u.shape[1]
    n_t = seq // tm
    blk0 = row0 // tm
    halo_per_tile = tm // CONV_HALO
    vec = lambda a: a.reshape(1, c)
    w = dw_w.reshape(CONV_WIDTH, c // LANES, LANES)
    return pl.pallas_call(
        functools.partial(_dwconv_body, tm=tm),
        grid=(nb, n_t),
        in_specs=[pl.BlockSpec((tm, c), lambda b, i: (blk0 + b * n_t + i, 0)),
                  pl.BlockSpec((CONV_HALO, c),
                               lambda b, i: (jnp.maximum((blk0 + b * n_t + i) * halo_per_tile - 1, 0), 0)),
                  pl.BlockSpec((1, CONV_HALO, c), lambda b, i: (b, 0, 0)),
                  _const_spec(w.shape), _const_spec((1, c)), _const_spec((1, c)), _const_spec((1, c))],
        out_specs=pl.BlockSpec((tm, c), lambda b, i: (b * n_t + i, 0)),
        out_shape=jax.ShapeDtypeStruct((nb * seq, c), BF16),
        scratch_shapes=[pltpu.VMEM(((tm + CONV_HALO) * (c // LANES), LANES), F32),
                        pltpu.VMEM((tm * (c // LANES), LANES), F32)],
        compiler_params=_cparams(2),
        name=f"dwconv_b{nb}",
    )(u, u, left, w, vec(dw_b), vec(ln_g), vec(ln_b))


def _block_diag(blocks):
    g, a, b = blocks.shape
    return jnp.einsum("gab,gh->gahb", blocks, jnp.eye(g, dtype=blocks.dtype)).reshape(g * a, g * b)


def _state_rows(re, im):
    nb = re.shape[0]
    return jnp.concatenate([re.reshape(nb, -1, LANES), im.reshape(nb, -1, LANES)], axis=1)


def kernel(x_prompt, x_sample, cache_attn_k, cache_attn_v, state_ssm_re, state_ssm_im, cache_conv, norm_mixer, norm_ffn, w_in_even, w_out_even, ssm_a_re, ssm_a_im, ssm_log_dt, ssm_b_re, ssm_b_im, ssm_c_re, ssm_c_im, ssm_d, ssm_glu_w, ssm_glu_b, attn_q_gain, attn_k_gain, attn_rel_bias, conv_w_in, conv_dw_w, conv_dw_b, conv_ln_g, conv_ln_b, conv_w_out, moe_gate_w, moe_gate_b, moe_router_w, moe_router_b, moe_w_gate_up, moe_w_down):
    nb_p, seq_p, d = x_prompt.shape
    nb_s, seq_s, _ = x_sample.shape
    t_p, t_s = nb_p * seq_p, nb_s * seq_s
    xp, xs = x_prompt.reshape(t_p, d), x_sample.reshape(t_s, d)
    depth = norm_mixer.shape[0]
    tril = jnp.tril(jnp.ones((TM, TM), F32), -1).astype(BF16)

    ks_p, vs_p, hs_p, ks_s, vs_s, hs_s, conv_p, conv_s = [], [], [], [], [], [], [], []
    x_src = (xp, xs)
    for layer in range(depth):
        j = layer // 2
        if layer % 2 == 0:
            u, q, k, v = _inproj(*x_src, norm_mixer[layer], w_in_even[j], attn_q_gain[j], attn_k_gain[j])
            d_ssm = u.shape[1]
            groups, n_state = ssm_a_re[j].shape
            bar_re, bar_im, bb_re, bb_im = _ssm_discretize(ssm_a_re[j], ssm_a_im[j], ssm_log_dt[j],
                                                           ssm_b_re[j], ssm_b_im[j])
            lam = jnp.concatenate([bar_re.reshape(-1, LANES), bar_im.reshape(-1, LANES)], axis=0)
            to_state = lambda m: _block_diag(m.reshape(groups, n_state, SSM_GROUP).transpose(0, 2, 1))
            bbig = jnp.concatenate([to_state(bb_re), to_state(bb_im)], axis=1).astype(BF16)
            cbig = jnp.concatenate([_block_diag(ssm_c_re[j].transpose(0, 2, 1)),
                                    _block_diag(-ssm_c_im[j].transpose(0, 2, 1))], axis=0).astype(BF16)
            n_q, k_q = 4, d_ssm // 2
            w_q = bbig.shape[1] // n_q
            blocks = lambda q: (slice(k_q * (q % 2), k_q * (q % 2 + 1)), slice(w_q * q, w_q * (q + 1)))
            bbig = jnp.stack([bbig[blocks(q)[0], blocks(q)[1]] for q in range(n_q)])
            cbig = jnp.stack([cbig[blocks(q)[1], blocks(q)[0]] for q in range(n_q)])
            ssm_args = (lam, bbig, cbig, ssm_d[j].reshape(1, d_ssm), _block_diag(ssm_glu_w[j]).astype(BF16),
                        ssm_glu_b[j].reshape(1, d_ssm))
            zero_state = jnp.zeros((nb_p, groups, n_state), F32)
            y_ssm_p, h_p = _ssm(u, 0, nb_p, seq_p, SSM_CHUNK, _state_rows(zero_state, zero_state), *ssm_args)
            y_ssm_s, h_s = _ssm(u, t_p, nb_s, seq_s, seq_s, _state_rows(state_ssm_re[j], state_ssm_im[j]), *ssm_args)
            base = _bias_base(attn_rel_bias[j])
            heads = base.shape[0]
            y_att_p = _attn_prompt(q, k, v, base.reshape(heads // 2, 2, -1), nb_p, seq_p)
            y_att_s = _attn_sample(q, k, v, cache_attn_k[j], cache_attn_v[j], base, t_p)
            w_out = w_out_even[j].astype(BF16)
            y_pairs = [(y_ssm_p.reshape(t_p, d_ssm), y_ssm_s.reshape(t_s, d_ssm)), (y_att_p, y_att_s)]
            ws = [w_out[:d_ssm], w_out[d_ssm:]]

            keep = min(BAND_ROWS, seq_p)
            heads_of = lambda a, nb: a.reshape(nb, -1, heads, HEAD_DIM).transpose(0, 2, 1, 3)
            tail_p = lambda a: jnp.concatenate([a[(b + 1) * seq_p - keep:(b + 1) * seq_p] for b in range(nb_p)])
            ks_p.append(heads_of(tail_p(k), nb_p))
            vs_p.append(heads_of(tail_p(v), nb_p))
            ks_s.append(heads_of(k[t_p:], nb_s))
            vs_s.append(heads_of(v[t_p:], nb_s))
            half = h_p.shape[1] // 2
            hs_p.append((h_p[:, :half].reshape(nb_p, groups, n_state), h_p[:, half:].reshape(nb_p, groups, n_state)))
            hs_s.append((h_s[:, :half].reshape(nb_s, groups, n_state), h_s[:, half:].reshape(nb_s, groups, n_state)))
        else:
            (x_merged,) = x_src
            uc = _conv_in(x_merged, norm_mixer[layer], conv_w_in[j])
            c = uc.shape[1]
            conv_args = (conv_dw_w[j], conv_dw_b[j], conv_ln_g[j], conv_ln_b[j])
            left_p = jnp.zeros((nb_p, CONV_HALO, c), F32)
            left_s = jnp.pad(cache_conv[j], ((0, 0), (CONV_HALO - (CONV_WIDTH - 1), 0), (0, 0)))
            act_p = _dwconv(uc, left_p, *conv_args, 0, nb_p, seq_p, TM_CONV)
            act_s = _dwconv(uc, left_s, *conv_args, t_p, nb_s, seq_s, seq_s)
            y_pairs = [(act_p, act_s)]
            ws = [conv_w_out[j].astype(BF16)]
            keep = CONV_WIDTH - 1
            conv_p.append(jnp.stack([uc[(b + 1) * seq_p - keep:(b + 1) * seq_p] for b in range(nb_p)]))
            ext_s = jnp.concatenate([cache_conv[j], uc[t_p:].reshape(nb_s, seq_s, c)], axis=1)
            conv_s.append(ext_s[:, ext_s.shape[1] - keep:])

        router = _router_weights(moe_gate_w[layer], moe_gate_b[layer], moe_router_w[layer], moe_router_b[layer])
        x_new, meta_i, meta_f, counts = _post_mixer(x_src, y_pairs, ws, norm_ffn[layer], router, tril, t_p, t_s)
        last = layer == depth - 1
        out = _moe_layer(x_new, norm_ffn[layer], meta_i, meta_f, counts, moe_w_gate_up, moe_w_down, layer, t_p, last)
        x_src = tuple(out)

    y_prompt, y_sample = x_src
    stack_c = lambda parts, idx: jnp.stack([p[idx] for p in parts])
    return (y_prompt.reshape(nb_p, seq_p, d), y_sample.reshape(nb_s, seq_s, d),
            jnp.stack(ks_p), jnp.stack(vs_p), stack_c(hs_p, 0), stack_c(hs_p, 1), jnp.stack(conv_p),
            jnp.stack(ks_s), jnp.stack(vs_s), stack_c(hs_s, 0), stack_c(hs_s, 1), jnp.stack(conv_s))
```

```python
import functools
import math

import jax
import jax.numpy as jnp
from jax import lax
from jax.experimental import pallas as pl
from jax.experimental.pallas import tpu as pltpu

F32, BF16, I32, U32 = jnp.float32, jnp.bfloat16, jnp.int32, jnp.uint32

CHUNK = 64
SSM_GROUP = 16
SSM_STATE = 64
HEAD_DIM = 64
LEFT_CHUNKS = 8
BAND_ROWS = LEFT_CHUNKS * CHUNK
MAX_REL = 128
CONV_WIDTH = 31
MOE_GROUPS = 4
EXPERTS_PER_GROUP = 8
N_EXPERTS = MOE_GROUPS * EXPERTS_PER_GROUP
EPS = 1e-6
NEG_INF = -1e30

LANES = 128
SUBLANES = 8
VMEM_LIMIT_BYTES = 56 * 1024 * 1024

TM = 512
TM_MOE = 512
QBLK = 2 * CHUNK
KWIN = QBLK + BAND_ROWS
QSTEP = 16
SSM_CHUNK = 512
D_MODEL = 1024
ROWS_O = D_MODEL // LANES
CONV_HALO = 32
TM_CONV = 512


def _cparams(n_grid):
    return pltpu.CompilerParams(dimension_semantics=("arbitrary",) * n_grid, vmem_limit_bytes=VMEM_LIMIT_BYTES)


def _sigmoid(x):
    return 1.0 / (1.0 + jnp.exp(-x))


def _rms(x, g):
    return x * lax.rsqrt(jnp.mean(x * x, axis=-1, keepdims=True) + EPS) * g


def _split_bf16(x):
    hi = lax.bitcast_convert_type(lax.bitcast_convert_type(x, U32) & jnp.uint32(0xFFFF0000), F32)
    return hi.astype(BF16), (x - hi).astype(BF16)


def _pair_specs(tm, width, n_prompt_blocks):
    return [pl.BlockSpec((tm, width), lambda i: (jnp.minimum(i, n_prompt_blocks - 1), 0)),
            pl.BlockSpec((tm, width), lambda i: (jnp.maximum(i - n_prompt_blocks, 0), 0))]


def _const_spec(shape):
    nd = len(shape)
    return pl.BlockSpec(shape, lambda *a: (0,) * nd)


def _inproj_body(xp_ref, xs_ref, g_ref, w_ref, qg_ref, kg_ref, seg_ref, u_ref, q_ref, k_ref, v_ref, *, n_pb, d_half):
    i = pl.program_id(0)
    x = jnp.where(i < n_pb, xp_ref[...], xs_ref[...])
    xn = _rms(x, g_ref[...]).astype(BF16)
    proj = jnp.dot(xn, w_ref[...], preferred_element_type=F32)
    u_ref[...] = proj[:, :d_half]
    v_ref[...] = proj[:, 3 * d_half:]

    def head_norm(t, gain):
        hi, lo = _split_bf16(t * t)
        msq = (jnp.dot(hi, seg_ref[...], preferred_element_type=F32)
               + jnp.dot(lo, seg_ref[...], preferred_element_type=F32))
        return t * lax.rsqrt(msq + EPS) * gain

    q_ref[...] = (head_norm(proj[:, d_half:2 * d_half], qg_ref[...]) * (HEAD_DIM ** -0.5)).astype(BF16)
    k_ref[...] = head_norm(proj[:, 2 * d_half:3 * d_half], kg_ref[...])


def _inproj(xp, xs, g, w, qg, kg):
    t_p, d = xp.shape
    t_s = xs.shape[0]
    d_half = w.shape[1] // 4
    n_pb = t_p // TM
    n_tot = (t_p + t_s) // TM
    heads = d_half // HEAD_DIM
    seg = jnp.kron(jnp.eye(heads, dtype=F32), jnp.full((HEAD_DIM, HEAD_DIM), 1.0 / HEAD_DIM, F32)).astype(BF16)
    out_f = jax.ShapeDtypeStruct((t_p + t_s, d_half), F32)
    out_b = jax.ShapeDtypeStruct((t_p + t_s, d_half), BF16)
    row = pl.BlockSpec((TM, d_half), lambda i: (i, 0))
    return pl.pallas_call(
        functools.partial(_inproj_body, n_pb=n_pb, d_half=d_half),
        grid=(n_tot,),
        in_specs=_pair_specs(TM, d, n_pb) + [_const_spec((1, d)), _const_spec(w.shape), _const_spec((1, d_half)),
                                             _const_spec((1, d_half)), _const_spec(seg.shape)],
        out_specs=[row, row, row, row],
        out_shape=[out_f, out_b, out_f, out_f],
        compiler_params=_cparams(1),
        name="inproj_even",
    )(xp, xs, g.reshape(1, d), w.astype(BF16), jnp.tile(qg, heads).reshape(1, d_half),
      jnp.tile(kg, heads).reshape(1, d_half), seg)


def _ssm_disc_body(are_ref, aim_ref, ldt_ref, bre_ref, bim_ref, barre_ref, barim_ref, bbre_ref, bbim_ref):
    lam_re, lam_im = are_ref[...], aim_ref[...]
    dt = jnp.exp(ldt_ref[...])
    mag = jnp.exp(lam_re * dt)
    bar_re, bar_im = mag * jnp.cos(lam_im * dt), mag * jnp.sin(lam_im * dt)
    den = lam_re * lam_re + lam_im * lam_im
    num_re = bar_re - 1.0
    coef_re = (num_re * lam_re + bar_im * lam_im) / den
    coef_im = (bar_im * lam_re - num_re * lam_im) / den
    br, bi = bre_ref[...], bim_ref[...]
    barre_ref[...] = bar_re
    barim_ref[...] = bar_im
    bbre_ref[...] = coef_re * br - coef_im * bi
    bbim_ref[...] = coef_re * bi + coef_im * br


def _ssm_discretize(a_re, a_im, log_dt, b_re, b_im):
    g, p = a_re.shape
    n = g * p
    col = jax.ShapeDtypeStruct((n, 1), F32)
    mat = jax.ShapeDtypeStruct((n, SSM_GROUP), F32)
    return pl.pallas_call(_ssm_disc_body, out_shape=[col, col, mat, mat], name="ssm_discretize")(
        a_re.reshape(n, 1), a_im.reshape(n, 1), jnp.repeat(log_dt, p).reshape(n, 1),
        b_re.reshape(n, SSM_GROUP), b_im.reshape(n, SSM_GROUP))


def _gelu_tanh(x):
    return 0.5 * x * (1.0 + jnp.tanh(math.sqrt(2.0 / math.pi) * (x + 0.044715 * (x * x * x))))


def _ssm_body(*refs, nb, lc, n_state_rows):
    u_refs = refs[:nb]
    h0_ref, lam_ref, bq_ref, cq_ref, d_ref, wglu_ref, bglu_ref, y_ref, ht_ref, s_ref, hcar_ref = refs[nb:]
    i = pl.program_id(0)
    half = n_state_rows // 2
    n_q = bq_ref.shape[0]
    tiles_q = n_state_rows // n_q
    k_q = bq_ref.shape[1]

    @pl.when(i == 0)
    def _():
        hcar_ref[...] = h0_ref[...]

    def tile_rows(tile):
        return slice(SUBLANES * tile, SUBLANES * (tile + 1))

    for b in range(nb):
        ub = u_refs[b][...].astype(BF16)
        for q in range(n_q):
            m = q % (n_q // 2)
            res = jnp.dot(ub[:, k_q * m:k_q * (m + 1)], bq_ref[q], preferred_element_type=F32)
            for c in range(tiles_q):
                s_ref[b, :, tile_rows(q * tiles_q + c), :] = (
                    res[:, LANES * c:LANES * (c + 1)].reshape(lc // SUBLANES, SUBLANES, LANES))

    lam_re, lam_im = lam_ref[:half, :], lam_ref[half:, :]

    def block(r, hs):
        hs = list(hs)
        for t in range(SUBLANES):
            for b in range(nb):
                bu = s_ref[b, r, pl.ds(t, n_state_rows, stride=SUBLANES), :]
                hr, hi = hs[2 * b], hs[2 * b + 1]
                nr = lam_re * hr - lam_im * hi + bu[:half]
                ni = lam_re * hi + lam_im * hr + bu[half:]
                s_ref[b, r, pl.ds(t, n_state_rows, stride=SUBLANES), :] = jnp.concatenate([nr, ni], axis=0)
                hs[2 * b], hs[2 * b + 1] = nr, ni
        return tuple(hs)

    init = []
    for b in range(nb):
        init += [hcar_ref[b, :half, :], hcar_ref[b, half:, :]]
    fin = lax.fori_loop(0, lc // SUBLANES, block, tuple(init))
    for b in range(nb):
        hb = jnp.concatenate([fin[2 * b], fin[2 * b + 1]], axis=0)
        hcar_ref[b] = hb
        ht_ref[b] = hb

    for b in range(nb):
        accs = []
        for m in range(n_q // 2):
            acc = None
            for q in (m, m + n_q // 2):
                tiles = [s_ref[b, :, tile_rows(q * tiles_q + c), :].reshape(lc, LANES) for c in range(tiles_q)]
                part = jnp.dot(jnp.concatenate(tiles, axis=1).astype(BF16), cq_ref[q], preferred_element_type=F32)
                acc = part if acc is None else acc + part
            accs.append(acc)
        y = jnp.concatenate(accs, axis=1) + d_ref[...] * u_refs[b][...]
        z = _gelu_tanh(y)
        gate = _sigmoid(jnp.dot(z.astype(BF16), wglu_ref[...], preferred_element_type=F32) + bglu_ref[...])
        y_ref[b] = (z * gate).astype(y_ref.dtype)


def _ssm(u, row0, nb, seq, lc, h0, lam, bbig, cbig, dvec, wglu, bglu):
    d_ssm = u.shape[1]
    n_state_rows = lam.shape[0]
    n_chunks = seq // lc
    blk0 = row0 // lc
    u_specs = [pl.BlockSpec((lc, d_ssm), functools.partial(lambda i, b: (blk0 + b * n_chunks + i, 0), b=b))
               for b in range(nb)]
    return pl.pallas_call(
        functools.partial(_ssm_body, nb=nb, lc=lc, n_state_rows=n_state_rows),
        grid=(n_chunks,),
        in_specs=u_specs + [_const_spec(h0.shape), _const_spec(lam.shape), _const_spec(bbig.shape),
                            _const_spec(cbig.shape), _const_spec(dvec.shape), _const_spec(wglu.shape),
                            _const_spec(bglu.shape)],
        out_specs=[pl.BlockSpec((nb, lc, d_ssm), lambda i: (0, i, 0)), _const_spec(h0.shape)],
        out_shape=[jax.ShapeDtypeStruct((nb, seq, d_ssm), BF16), jax.ShapeDtypeStruct(h0.shape, F32)],
        scratch_shapes=[pltpu.VMEM((nb, lc // SUBLANES, SUBLANES * n_state_rows, LANES), F32),
                        pltpu.VMEM(h0.shape, F32)],
        compiler_params=_cparams(1),
        name=f"ssm_b{nb}",
    )(*([u] * nb), h0, lam, bbig, cbig, dvec, wglu, bglu)


def _bias_rows(m_ref, lead, base, n_rows, n_cols):
    for r in range(n_rows):
        m_ref[lead, r:r + 1, :] = base[:, QBLK - 1 - r:QBLK - 1 - r + n_cols]


def _attn_prompt_body(q_ref, k_ref, v_ref, base_ref, o_ref, kb_ref, vb_ref, m_ref):
    j = pl.program_id(2)

    @pl.when(j == 0)
    def _():
        kb_ref[:BAND_ROWS, :] = jnp.zeros((BAND_ROWS, LANES), BF16)
        vb_ref[:BAND_ROWS, :] = jnp.zeros((BAND_ROWS, LANES), BF16)
        kb_ref[BAND_ROWS:, :] = k_ref[...].astype(BF16)
        vb_ref[BAND_ROWS:, :] = v_ref[...].astype(BF16)
        rq = lax.broadcasted_iota(I32, (QBLK, KWIN), 0) // CHUNK
        ck = lax.broadcasted_iota(I32, (QBLK, KWIN), 1) // CHUNK
        in_band = (ck >= rq) & (ck <= rq + LEFT_CHUNKS)
        for a in range(2):
            _bias_rows(m_ref, a, base_ref[0, a:a + 1, :], QBLK, KWIN)
            m_ref[a] = jnp.where(in_band, m_ref[a], NEG_INF)

    start = pl.multiple_of(j * (QSTEP * QBLK), QSTEP * QBLK)
    lane = lax.broadcasted_iota(I32, (QBLK, LANES), 1)
    col = lax.broadcasted_iota(I32, (2 * QBLK, KWIN), 1)
    bias = m_ref[...].reshape(2 * QBLK, KWIN)
    for c in range(QSTEP):
        q2 = q_ref[QBLK * c:QBLK * (c + 1), :]
        kw = kb_ref[pl.ds(start + QBLK * c, KWIN), :]
        vw = vb_ref[pl.ds(start + QBLK * c, KWIN), :]
        first_valid = BAND_ROWS - (j * QSTEP + c) * QBLK
        zero = jnp.zeros_like(q2)
        qs = jnp.concatenate([jnp.where(lane < HEAD_DIM, q2, zero), jnp.where(lane >= HEAD_DIM, q2, zero)], axis=0)
        s = lax.dot_general(qs, kw, (((1,), (1,)), ((), ())), preferred_element_type=F32)
        s = jnp.where(col >= first_valid, s + bias, NEG_INF)
        p = jnp.exp(s - jnp.max(s, axis=1, keepdims=True))
        o = jnp.dot(p.astype(BF16), vw, preferred_element_type=F32) * (1.0 / jnp.sum(p, axis=1, keepdims=True))
        o_ref[QBLK * c:QBLK * (c + 1), :] = jnp.where(lane < HEAD_DIM, o[:QBLK], o[QBLK:]).astype(o_ref.dtype)


def _attn_prompt(q, k, v, base2, nb, seq):
    d_attn = q.shape[1]
    pairs = d_attn // LANES
    nq = seq // (QSTEP * QBLK)
    kv_spec = pl.BlockSpec((seq, LANES), lambda b, p, j: (b, p))
    return pl.pallas_call(
        _attn_prompt_body,
        grid=(nb, pairs, nq),
        in_specs=[pl.BlockSpec((QSTEP * QBLK, LANES), lambda b, p, j: (b * nq + j, p)), kv_spec, kv_spec,
                  pl.BlockSpec((1, 2, base2.shape[2]), lambda b, p, j: (p, 0, 0))],
        out_specs=pl.BlockSpec((QSTEP * QBLK, LANES), lambda b, p, j: (b * nq + j, p)),
        out_shape=jax.ShapeDtypeStruct((nb * seq, d_attn), BF16),
        scratch_shapes=[pltpu.VMEM((seq + BAND_ROWS, LANES), BF16), pltpu.VMEM((seq + BAND_ROWS, LANES), BF16),
                        pltpu.VMEM((2, QBLK, KWIN), F32)],
        compiler_params=_cparams(3),
        name="attn_prompt",
    )(q, k, v, base2)


def _attn_sample_body(q_ref, k_ref, v_ref, ck_ref, cv_ref, base_ref, o_ref, m_ref, *, heads, n_cache):
    @pl.when(pl.program_id(0) == 0)
    def _():
        for h in range(heads):
            _bias_rows(m_ref, h, base_ref[h:h + 1, :], CHUNK, n_cache + CHUNK)

    q = q_ref[...]
    k = k_ref[...].astype(BF16)
    v = v_ref[...].astype(BF16)
    nt = (((1,), (1,)), ((), ()))
    outs = []
    for h in range(heads):
        sl = slice(HEAD_DIM * h, HEAD_DIM * (h + 1))
        qh = q[:, sl]
        s1 = lax.dot_general(qh, ck_ref[0, h].astype(BF16), nt, preferred_element_type=F32) + m_ref[h, :, :n_cache]
        s2 = lax.dot_general(qh, k[:, sl], nt, preferred_element_type=F32) + m_ref[h, :, n_cache:]
        mx = jnp.maximum(jnp.max(s1, axis=1, keepdims=True), jnp.max(s2, axis=1, keepdims=True))
        p1, p2 = jnp.exp(s1 - mx), jnp.exp(s2 - mx)
        den = jnp.sum(p1, axis=1, keepdims=True) + jnp.sum(p2, axis=1, keepdims=True)
        o = (jnp.dot(p1.astype(BF16), cv_ref[0, h].astype(BF16), preferred_element_type=F32)
             + jnp.dot(p2.astype(BF16), v[:, sl], preferred_element_type=F32))
        outs.append(o * (1.0 / den))
    o_ref[...] = jnp.concatenate(outs, axis=1).astype(o_ref.dtype)


def _attn_sample(q, k, v, cache_k, cache_v, base, row0):
    nb, heads, n_cache, _ = cache_k.shape
    d_attn = q.shape[1]
    blk0 = row0 // CHUNK
    row = pl.BlockSpec((CHUNK, d_attn), lambda b: (blk0 + b, 0))
    cache = pl.BlockSpec((1, heads, n_cache, HEAD_DIM), lambda b: (b, 0, 0, 0))
    return pl.pallas_call(
        functools.partial(_attn_sample_body, heads=heads, n_cache=n_cache),
        grid=(nb,),
        in_specs=[row, row, row, cache, cache, _const_spec(base.shape)],
        out_specs=pl.BlockSpec((CHUNK, d_attn), lambda b: (b, 0)),
        out_shape=jax.ShapeDtypeStruct((nb * CHUNK, d_attn), BF16),
        scratch_shapes=[pltpu.VMEM((heads, CHUNK, n_cache + CHUNK), F32)],
        compiler_params=_cparams(1),
        name="attn_sample",
    )(q, k, v, cache_k, cache_v, base)


def _bias_base(table):
    heads = table.shape[0]
    far = jnp.broadcast_to(table[:, 2 * MAX_REL:], (heads, KWIN - 1 - MAX_REL + 1))
    near = table[:, 2 * MAX_REL - 1:0:-1]
    base = jnp.concatenate([far, near], axis=1)
    return jnp.pad(base, ((0, 0), (0, KWIN + QBLK - base.shape[1])))


def _post_body(*refs, n_x, n_y, n_pb):
    i = pl.program_id(0)
    pos = 0

    def take(n):
        nonlocal pos
        out = refs[pos:pos + n]
        pos += n
        return out

    def merged(pair):
        if len(pair) == 1:
            return pair[0][...]
        return jnp.where(i < n_pb, pair[0][...], pair[1][...])

    x_refs = take(n_x)
    y_pairs = [take(2) for _ in range(n_y)]
    w_refs = take(n_y)
    g_ref, wr_ref, br_ref, tril_ref, cin_ref = take(5)
    xo_ref, mi_ref, mf_ref, cnt_ref = take(4)
    (carry_ref,) = take(1)

    x = merged(x_refs)
    for pair, w_ref in zip(y_pairs, w_refs):
        x = x + jnp.dot(merged(pair).astype(BF16), w_ref[...], preferred_element_type=F32)
    xo_ref[...] = x

    xn = _rms(x, g_ref[...])
    xh, xl = _split_bf16(xn)
    ph = jnp.dot(xh, wr_ref[...], preferred_element_type=F32)
    pl_ = jnp.dot(xl, wr_ref[...], preferred_element_type=F32)
    logits = (ph[:, :LANES] + pl_[:, :LANES] + ph[:, LANES:] + pl_[:, LANES:]) + br_ref[...]
    lane = lax.broadcasted_iota(I32, logits.shape, 1)

    def first_argmax(vals):
        top = jnp.max(vals, axis=1, keepdims=True)
        return top, jnp.min(jnp.where(vals == top, lane, LANES), axis=1, keepdims=True)

    is_group = lane < MOE_GROUPS
    g_top, g_idx = first_argmax(jnp.where(is_group, logits, -jnp.inf))
    g_w = 1.0 / jnp.sum(jnp.where(is_group, jnp.exp(logits - g_top), 0.0), axis=1, keepdims=True)
    first = MOE_GROUPS + EXPERTS_PER_GROUP * g_idx
    e_log = jnp.where((lane >= first) & (lane < first + EXPERTS_PER_GROUP), logits, -jnp.inf)
    v1, l1 = first_argmax(e_log)
    v2, l2 = first_argmax(jnp.where(lane == l1, -jnp.inf, e_log))
    t = jnp.exp(v2 - v1)
    w1 = 1.0 / (1.0 + t)
    w2 = t * w1
    e1, e2 = l1 - MOE_GROUPS, l2 - MOE_GROUPS

    @pl.when(i == 0)
    def _():
        carry_ref[...] = cin_ref[...]

    hit1, hit2 = lane == e1, lane == e2
    oh = jnp.where(hit1 | hit2, 1.0, 0.0)
    before = jnp.dot(tril_ref[...], oh.astype(BF16), preferred_element_type=F32) + carry_ref[...]
    r1 = jnp.sum(jnp.where(hit1, before, 0.0), axis=1, keepdims=True).astype(I32)
    r2 = jnp.sum(jnp.where(hit2, before, 0.0), axis=1, keepdims=True).astype(I32)
    carry_ref[...] = carry_ref[...] + jnp.sum(oh, axis=0, keepdims=True)
    cnt_ref[...] = carry_ref[...]

    mi_ref[...] = jnp.where(lane == 0, e1, jnp.where(lane == 1, e2, jnp.where(lane == 2, r1, r2)))
    mf_ref[...] = jnp.where(lane == 0, g_w * w1, g_w * w2)


def _post_mixer(x_src, y_pairs, ws, g, router, tril, t_p, t_s):
    wr, br = router
    d = ws[0].shape[1]
    n_pb = t_p // TM
    n_tot = (t_p + t_s) // TM
    row = lambda width: pl.BlockSpec((TM, width), lambda i: (i, 0))
    in_specs, args = [], []
    if len(x_src) == 1:
        in_specs.append(row(d))
    else:
        in_specs += _pair_specs(TM, d, n_pb)
    args += list(x_src)
    for yp, ys in y_pairs:
        in_specs += _pair_specs(TM, yp.shape[1], n_pb)
        args += [yp, ys]
    consts = list(ws) + [g.reshape(1, d), wr, br, tril, jnp.zeros((1, LANES), F32)]
    in_specs += [_const_spec(c.shape) for c in consts]
    t = t_p + t_s
    return pl.pallas_call(
        functools.partial(_post_body, n_x=len(x_src), n_y=len(y_pairs), n_pb=n_pb),
        grid=(n_tot,),
        in_specs=in_specs,
        out_specs=[row(d), row(LANES), row(LANES), _const_spec((1, LANES))],
        out_shape=[jax.ShapeDtypeStruct((t, d), F32), jax.ShapeDtypeStruct((t, LANES), I32), jax.ShapeDtypeStruct((t, LANES), F32),
                   jax.ShapeDtypeStruct((1, LANES), F32)],
        scratch_shapes=[pltpu.VMEM((1, LANES), F32)],
        compiler_params=_cparams(1),
        name="post_mixer",
    )(*args, *consts)


def _router_weights(gate_w, gate_b, router_w, router_b):
    d = gate_w.shape[0]
    w = jnp.concatenate([gate_w, router_w.transpose(1, 0, 2).reshape(d, N_EXPERTS)], axis=1)
    b = jnp.concatenate([gate_b, router_b.reshape(N_EXPERTS)])
    w = jnp.pad(w, ((0, 0), (0, LANES - w.shape[1])))
    b = jnp.pad(b, (0, LANES - b.shape[0])).reshape(1, LANES)
    hi, lo = _split_bf16(w)
    return jnp.concatenate([hi, lo], axis=1), b


def _dispatch_body(slots_ref, pad_pos_ref, pad_n_ref, x_ref, g_ref, xs_ref, stage_ref, zero_ref, sem, *, tm, tile,
                   n_steps):
    i = pl.program_id(0)

    def zero_copy(pos, n):
        return pltpu.make_async_copy(zero_ref.at[pl.ds(0, n)], xs_ref.at[pl.ds(pos, n)], sem.at[2])

    @pl.when(i == 0)
    def _():
        zero_ref[...] = jnp.zeros(zero_ref.shape, zero_ref.dtype)

        def tails(act):
            for e in range(N_EXPERTS):
                pos, n = pad_pos_ref[e], pad_n_ref[e]
                bit = tile // 2
                while bit >= 1:
                    pl.when((n & bit) != 0)(functools.partial(act, pos, bit))
                    pos = pos + (n & bit)
                    bit //= 2

        used_tiles = (pad_pos_ref[N_EXPERTS - 1] + pad_n_ref[N_EXPERTS - 1]) // tile
        all_tiles = xs_ref.shape[0] // tile
        tails(lambda pos, n: zero_copy(pos, n).start())
        lax.fori_loop(used_tiles, all_tiles, lambda tl, c: (zero_copy(tl * tile, tile).start(), c)[1], 0)
        tails(lambda pos, n: zero_copy(pos, n).wait())
        lax.fori_loop(used_tiles, all_tiles, lambda tl, c: (zero_copy(tl * tile, tile).wait(), c)[1], 0)

    par = i % 2

    def wait_step(p):
        pltpu.make_async_copy(xs_ref.at[pl.ds(0, 2 * tm)], xs_ref.at[pl.ds(0, 2 * tm)], sem.at[p]).wait()

    pl.when(i >= 2)(lambda: wait_step(par))
    xn = _rms(x_ref[...], g_ref[...])
    for c in range(ROWS_O):
        stage_ref[par, pl.ds(c, tm, stride=ROWS_O), :] = xn[:, LANES * c:LANES * (c + 1)]

    def start(r, c):
        src = stage_ref.at[par, pl.ds(pl.multiple_of(r * ROWS_O, ROWS_O), ROWS_O)]
        for k in range(2):
            slot = slots_ref[2 * (i * tm + r) + k]
            pltpu.make_async_copy(src, xs_ref.at[slot], sem.at[par]).start(priority=k)
        return c

    lax.fori_loop(0, tm, start, 0, unroll=8)

    @pl.when(i == n_steps - 1)
    def _():
        if n_steps > 1:
            wait_step(1 - par)
        wait_step(par)


def _dispatch(x, g, slots, pad_pos, pad_n, n_slots):
    t, d = x.shape
    slab = (ROWS_O, LANES)
    grid_spec = pltpu.PrefetchScalarGridSpec(
        num_scalar_prefetch=3,
        grid=(t // TM,),
        in_specs=[pl.BlockSpec((TM, d), lambda i, *_: (i, 0)), pl.BlockSpec((1, d), lambda i, *_: (0, 0))],
        out_specs=pl.BlockSpec(memory_space=pl.ANY),
        scratch_shapes=[pltpu.VMEM((2, TM * ROWS_O, LANES), F32), pltpu.VMEM((TM_MOE,) + slab, F32),
                        pltpu.SemaphoreType.DMA((3,))],
    )
    return pl.pallas_call(
        functools.partial(_dispatch_body, tm=TM, tile=TM_MOE, n_steps=t // TM),
        grid_spec=grid_spec,
        out_shape=jax.ShapeDtypeStruct((n_slots,) + slab, F32),
        compiler_params=_cparams(1),
        name="moe_dispatch",
    )(slots, pad_pos, pad_n, x, g.reshape(1, d))


def _moe_body(te_ref, nt_ref, x_ref, wgu_ref, wd_ref, o_ref, wgu_bf, wd_bf, *, tm):
    i = pl.program_id(0)
    prev = te_ref[jnp.maximum(i - 1, 0)]

    @pl.when((i == 0) | (te_ref[i] != prev))
    def _():
        wgu_bf[...] = wgu_ref[0, 0].astype(BF16)
        wd_bf[...] = wd_ref[0, 0].astype(BF16)

    @pl.when(i < nt_ref[0])
    def _():
        x = jnp.concatenate([x_ref[pl.ds(c, tm, stride=ROWS_O), :] for c in range(ROWS_O)], axis=1)
        h = jnp.dot(x.astype(BF16), wgu_bf[...], preferred_element_type=F32)
        d_exp = h.shape[1] // 2
        gt, up = h[:, :d_exp], h[:, d_exp:]
        act = (gt * _sigmoid(gt) * up).astype(BF16)
        out = jnp.dot(act, wd_bf[...], preferred_element_type=F32)
        for c in range(ROWS_O):
            o_ref[pl.ds(c, tm, stride=ROWS_O), :] = out[:, LANES * c:LANES * (c + 1)]

    @pl.when(i >= nt_ref[0])
    def _():
        o_ref[...] = jnp.zeros(o_ref.shape, o_ref.dtype)


def _moe_experts(xsorted, tile_expert, n_tiles, wgu, wd, layer):
    n_slots = xsorted.shape[0] // ROWS_O
    n_max = n_slots // TM_MOE
    grid_spec = pltpu.PrefetchScalarGridSpec(
        num_scalar_prefetch=2,
        grid=(n_max,),
        in_specs=[pl.BlockSpec((TM_MOE * ROWS_O, LANES), lambda i, te, nt: (jnp.minimum(i, nt[0] - 1), 0)),
                  pl.BlockSpec((1, 1) + wgu.shape[2:], lambda i, te, nt: (layer, te[i], 0, 0)),
                  pl.BlockSpec((1, 1) + wd.shape[2:], lambda i, te, nt: (layer, te[i], 0, 0))],
        out_specs=pl.BlockSpec((TM_MOE * ROWS_O, LANES), lambda i, te, nt: (i, 0)),
        scratch_shapes=[pltpu.VMEM(wgu.shape[2:], BF16), pltpu.VMEM(wd.shape[2:], BF16)],
    )
    return pl.pallas_call(
        functools.partial(_moe_body, tm=TM_MOE),
        grid_spec=grid_spec,
        out_shape=jax.ShapeDtypeStruct((n_slots * ROWS_O, LANES), F32),
        compiler_params=_cparams(1),
        name="moe_experts",
    )(tile_expert, n_tiles, xsorted, wgu, wd)


def _combine_body(slots_ref, x_ref, mf_ref, os_ref, *rest, tm, n_steps, n_pb):
    out_refs, (buf_ref, sem) = rest[:-2], rest[-2:]
    i = pl.program_id(0)

    def issue(tile, par):
        def body(r, c):
            for k in range(2):
                slot = slots_ref[2 * (tile * tm + r) + k]
                dst = buf_ref.at[par, k, pl.ds(pl.multiple_of(r * ROWS_O, ROWS_O), ROWS_O)]
                pltpu.make_async_copy(os_ref.at[slot], dst, sem.at[par]).start(priority=k)
            return c
        lax.fori_loop(0, tm, body, 0, unroll=8)

    @pl.when(i == 0)
    def _():
        issue(0, 0)

    @pl.when(i + 1 < n_steps)
    def _():
        issue(i + 1, (i + 1) % 2)

    par = i % 2
    pltpu.make_async_copy(buf_ref.at[par], buf_ref.at[par], sem.at[par]).wait()

    def write(out_ref):
        mf = mf_ref[...]
        g1, g2 = mf[:, 0:1], mf[:, 1:2]
        for c in range(ROWS_O):
            cols = slice(LANES * c, LANES * (c + 1))
            piece = lambda k: buf_ref[par, k, pl.ds(c, tm, stride=ROWS_O), :]
            out_ref[:, cols] = x_ref[:, cols] + g1 * piece(0) + g2 * piece(1)

    if len(out_refs) == 1:
        write(out_refs[0])
    else:
        pl.when(i < n_pb)(lambda: write(out_refs[0]))
        pl.when(i >= n_pb)(lambda: write(out_refs[1]))


def _combine(x, meta_f, slots, osorted, t_p, split):
    t, d = x.shape
    n_steps = t // TM
    n_pb = t_p // TM
    row = lambda width: pl.BlockSpec((TM, width), lambda i, *_: (i, 0))
    if split:
        out_specs = [pl.BlockSpec((TM, d), lambda i, *_: (jnp.minimum(i, n_pb - 1), 0)),
                     pl.BlockSpec((TM, d), lambda i, *_: (jnp.maximum(i - n_pb, 0), 0))]
        out_shape = [jax.ShapeDtypeStruct((t_p, d), F32), jax.ShapeDtypeStruct((t - t_p, d), F32)]
    else:
        out_specs = [row(d)]
        out_shape = [jax.ShapeDtypeStruct((t, d), F32)]
    grid_spec = pltpu.PrefetchScalarGridSpec(
        num_scalar_prefetch=1,
        grid=(n_steps,),
        in_specs=[row(d), row(LANES), pl.BlockSpec(memory_space=pl.ANY)],
        out_specs=out_specs,
        scratch_shapes=[pltpu.VMEM((2, 2, TM * ROWS_O, LANES), F32), pltpu.SemaphoreType.DMA((2,))],
    )
    return pl.pallas_call(
        functools.partial(_combine_body, tm=TM, n_steps=n_steps, n_pb=n_pb),
        grid_spec=grid_spec,
        out_shape=out_shape,
        compiler_params=_cparams(1),
        name="moe_combine",
    )(slots, x, meta_f, osorted)


def _moe_layer(x, g, meta_i, meta_f, counts, wgu, wd, layer, t_p, split):
    t = x.shape[0]
    n_max = (2 * t + N_EXPERTS * (TM_MOE - 1)) // TM_MOE
    cnt = counts[0, :N_EXPERTS].astype(I32)
    padded = (cnt + TM_MOE - 1) // TM_MOE * TM_MOE
    ends = jnp.cumsum(padded)
    offs = ends - padded
    experts = jnp.arange(N_EXPERTS, dtype=I32)
    first_slot = jnp.sum(jnp.where(meta_i[:, 0:2, None] == experts, offs, 0), axis=-1)
    slots = (first_slot + meta_i[:, 2:4]).reshape(-1)
    n_tiles = (ends[-1] // TM_MOE).reshape(1)
    tile_start = jnp.arange(n_max, dtype=I32) * TM_MOE
    tile_expert = jnp.minimum(jnp.sum((ends[None, :] <= tile_start[:, None]).astype(I32), axis=1), N_EXPERTS - 1)
    n_slots = n_max * TM_MOE
    xsorted = _dispatch(x, g, slots, offs + cnt, padded - cnt, n_slots)
    osorted = _moe_experts(xsorted.reshape(n_slots * ROWS_O, LANES), tile_expert, n_tiles, wgu, wd, layer)
    return _combine(x, meta_f, slots, osorted.reshape(n_slots, ROWS_O, LANES), t_p, split)


def _conv_in_body(x_ref, g_ref, w_ref, u_ref):
    xn = _rms(x_ref[...], g_ref[...]).astype(BF16)
    h = jnp.dot(xn, w_ref[...], preferred_element_type=F32)
    c = h.shape[1] // 2
    u_ref[...] = h[:, :c] * _sigmoid(h[:, c:])


def _conv_in(x, g, w):
    t, d = x.shape
    c = w.shape[1] // 2
    return pl.pallas_call(
        _conv_in_body,
        grid=(t // TM,),
        in_specs=[pl.BlockSpec((TM, d), lambda i: (i, 0)), _const_spec((1, d)), _const_spec(w.shape)],
        out_specs=pl.BlockSpec((TM, c), lambda i: (i, 0)),
        out_shape=jax.ShapeDtypeStruct((t, c), F32),
        compiler_params=_cparams(1),
        name="conv_in",
    )(x, g.reshape(1, d), w.astype(BF16))


def _dwconv_body(cur_ref, prev_ref, left_ref, w_ref, b_ref, g_ref, beta_ref, o_ref, win_ref, y_ref, *, tm):
    i = pl.program_id(1)
    n_lt = cur_ref.shape[1] // LANES
    halo = jnp.where(i == 0, left_ref[0], prev_ref[...])
    cur = cur_ref[...]
    for c in range(n_lt):
        cols = slice(LANES * c, LANES * (c + 1))
        win_ref[pl.ds(c, CONV_HALO, stride=n_lt), :] = halo[:, cols]
        win_ref[pl.ds(CONV_HALO * n_lt + c, tm, stride=n_lt), :] = cur[:, cols]
    first_tap = CONV_HALO - (CONV_WIDTH - 1)
    taps = [w_ref[j] for j in range(CONV_WIDTH)]

    def block(rb, carry):
        for t in range(SUBLANES):
            frame = rb * SUBLANES + t
            acc = None
            for j in range(CONV_WIDTH):
                row = pl.multiple_of((frame + first_tap + j) * n_lt, n_lt)
                term = taps[j] * win_ref[pl.ds(row, n_lt), :]
                acc = term if acc is None else acc + term
            y_ref[pl.ds(pl.multiple_of(frame * n_lt, n_lt), n_lt), :] = acc
        return carry

    lax.fori_loop(0, tm // SUBLANES, block, 0)
    y = jnp.concatenate([y_ref[pl.ds(c, tm, stride=n_lt), :] for c in range(n_lt)], axis=1)
    y = y + b_ref[...]
    mu = jnp.mean(y, axis=1, keepdims=True)
    yc = y - mu
    yn = yc * lax.rsqrt(jnp.mean(yc * yc, axis=1, keepdims=True) + EPS) * g_ref[...] + beta_ref[...]
    o_ref[...] = (yn * _sigmoid(yn)).astype(BF16)


def _dwconv(u, left, dw_w, dw_b, ln_g, ln_b, row0, nb, seq, tm):
    c = u.shape[1]
    n_t = seq // tm
    blk0 = row0 // tm
    halo_per_tile = tm // CONV_HALO
    vec = lambda a: a.reshape(1, c)
    w = dw_w.reshape(CONV_WIDTH, c // LANES, LANES)
    return pl.pallas_call(
        functools.partial(_dwconv_body, tm=tm),
        grid=(nb, n_t),
        in_specs=[pl.BlockSpec((tm, c), lambda b, i: (blk0 + b * n_t + i, 0)),
                  pl.BlockSpec((CONV_HALO, c),
                               lambda b, i: (jnp.maximum((blk0 + b * n_t + i) * halo_per_tile - 1, 0), 0)),
                  pl.BlockSpec((1, CONV_HALO, c), lambda b, i: (b, 0, 0)),
                  _const_spec(w.shape), _const_spec((1, c)), _const_spec((1, c)), _const_spec((1, c))],
        out_specs=pl.BlockSpec((tm, c), lambda b, i: (b * n_t + i, 0)),
        out_shape=jax.ShapeDtypeStruct((nb * seq, c), BF16),
        scratch_shapes=[pltpu.VMEM(((tm + CONV_HALO) * (c // LANES), LANES), F32),
                        pltpu.VMEM((tm * (c // LANES), LANES), F32)],
        compiler_params=_cparams(2),
        name=f"dwconv_b{nb}",
    )(u, u, left, w, vec(dw_b), vec(ln_g), vec(ln_b))


def _block_diag(blocks):
    g, a, b = blocks.shape
    return jnp.einsum("gab,gh->gahb", blocks, jnp.eye(g, dtype=blocks.dtype)).reshape(g * a, g * b)


def _state_rows(re, im):
    nb = re.shape[0]
    return jnp.concatenate([re.reshape(nb, -1, LANES), im.reshape(nb, -1, LANES)], axis=1)


def kernel(x_prompt, x_sample, cache_attn_k, cache_attn_v, state_ssm_re, state_ssm_im, cache_conv, norm_mixer, norm_ffn, w_in_even, w_out_even, ssm_a_re, ssm_a_im, ssm_log_dt, ssm_b_re, ssm_b_im, ssm_c_re, ssm_c_im, ssm_d, ssm_glu_w, ssm_glu_b, attn_q_gain, attn_k_gain, attn_rel_bias, conv_w_in, conv_dw_w, conv_dw_b, conv_ln_g, conv_ln_b, conv_w_out, moe_gate_w, moe_gate_b, moe_router_w, moe_router_b, moe_w_gate_up, moe_w_down):
    nb_p, seq_p, d = x_prompt.shape
    nb_s, seq_s, _ = x_sample.shape
    t_p, t_s = nb_p * seq_p, nb_s * seq_s
    xp, xs = x_prompt.reshape(t_p, d), x_sample.reshape(t_s, d)
    depth = norm_mixer.shape[0]
    tril = jnp.tril(jnp.ones((TM, TM), F32), -1).astype(BF16)

    ks_p, vs_p, hs_p, ks_s, vs_s, hs_s, conv_p, conv_s = [], [], [], [], [], [], [], []
    x_src = (xp, xs)
    for layer in range(depth):
        j = layer // 2
        if layer % 2 == 0:
            u, q, k, v = _inproj(*x_src, norm_mixer[layer], w_in_even[j], attn_q_gain[j], attn_k_gain[j])
            d_ssm = u.shape[1]
            groups, n_state = ssm_a_re[j].shape
            bar_re, bar_im, bb_re, bb_im = _ssm_discretize(ssm_a_re[j], ssm_a_im[j], ssm_log_dt[j],
                                                           ssm_b_re[j], ssm_b_im[j])
            lam = jnp.concatenate([bar_re.reshape(-1, LANES), bar_im.reshape(-1, LANES)], axis=0)
            to_state = lambda m: _block_diag(m.reshape(groups, n_state, SSM_GROUP).transpose(0, 2, 1))
            bbig = jnp.concatenate([to_state(bb_re), to_state(bb_im)], axis=1).astype(BF16)
            cbig = jnp.concatenate([_block_diag(ssm_c_re[j].transpose(0, 2, 1)),
                                    _block_diag(-ssm_c_im[j].transpose(0, 2, 1))], axis=0).astype(BF16)
            n_q, k_q = 4, d_ssm // 2
            w_q = bbig.shape[1] // n_q
            blocks = lambda q: (slice(k_q * (q % 2), k_q * (q % 2 + 1)), slice(w_q * q, w_q * (q + 1)))
            bbig = jnp.stack([bbig[blocks(q)[0], blocks(q)[1]] for q in range(n_q)])
            cbig = jnp.stack([cbig[blocks(q)[1], blocks(q)[0]] for q in range(n_q)])
            ssm_args = (lam, bbig, cbig, ssm_d[j].reshape(1, d_ssm), _block_diag(ssm_glu_w[j]).astype(BF16),
                        ssm_glu_b[j].reshape(1, d_ssm))
            zero_state = jnp.zeros((nb_p, groups, n_state), F32)
            y_ssm_p, h_p = _ssm(u, 0, nb_p, seq_p, SSM_CHUNK, _state_rows(zero_state, zero_state), *ssm_args)
            y_ssm_s, h_s = _ssm(u, t_p, nb_s, seq_s, seq_s, _state_rows(state_ssm_re[j], state_ssm_im[j]), *ssm_args)
            base = _bias_base(attn_rel_bias[j])
            heads = base.shape[0]
            y_att_p = _attn_prompt(q, k, v, base.reshape(heads // 2, 2, -1), nb_p, seq_p)
            y_att_s = _attn_sample(q, k, v, cache_attn_k[j], cache_attn_v[j], base, t_p)
            w_out = w_out_even[j].astype(BF16)
            y_pairs = [(y_ssm_p.reshape(t_p, d_ssm), y_ssm_s.reshape(t_s, d_ssm)), (y_att_p, y_att_s)]
            ws = [w_out[:d_ssm], w_out[d_ssm:]]

            keep = min(BAND_ROWS, seq_p)
            heads_of = lambda a, nb: a.reshape(nb, -1, heads, HEAD_DIM).transpose(0, 2, 1, 3)
            tail_p = lambda a: jnp.concatenate([a[(b + 1) * seq_p - keep:(b + 1) * seq_p] for b in range(nb_p)])
            ks_p.append(heads_of(tail_p(k), nb_p))
            vs_p.append(heads_of(tail_p(v), nb_p))
            ks_s.append(heads_of(k[t_p:], nb_s))
            vs_s.append(heads_of(v[t_p:], nb_s))
            half = h_p.shape[1] // 2
            hs_p.append((h_p[:, :half].reshape(nb_p, groups, n_state), h_p[:, half:].reshape(nb_p, groups, n_state)))
            hs_s.append((h_s[:, :half].reshape(nb_s, groups, n_state), h_s[:, half:].reshape(nb_s, groups, n_state)))
        else:
            (x_merged,) = x_src
            uc = _conv_in(x_merged, norm_mixer[layer], conv_w_in[j])
            c = uc.shape[1]
            conv_args = (conv_dw_w[j], conv_dw_b[j], conv_ln_g[j], conv_ln_b[j])
            left_p = jnp.zeros((nb_p, CONV_HALO, c), F32)
            left_s = jnp.pad(cache_conv[j], ((0, 0), (CONV_HALO - (CONV_WIDTH - 1), 0), (0, 0)))
            act_p = _dwconv(uc, left_p, *conv_args, 0, nb_p, seq_p, TM_CONV)
            act_s = _dwconv(uc, left_s, *conv_args, t_p, nb_s, seq_s, seq_s)
            y_pairs = [(act_p, act_s)]
            ws = [conv_w_out[j].astype(BF16)]
            keep = CONV_WIDTH - 1
            conv_p.append(jnp.stack([uc[(b + 1) * seq_p - keep:(b + 1) * seq_p] for b in range(nb_p)]))
            ext_s = jnp.concatenate([cache_conv[j], uc[t_p:].reshape(nb_s, seq_s, c)], axis=1)
            conv_s.append(ext_s[:, ext_s.shape[1] - keep:])

        router = _router_weights(moe_gate_w[layer], moe_gate_b[layer], moe_router_w[layer], moe_router_b[layer])
        x_new, meta_i, meta_f, counts = _post_mixer(x_src, y_pairs, ws, norm_ffn[layer], router, tril, t_p, t_s)
        last = layer == depth - 1
        out = _moe_layer(x_new, norm_ffn[layer], meta_i, meta_f, counts, moe_w_gate_up, moe_w_down, layer, t_p, last)
        x_src = tuple(out)

    y_prompt, y_sample = x_src
    stack_c = lambda parts, idx: jnp.stack([p[idx] for p in parts])
    return (y_prompt.reshape(nb_p, seq_p, d), y_sample.reshape(nb_s, seq_s, d),
            jnp.stack(ks_p), jnp.stack(vs_p), stack_c(hs_p, 0), stack_c(hs_p, 1), jnp.stack(conv_p),
            jnp.stack(ks_s), jnp.stack(vs_s), stack_c(hs_s, 0), stack_c(hs_s, 1), jnp.stack(conv_s))
```
